```python
import math
import jax, jax.numpy as jnp
from jax import lax
import numpy as np

D_MODEL = 1024
BATCH = 8
SEQ = 4096
DEPTH = 4

CTX_LEN = 256
GRID_W = 64
N_DN_HEADS = 4
DN_HEAD_DIM = 128
DN_WIDTH = N_DN_HEADS * DN_HEAD_DIM
N_FT_GROUPS = 4
FT_GROUP_DIM = 128
FT_WIDTH = N_FT_GROUPS * FT_GROUP_DIM
MIX_WIDTH = DN_WIDTH + FT_WIDTH
CONV_K = 5
CHUNK = 64
FT_OFF = 4 * DN_WIDTH + 4 * N_DN_HEADS
IN_WIDTH = FT_OFF + FT_WIDTH
N_EXPERTS = 16
N_EXPERT_GROUPS = 4
EXPERTS_PER_GROUP = N_EXPERTS // N_EXPERT_GROUPS
TOP_K = 2
D_EXPERT = 512
MOE_BLOCK = 256
POS_BASE = 10000.0
EPS = 1e-6

kernel_name = "hybrid_deltanet_fourier_moe_dit"

F32 = jnp.float32


def rmsnorm(x, w):
    xf = x.astype(F32)
    y = xf * lax.rsqrt(jnp.mean(xf * xf, axis=-1, keepdims=True) + EPS)
    return (y * w.astype(F32)).astype(x.dtype)


def modulate(h, shift, scale):
    return h * (1.0 + scale) + shift


def l2norm(t):
    return t * lax.rsqrt(jnp.sum(t * t, axis=-1, keepdims=True) + EPS)


def sincos_2d(length, dim, dtype):
    rows = length // GRID_W
    quarter = dim // 4
    omega = 1.0 / (POS_BASE ** (jnp.arange(quarter, dtype=F32) / quarter))
    ang_r = jnp.arange(rows, dtype=F32)[:, None] * omega
    ang_c = jnp.arange(GRID_W, dtype=F32)[:, None] * omega
    emb_r = jnp.concatenate([jnp.sin(ang_r), jnp.cos(ang_r)], axis=-1)
    emb_c = jnp.concatenate([jnp.sin(ang_c), jnp.cos(ang_c)], axis=-1)
    half = dim // 2
    emb = jnp.concatenate([jnp.broadcast_to(emb_r[:, None, :], (rows, GRID_W, half)),
                           jnp.broadcast_to(emb_c[None, :, :], (rows, GRID_W, half))], axis=-1)
    return emb.reshape(rows * GRID_W, dim).astype(dtype)


def short_conv(u, w):
    return lax.conv_general_dilated(u, w[:, None, :].astype(u.dtype), window_strides=(1,), padding='SAME',
                                    dimension_numbers=('NWC', 'WIO', 'NWC'),
                                    feature_group_count=u.shape[-1])


def dn_inputs(p, conv_w, a_log, dt_bias):
    b, l, _ = p.shape
    qkv = jax.nn.silu(short_conv(p[..., :3 * DN_WIDTH], conv_w)).astype(F32)
    q, k, v = jnp.split(qkv, 3, axis=-1)
    heads = lambda t: t.reshape(b, l, N_DN_HEADS, DN_HEAD_DIM).transpose(0, 2, 1, 3)
    q = l2norm(heads(q)) * (DN_HEAD_DIM ** -0.5)
    k = l2norm(heads(k))
    v = heads(v)
    ab = p[..., 4 * DN_WIDTH:FT_OFF].astype(F32).reshape(b, l, 2, 2, N_DN_HEADS)
    g = -jnp.exp(a_log.astype(F32)) * jax.nn.softplus(ab[..., 0, :] + dt_bias.astype(F32))
    beta = jax.nn.sigmoid(ab[..., 1, :])
    return q, k, v, g.transpose(2, 0, 3, 1), beta.transpose(2, 0, 3, 1)


def gated_delta_chunked(q, k, v, g, beta, s0):
    b, h, l, dk = q.shape
    dv = v.shape[-1]
    n = l // CHUNK
    chunks = lambda t: t.reshape(b, h, n, CHUNK, *t.shape[3:])
    q, k, v, g, beta = chunks(q), chunks(k), chunks(v), chunks(g), chunks(beta)
    g = jnp.cumsum(g, axis=-1)
    idx = jnp.arange(CHUNK)
    incl = idx[:, None] >= idx[None, :]
    strict = idx[:, None] > idx[None, :]
    decay = jnp.exp(jnp.where(incl, g[..., :, None] - g[..., None, :], -jnp.inf))
    kb = k * beta[..., None]
    a_mat = jnp.where(strict, jnp.einsum('bhnid,bhnjd->bhnij', kb, k) * decay, 0.0)
    lower = a_mat + jnp.eye(CHUNK, dtype=a_mat.dtype)
    rhs = jnp.concatenate([v * beta[..., None], kb * jnp.exp(g)[..., None]], axis=-1)
    sol = lax.linalg.triangular_solve(lower, rhs, left_side=True, lower=True, unit_diagonal=True)
    u, w = sol[..., :dv], sol[..., dv:]
    qk = jnp.where(incl, jnp.einsum('bhnid,bhnjd->bhnij', q, k) * decay, 0.0)
    q_dec = q * jnp.exp(g)[..., None]
    k_dec = k * jnp.exp(g[..., -1:] - g)[..., None]
    g_end = jnp.exp(g[..., -1])

    def step(s, inp):
        u_i, w_i, q_i, k_i, qk_i, ge_i = inp
        v_new = u_i - jnp.einsum('bhck,bhkv->bhcv', w_i, s)
        o_i = jnp.einsum('bhck,bhkv->bhcv', q_i, s) + jnp.einsum('bhcj,bhjv->bhcv', qk_i, v_new)
        s = s * ge_i[..., None, None] + jnp.einsum('bhck,bhcv->bhkv', k_i, v_new)
        return s, o_i

    xs = tuple(jnp.moveaxis(t, 2, 0) for t in (u, w, q_dec, k_dec, qk, g_end))
    s_fin, o = lax.scan(step, s0, xs)
    o = jnp.moveaxis(o, 0, 2).reshape(b, h, l, dv)
    return o, s_fin


def delta_bidir(q, k, v, g, beta, s0_f, s0_b):
    o_f, s_f = gated_delta_chunked(q, k, v, g[0], beta[0], s0_f)
    flip = lambda t: jnp.flip(t, axis=2)
    o_b, s_b = gated_delta_chunked(flip(q), flip(k), flip(v), flip(g[1]), flip(beta[1]), s0_b)
    return o_f + flip(o_b), s_f, s_b


def dn_output(o, z, norm_w):
    b, h, l, d = o.shape
    o = o.transpose(0, 2, 1, 3)
    o = o * lax.rsqrt(jnp.mean(o * o, axis=-1, keepdims=True) + EPS) * norm_w.astype(F32)
    o = o * jax.nn.silu(z.astype(F32).reshape(b, l, h, d))
    return o.reshape(b, l, h * d).astype(z.dtype)


def fourier_mix(f, norm_w):
    b, l, _ = f.shape
    fg = f.astype(F32).reshape(b, l, N_FT_GROUPS, FT_GROUP_DIM)
    y = jnp.real(jnp.fft.fft2(fg, axes=(1, 3), norm='ortho')).reshape(b, l, FT_WIDTH)
    return rmsnorm(y, norm_w).astype(f.dtype)


def mixer(h_lat, h_ctx, w_in, conv_w, a_log, dt_bias, dn_norm_w, ft_norm_w, w_out, ctx_out):
    p_l = h_lat @ w_in
    p_c = h_ctx @ (w_in if ctx_out else w_in[:, :FT_OFF])
    ql, kl, vl, gl, bl = dn_inputs(p_l, conv_w, a_log, dt_bias)
    qc, kc, vc, gc, bc = dn_inputs(p_c, conv_w, a_log, dt_bias)
    s0 = jnp.zeros((h_ctx.shape[0], N_DN_HEADS, DN_HEAD_DIM, DN_HEAD_DIM), F32)
    o_c, s_f, s_b = delta_bidir(qc, kc, vc, gc, bc, s0, s0)
    o_l, _, _ = delta_bidir(ql, kl, vl, gl, bl, s_f, s_b)

    def merge(p, o):
        dn = dn_output(o, p[..., 3 * DN_WIDTH:4 * DN_WIDTH], dn_norm_w)
        ft = fourier_mix(p[..., FT_OFF:], ft_norm_w)
        return jnp.concatenate([dn, ft], axis=-1) @ w_out

    out_l = merge(p_l, o_l)
    out_c = merge(p_c, o_c) if ctx_out else None
    return out_l, out_c


def moe(h, router_w, router_bias, w_gate, w_up, w_down):
    t, d = h.shape
    scores = jax.nn.sigmoid(h.astype(F32) @ router_w.astype(F32))
    biased = (scores + router_bias.astype(F32)).reshape(t, N_EXPERT_GROUPS, EXPERTS_PER_GROUP)
    group_score = jnp.sum(lax.top_k(biased, 2)[0], axis=-1)
    best_group = jnp.argmax(group_score, axis=-1).astype(jnp.int32)
    in_group = jnp.take_along_axis(biased, best_group[:, None, None], axis=1)[:, 0]
    _, local = lax.top_k(in_group, TOP_K)
    expert_idx = best_group[:, None] * EXPERTS_PER_GROUP + local.astype(jnp.int32)
    wts = jnp.take_along_axis(scores, expert_idx, axis=1)
    wts = wts / jnp.sum(wts, axis=-1, keepdims=True)

    a = t * TOP_K
    flat_e = expert_idx.reshape(a)
    flat_tok = jnp.arange(a, dtype=jnp.int32) // TOP_K
    flat_w = wts.reshape(a).astype(h.dtype)
    order = jnp.argsort(flat_e)
    se = flat_e[order]
    counts = jnp.zeros((N_EXPERTS,), jnp.int32).at[flat_e].add(1)
    padded = (counts + MOE_BLOCK - 1) // MOE_BLOCK * MOE_BLOCK
    pad_end = jnp.cumsum(padded)
    pad_start = pad_end - padded
    start = jnp.cumsum(counts) - counts
    dest = pad_start[se] + (jnp.arange(a, dtype=jnp.int32) - start[se])
    n_blk = (a + N_EXPERTS * (MOE_BLOCK - 1) + MOE_BLOCK - 1) // MOE_BLOCK
    p = n_blk * MOE_BLOCK
    slot_tok = jnp.full((p,), t, jnp.int32).at[dest].set(flat_tok[order])
    slot_w = jnp.zeros((p,), h.dtype).at[dest].set(flat_w[order])
    blk_expert = jnp.minimum(jnp.searchsorted(pad_end, jnp.arange(n_blk, dtype=jnp.int32) * MOE_BLOCK,
                                              side='right'), N_EXPERTS - 1)
    h_pad = jnp.concatenate([h, jnp.zeros((1, d), h.dtype)], axis=0)

    def expert_block(args):
        tok, gate, e = args
        xb = h_pad[tok]
        hid = jax.nn.silu(xb @ w_gate[e]) * (xb @ w_up[e])
        return (hid @ w_down[e]) * gate[:, None]

    ys = lax.map(expert_block, (slot_tok.reshape(n_blk, MOE_BLOCK), slot_w.reshape(n_blk, MOE_BLOCK), blk_expert))
    return jax.ops.segment_sum(ys.reshape(p, d), slot_tok, num_segments=t + 1)[:t]


def setup_inputs(seed: int = 0) -> dict:
    key = jax.random.key(seed)
    ks = jax.random.split(key, 21)
    nrm = lambda k, shape, s: jax.random.normal(k, shape, F32) * s
    x = nrm(ks[0], (BATCH, SEQ, D_MODEL), 1.0)
    c = nrm(ks[1], (BATCH, D_MODEL), 1.0)
    ctx = nrm(ks[2], (BATCH, CTX_LEN, D_MODEL), 1.0)
    c_ctx = nrm(ks[3], (D_MODEL,), 1.0)
    ada_w = nrm(ks[4], (DEPTH, D_MODEL, 6 * D_MODEL), 0.5 * D_MODEL ** -0.5)
    ada_b = nrm(ks[5], (DEPTH, 6 * D_MODEL), 0.02)
    norm_mix_w = 1.0 + nrm(ks[6], (DEPTH, D_MODEL), 0.02)
    norm_ffn_w = 1.0 + nrm(ks[7], (DEPTH, D_MODEL), 0.02)
    w_in = nrm(ks[8], (DEPTH, D_MODEL, IN_WIDTH), D_MODEL ** -0.5)
    conv_w = nrm(ks[9], (DEPTH, CONV_K, 3 * DN_WIDTH), CONV_K ** -0.5)
    a_log = jnp.log(jax.random.uniform(ks[10], (DEPTH, 2, N_DN_HEADS), F32, 1.0, 16.0))
    dt = jnp.exp(jax.random.uniform(ks[11], (DEPTH, 2, N_DN_HEADS), F32, math.log(1e-3), math.log(1e-1)))
    dt_bias = dt + jnp.log(-jnp.expm1(-dt))
    dn_norm_w = 1.0 + nrm(ks[12], (DEPTH, DN_HEAD_DIM), 0.02)
    ft_norm_w = 1.0 + nrm(ks[13], (DEPTH, FT_WIDTH), 0.02)
    w_out = nrm(ks[14], (DEPTH, MIX_WIDTH, D_MODEL), MIX_WIDTH ** -0.5)
    router_w = nrm(ks[15], (D_MODEL, N_EXPERTS), D_MODEL ** -0.5)
    router_bias = nrm(ks[16], (N_EXPERTS,), 0.01)
    w_gate = nrm(ks[17], (DEPTH, N_EXPERTS, D_MODEL, D_EXPERT), D_MODEL ** -0.5)
    w_up = nrm(ks[18], (DEPTH, N_EXPERTS, D_MODEL, D_EXPERT), D_MODEL ** -0.5)
    w_down = nrm(ks[19], (DEPTH, N_EXPERTS, D_EXPERT, D_MODEL), D_EXPERT ** -0.5)
    final_norm_w = 1.0 + nrm(ks[20], (D_MODEL,), 0.02)
    return {"x": x, "c": c, "ctx": ctx, "c_ctx": c_ctx, "ada_w": ada_w, "ada_b": ada_b,
            "norm_mix_w": norm_mix_w, "norm_ffn_w": norm_ffn_w, "w_in": w_in, "conv_w": conv_w,
            "a_log": a_log, "dt_bias": dt_bias, "dn_norm_w": dn_norm_w, "ft_norm_w": ft_norm_w,
            "w_out": w_out, "router_w": router_w, "router_bias": router_bias, "w_gate": w_gate,
            "w_up": w_up, "w_down": w_down, "final_norm_w": final_norm_w}


def reference(x, c, ctx, c_ctx, ada_w, ada_b, norm_mix_w, norm_ffn_w, w_in, conv_w, a_log, dt_bias,
              dn_norm_w, ft_norm_w, w_out, router_w, router_bias, w_gate, w_up, w_down, final_norm_w):
    b, l, d = x.shape
    lc = ctx.shape[1]
    xl = x + sincos_2d(l, d, x.dtype)[None]
    xc = ctx
    c_act = jax.nn.silu(c)
    cc_act = jax.nn.silu(c_ctx)
    for i in range(DEPTH):
        last = i == DEPTH - 1
        mod_l = (c_act @ ada_w[i] + ada_b[i])[:, None, :]
        mod_c = (cc_act @ ada_w[i] + ada_b[i])[None, None, :]
        sh1, sc1, g1, sh2, sc2, g2 = jnp.split(mod_l, 6, axis=-1)
        sh1c, sc1c, g1c, sh2c, sc2c, g2c = jnp.split(mod_c, 6, axis=-1)

        hl = modulate(rmsnorm(xl, norm_mix_w[i]), sh1, sc1)
        hc = modulate(rmsnorm(xc, norm_mix_w[i]), sh1c, sc1c)
        ol, oc = mixer(hl, hc, w_in[i], conv_w[i], a_log[i], dt_bias[i], dn_norm_w[i], ft_norm_w[i],
                       w_out[i], not last)
        xl = xl + g1 * ol
        hl2 = modulate(rmsnorm(xl, norm_ffn_w[i]), sh2, sc2)
        if last:
            y = moe(hl2.reshape(b * l, d), router_w, router_bias, w_gate[i], w_up[i], w_down[i])
            xl = xl + g2 * y.reshape(b, l, d)
        else:
            xc = xc + g1c * oc
            hc2 = modulate(rmsnorm(xc, norm_ffn_w[i]), sh2c, sc2c)
            tokens = jnp.concatenate([hl2.reshape(b * l, d), hc2.reshape(b * lc, d)], axis=0)
            y = moe(tokens, router_w, router_bias, w_gate[i], w_up[i], w_down[i])
            xl = xl + g2 * y[:b * l].reshape(b, l, d)
            xc = xc + g2c * y[b * l:].reshape(b, lc, d)
    return rmsnorm(xl, final_norm_w)
```

```python
import functools
import math

import jax
import jax.numpy as jnp
from jax import lax
from jax.experimental import pallas as pl
from jax.experimental.pallas import tpu as pltpu

F32 = jnp.float32
BF16 = jnp.bfloat16

GRID_W = 64
N_DN_HEADS = 4
DN_HEAD_DIM = 128
DN_WIDTH = N_DN_HEADS * DN_HEAD_DIM
N_FT_GROUPS = 4
FT_GROUP_DIM = 128
FT_WIDTH = N_FT_GROUPS * FT_GROUP_DIM
CONV_K = 5
CHUNK = 64
N_EXPERTS = 16
N_EXPERT_GROUPS = 4
EXPERTS_PER_GROUP = N_EXPERTS // N_EXPERT_GROUPS
D_EXPERT = 512
MOE_BLOCK = 256
POS_BASE = 10000.0
EPS = 1e-6
N_GATE_COLS = 4 * N_DN_HEADS
GATE_PAD = 128

VMEM_LIMIT = 56 * 1024 * 1024


def _params(sem, vmem=VMEM_LIMIT):
    return pltpu.CompilerParams(dimension_semantics=sem, vmem_limit_bytes=vmem)


def _dot(a, b):
    return jnp.dot(a.astype(BF16), b.astype(BF16), preferred_element_type=F32)


def _dot_nt(a, b):
    return lax.dot_general(a.astype(BF16), b.astype(BF16), (((1,), (1,)), ((), ())),
                           preferred_element_type=F32)


def _sigmoid(x):
    return 1.0 / (1.0 + jnp.exp(-x))


def _silu(x):
    return x * _sigmoid(x)


def _softplus(x):
    return jnp.maximum(x, 0.0) + jnp.log(1.0 + jnp.exp(-jnp.abs(x)))


def _ada_kernel(c_ref, w_ref, b_ref, o_ref):
    act = _silu(c_ref[...])
    o_ref[0] = jnp.dot(act, w_ref[0], preferred_element_type=F32,
                       precision=lax.Precision.HIGHEST) + b_ref[0]


def _ada(cvec, ada_w, ada_b):
    depth, d, n = ada_w.shape
    rows = cvec.shape[0]
    tn = 1024
    return pl.pallas_call(
        _ada_kernel,
        out_shape=jax.ShapeDtypeStruct((depth, rows, n), F32),
        grid=(depth, n // tn),
        in_specs=[pl.BlockSpec((rows, d), lambda i, j: (0, 0)),
                  pl.BlockSpec((1, d, tn), lambda i, j: (i, 0, j)),
                  pl.BlockSpec((1, 1, tn), lambda i, j: (i, 0, j))],
        out_specs=pl.BlockSpec((1, rows, tn), lambda i, j: (i, 0, j)),
        compiler_params=_params(("parallel", "parallel")),
        name="ada_mod",
    )(cvec, ada_w, ada_b.reshape(depth, 1, n))


def _seg_scan(x, pos, axis, reverse):
    n = x.shape[axis]
    s = 1
    while s < CHUNK:
        if reverse:
            shifted = pltpu.roll(x, n - s, axis)
            x = x + jnp.where(pos < CHUNK - s, shifted, 0.0)
        else:
            shifted = pltpu.roll(x, s, axis)
            x = x + jnp.where(pos >= s, shifted, 0.0)
        s *= 2
    return x


def _in_proj_kernel(tm, n_lat_tiles_per_b, n_b, x_ref, mod_ref, nw_ref, wqkv_ref, wz_ref, wft_ref,
                    wab_ref, wabt_ref, gpar_ref, gpart_ref, dft_ref,
                    qkv_ref, z_ref, xc_ref, xs_ref, gcol_ref, grow_ref):
    t = pl.program_id(0)
    bi = jnp.minimum(t // n_lat_tiles_per_b, n_b)
    d = x_ref.shape[1]
    x = x_ref[...]
    y = x * lax.rsqrt(jnp.mean(x * x, axis=-1, keepdims=True) + EPS) * nw_ref[...]
    shift = mod_ref[0, pl.ds(bi, 1), 0:d]
    scale = mod_ref[0, pl.ds(bi, 1), d:2 * d]
    h = (y * (1.0 + scale) + shift).astype(BF16)

    for j in range(3):
        cs = slice(j * DN_WIDTH, (j + 1) * DN_WIDTH)
        qkv_ref[:, cs] = jnp.dot(h, wqkv_ref[:, cs], preferred_element_type=F32)
    z_ref[...] = jnp.dot(h, wz_ref[...], preferred_element_type=F32)

    ft = jnp.dot(h, wft_ref[...], preferred_element_type=F32).astype(BF16)
    for g in range(N_FT_GROUPS):
        cs = slice(g * FT_GROUP_DIM, (g + 1) * FT_GROUP_DIM)
        cssn = jnp.dot(ft[:, cs], dft_ref[...], preferred_element_type=F32)
        xc_ref[:, cs] = cssn[:, :FT_GROUP_DIM].astype(BF16)
        xs_ref[:, cs] = cssn[:, FT_GROUP_DIM:].astype(BF16)

    ab = jnp.dot(h, wab_ref[...], preferred_element_type=F32)
    abt = lax.dot_general(wabt_ref[...], h, (((1,), (1,)), ((), ())), preferred_element_type=F32)

    def gates(v, par, axis):
        neg_a, dtb, is_a, is_bwd = par
        g = neg_a * _softplus(v + dtb)
        pos = lax.broadcasted_iota(jnp.int32, v.shape, axis) % CHUNK
        fwd = _seg_scan(g, pos, axis, reverse=False)
        bwd = _seg_scan(g, pos, axis, reverse=True)
        cum = jnp.where(is_bwd > 0.5, bwd, fwd)
        return jnp.where(is_a > 0.5, cum, _sigmoid(v))

    gp = gpar_ref[...]
    gc = gates(ab, (gp[0:1], gp[1:2], gp[2:3], gp[3:4]), 0)
    gpt = gpart_ref[...]
    gt = gates(abt, (gpt[:, 0:1], gpt[:, 1:2], gpt[:, 2:3], gpt[:, 3:4]), 1)
    for hh in range(N_DN_HEADS):
        gcol_ref[hh] = gc[:, 4 * hh:4 * hh + 4]
        for j in range(tm // CHUNK):
            grow_ref[hh, j] = gt[4 * hh:4 * hh + 4, j * CHUNK:(j + 1) * CHUNK]


def _in_proj(xs, mod_i, layer, norm_w, wqkv, wz, wft, wab, wabt, gpar, gpart, dft_cs, n_b, seq):
    t_rows, d = xs.shape
    tm = 512
    n_tiles = t_rows // tm
    kern = functools.partial(_in_proj_kernel, tm, seq // tm, n_b)
    const = lambda *shape: pl.BlockSpec(shape, lambda t: tuple(0 for _ in shape))
    rows = lambda w: pl.BlockSpec((tm, w), lambda t: (t, 0))
    return pl.pallas_call(
        kern,
        out_shape=(jax.ShapeDtypeStruct((t_rows, 3 * DN_WIDTH), F32),
                   jax.ShapeDtypeStruct((t_rows, DN_WIDTH), F32),
                   jax.ShapeDtypeStruct((t_rows, FT_WIDTH), BF16),
                   jax.ShapeDtypeStruct((t_rows, FT_WIDTH), BF16),
                   jax.ShapeDtypeStruct((N_DN_HEADS, t_rows, 4), F32),
                   jax.ShapeDtypeStruct((N_DN_HEADS, t_rows // CHUNK, 4, CHUNK), F32)),
        grid=(n_tiles,),
        in_specs=[rows(d),
                  pl.BlockSpec((1,) + mod_i.shape[1:], lambda t: (layer, 0, 0)),
                  const(1, d),
                  const(d, 3 * DN_WIDTH), const(d, DN_WIDTH), const(d, FT_WIDTH),
                  const(d, GATE_PAD), const(N_GATE_COLS, d),
                  const(4, GATE_PAD), const(N_GATE_COLS, 4),
                  const(FT_GROUP_DIM, 2 * FT_GROUP_DIM)],
        out_specs=(rows(3 * DN_WIDTH), rows(DN_WIDTH), rows(FT_WIDTH), rows(FT_WIDTH),
                   pl.BlockSpec((N_DN_HEADS, tm, 4), lambda t: (0, t, 0)),
                   pl.BlockSpec((N_DN_HEADS, tm // CHUNK, 4, CHUNK), lambda t: (0, t, 0, 0))),
        compiler_params=_params(("parallel",)),
        name="in_proj",
    )(xs, mod_i, norm_w, wqkv, wz, wft, wab, wabt, gpar, gpart, dft_cs)


def _conv_prep(src_ref, cw, n_rows, dst_ref, mode):
    n_chunks = n_rows // CHUNK
    halo = 8

    def body(c, carry):
        r0 = pl.multiple_of(c * CHUNK, CHUNK)
        main = src_ref[pl.ds(r0, CHUNK), :]
        prev = src_ref[pl.ds(pl.multiple_of(jnp.maximum(r0 - halo, 0), halo), halo), :]
        nxt = src_ref[pl.ds(pl.multiple_of(jnp.minimum(r0 + CHUNK, n_rows - halo), halo), halo), :]
        prev = jnp.where(c > 0, prev, 0.0)
        nxt = jnp.where(c < n_chunks - 1, nxt, 0.0)
        win = jnp.concatenate([prev, main, nxt], axis=0)
        n_win = CHUNK + 2 * halo
        acc = jnp.zeros((CHUNK, DN_HEAD_DIM), F32)
        for j in range(CONV_K):
            shift = (CONV_K // 2 - j) % n_win
            rolled = win if shift == 0 else pltpu.roll(win, shift, 0)
            acc = acc + rolled[halo:halo + CHUNK] * cw[j:j + 1, :]
        y = _silu(acc)
        if mode != "v":
            y = y * lax.rsqrt(jnp.sum(y * y, axis=-1, keepdims=True) + EPS)
        if mode == "q":
            y = y * (DN_HEAD_DIM ** -0.5)
        dst_ref[pl.ds(r0, CHUNK), :] = y
        return carry

    lax.fori_loop(0, n_chunks, body, 0)


def _tri_inverse(a, ii, jj):
    eye = (ii == jj).astype(F32)
    blk = lambda w: (ii // w) == (jj // w)
    ad = jnp.where(blk(8), a, 0.0)
    a2 = _dot(ad, ad)
    a4 = _dot(a2, a2)
    t = eye - ad
    t = t + _dot(t, a2)
    t = t + _dot(t, a4)
    w = 8
    while w < CHUNK:
        off = jnp.where(jnp.logical_and(blk(2 * w), jnp.logical_not(blk(w))), a, 0.0)
        t = t - _dot(_dot(t, off), t)
        w *= 2
    return t


def _chunk_step(c, backward, q_ref, k_ref, v_ref, gcol_ref, grow_ref, s_ref, o_ref):
    r0 = pl.multiple_of(c * CHUNK, CHUNK)
    q = q_ref[pl.ds(r0, CHUNK), :]
    k = k_ref[pl.ds(r0, CHUNK), :]
    v = v_ref[pl.ds(r0, CHUNK), :]
    col_g = 2 if backward else 0
    gates = gcol_ref[0, pl.ds(r0, CHUNK), :]
    gcol = gates[:, col_g:col_g + 1]
    bcol = gates[:, col_g + 1:col_g + 2]
    grow = grow_ref[0, c][col_g:col_g + 1, :]
    ii = lax.broadcasted_iota(jnp.int32, (CHUNK, CHUNK), 0)
    jj = lax.broadcasted_iota(jnp.int32, (CHUNK, CHUNK), 1)
    if backward:
        incl, strict = ii <= jj, ii < jj
        glast = gcol[0:1, :]
    else:
        incl, strict = ii >= jj, ii > jj
        glast = gcol[CHUNK - 1:CHUNK, :]
    decay = jnp.where(incl, jnp.exp(gcol - grow), 0.0)
    kb = k * bcol
    a_mat = jnp.where(strict, _dot_nt(kb, k) * decay, 0.0)
    t_inv = _tri_inverse(a_mat, ii, jj)
    eg = jnp.exp(gcol)
    u = _dot(t_inv, v * bcol)
    w = _dot(t_inv, kb * eg)
    qk = jnp.where(incl, _dot_nt(q, k) * decay, 0.0)
    q_dec = q * eg
    k_dec = k * jnp.exp(glast - gcol)
    g_end = jnp.exp(glast)
    s = s_ref[...]
    v_new = u - _dot(w, s)
    o_ref[pl.ds(r0, CHUNK), :] = _dot(q_dec, s) + _dot(qk, v_new)
    s_ref[...] = s * g_end + _dot(k_dec.T, v_new)


def _dn_kernel(seq, ctx_len,
               ql_ref, kl_ref, vl_ref, qc_ref, kc_ref, vc_ref, cwq_ref, cwk_ref, cwv_ref,
               zl_ref, zc_ref, gcl_ref, gcc_ref, grl_ref, grc_ref, nw_ref,
               ol_ref, oc_ref,
               qn, kn, vn, of, ob, sf, sb):
    nw = nw_ref[...]
    sf[...] = jnp.zeros_like(sf)
    sb[...] = jnp.zeros_like(sb)

    def segment(n_rows, q_ref, k_ref, v_ref, z_ref, gcol_ref, grow_ref, out_ref):
        _conv_prep(q_ref, cwq_ref[...], n_rows, qn, "q")
        _conv_prep(k_ref, cwk_ref[...], n_rows, kn, "k")
        _conv_prep(v_ref, cwv_ref[...], n_rows, vn, "v")
        n_chunks = n_rows // CHUNK

        def step(s, carry):
            _chunk_step(s, False, qn, kn, vn, gcol_ref, grow_ref, sf, of)
            _chunk_step(n_chunks - 1 - s, True, qn, kn, vn, gcol_ref, grow_ref, sb, ob)
            return carry

        lax.fori_loop(0, n_chunks, step, 0)

        blk = 256

        def fin(i, carry):
            r0 = pl.multiple_of(i * blk, blk)
            o = of[pl.ds(r0, blk), :] + ob[pl.ds(r0, blk), :]
            o = o * lax.rsqrt(jnp.mean(o * o, axis=-1, keepdims=True) + EPS) * nw
            out_ref[pl.ds(r0, blk), :] = (o * _silu(z_ref[pl.ds(r0, blk), :])).astype(out_ref.dtype)
            return carry

        lax.fori_loop(0, n_rows // blk, fin, 0)

    segment(ctx_len, qc_ref, kc_ref, vc_ref, zc_ref, gcc_ref, grc_ref, oc_ref)
    segment(seq, ql_ref, kl_ref, vl_ref, zl_ref, gcl_ref, grl_ref, ol_ref)


def _deltanet(qkv, z, gcol, grow, conv_w, dn_norm_w, n_b, seq, ctx_len):
    hd = DN_HEAD_DIM
    cb = n_b * seq // ctx_len
    lat = lambda off: pl.BlockSpec((seq, hd), lambda b, h: (b, h + off))
    ctx = lambda off: pl.BlockSpec((ctx_len, hd), lambda b, h: (cb + b, h + off))
    cws = lambda off: pl.BlockSpec((CONV_K, hd), lambda b, h: (0, h + off))
    kern = functools.partial(_dn_kernel, seq, ctx_len)
    return pl.pallas_call(
        kern,
        out_shape=(jax.ShapeDtypeStruct((n_b * seq, DN_WIDTH), BF16),
                   jax.ShapeDtypeStruct((n_b * ctx_len, DN_WIDTH), BF16)),
        grid=(n_b, N_DN_HEADS),
        in_specs=[lat(0), lat(N_DN_HEADS), lat(2 * N_DN_HEADS),
                  ctx(0), ctx(N_DN_HEADS), ctx(2 * N_DN_HEADS),
                  cws(0), cws(N_DN_HEADS), cws(2 * N_DN_HEADS),
                  lat(0), ctx(0),
                  pl.BlockSpec((1, seq, 4), lambda b, h: (h, b, 0)),
                  pl.BlockSpec((1, ctx_len, 4), lambda b, h: (h, cb + b, 0)),
                  pl.BlockSpec((1, seq // CHUNK, 4, CHUNK), lambda b, h: (h, b, 0, 0)),
                  pl.BlockSpec((1, ctx_len // CHUNK, 4, CHUNK), lambda b, h: (h, cb + b, 0, 0)),
                  pl.BlockSpec((1, hd), lambda b, h: (0, 0))],
        out_specs=(pl.BlockSpec((seq, hd), lambda b, h: (b, h)),
                   pl.BlockSpec((ctx_len, hd), lambda b, h: (b, h))),
        scratch_shapes=[pltpu.VMEM((seq, hd), F32) for _ in range(5)]
        + [pltpu.VMEM((hd, hd), F32) for _ in range(2)],
        compiler_params=_params(("parallel", "parallel")),
        name="deltanet",
    )(qkv, qkv, qkv, qkv, qkv, qkv, conv_w, conv_w, conv_w, z, z, gcol, gcol, grow, grow, dn_norm_w)


def _ft_kernel(scale, cl_ref, sl_ref, xc_ref, xs_ref, nw_ref, o_ref):
    y = (jnp.dot(cl_ref[...], xc_ref[...], preferred_element_type=F32)
         - jnp.dot(sl_ref[...], xs_ref[...], preferred_element_type=F32)) * scale
    y = y * lax.rsqrt(jnp.mean(y * y, axis=-1, keepdims=True) + EPS) * nw_ref[...]
    o_ref[...] = y.astype(o_ref.dtype)


def _fourier(xc, xs, cos_m, sin_m, ft_norm_w, n_b, length, row_block0):
    tm = min(512, length)
    n_m = length // tm
    scale = 1.0 / math.sqrt(length * FT_GROUP_DIM)
    return pl.pallas_call(
        functools.partial(_ft_kernel, scale),
        out_shape=jax.ShapeDtypeStruct((n_b * length, FT_WIDTH), BF16),
        grid=(n_b, n_m),
        in_specs=[pl.BlockSpec((tm, length), lambda b, m: (m, 0)),
                  pl.BlockSpec((tm, length), lambda b, m: (m, 0)),
                  pl.BlockSpec((length, FT_WIDTH), lambda b, m: (row_block0 + b, 0)),
                  pl.BlockSpec((length, FT_WIDTH), lambda b, m: (row_block0 + b, 0)),
                  pl.BlockSpec((1, FT_WIDTH), lambda b, m: (0, 0))],
        out_specs=pl.BlockSpec((tm, FT_WIDTH), lambda b, m: (b * n_m + m, 0)),
        compiler_params=_params(("parallel", "parallel")),
        name="fourier_seq",
    )(cos_m, sin_m, xc, xs, ft_norm_w)


def _dft_tables(length):
    m = jnp.arange(length, dtype=jnp.int32)
    ph = (m[:, None] * m[None, :]) % length
    ang = ph.astype(F32) * (2.0 * math.pi / length)
    return jnp.cos(ang).astype(BF16), jnp.sin(ang).astype(BF16)


def _out_proj_kernel(n_lat_tiles_per_b, n_b, x_ref, dn_ref, ft_ref, wdn_ref, wft_ref, mod_ref, nw_ref,
                     rwt_ref, x1_ref, h2_ref, lg_ref):
    t = pl.program_id(0)
    bi = jnp.minimum(t // n_lat_tiles_per_b, n_b)
    d = x_ref.shape[1]
    mix = (jnp.dot(dn_ref[...], wdn_ref[...], preferred_element_type=F32)
           + jnp.dot(ft_ref[...], wft_ref[...], preferred_element_type=F32))
    g1 = mod_ref[0, pl.ds(bi, 1), 2 * d:3 * d]
    sh2 = mod_ref[0, pl.ds(bi, 1), 3 * d:4 * d]
    sc2 = mod_ref[0, pl.ds(bi, 1), 4 * d:5 * d]
    x1 = x_ref[...] + g1 * mix
    x1_ref[...] = x1
    y = x1 * lax.rsqrt(jnp.mean(x1 * x1, axis=-1, keepdims=True) + EPS) * nw_ref[...]
    h2 = y * (1.0 + sc2) + sh2
    h2_ref[...] = h2.astype(BF16)
    lg_ref[...] = lax.dot_general(rwt_ref[...], h2, (((1,), (1,)), ((), ())),
                                  preferred_element_type=F32, precision=lax.Precision.HIGHEST)


def _out_proj(xs, dn, ft, wdn, wft, mod_i, layer, norm_w, router_wt, n_b, seq, t_rows):
    d = xs.shape[1]
    tm = 512
    kern = functools.partial(_out_proj_kernel, seq // tm, n_b)
    const = lambda *shape: pl.BlockSpec(shape, lambda t: tuple(0 for _ in shape))
    rows = lambda w: pl.BlockSpec((tm, w), lambda t: (t, 0))
    return pl.pallas_call(
        kern,
        out_shape=(jax.ShapeDtypeStruct((t_rows, d), F32),
                   jax.ShapeDtypeStruct((t_rows, d), BF16),
                   jax.ShapeDtypeStruct((N_EXPERTS, t_rows), F32)),
        grid=(t_rows // tm,),
        in_specs=[rows(d), rows(DN_WIDTH), rows(FT_WIDTH),
                  const(DN_WIDTH, d), const(FT_WIDTH, d),
                  pl.BlockSpec((1,) + mod_i.shape[1:], lambda t: (layer, 0, 0)),
                  const(1, d), const(N_EXPERTS, d)],
        out_specs=(rows(d), rows(d), pl.BlockSpec((N_EXPERTS, tm), lambda t: (0, t))),
        compiler_params=_params(("parallel",)),
        name="out_proj",
    )(xs, dn, ft, wdn, wft, mod_i, norm_w, router_wt)


def _route_kernel(tr, lg_ref, bias_ref, idx_ref, wt_ref, cnt_ref, upper, carry):
    step = pl.program_id(0)

    @pl.when(step == 0)
    def _():
        a = lax.broadcasted_iota(jnp.int32, (tr, tr), 0)
        b = lax.broadcasted_iota(jnp.int32, (tr, tr), 1)
        upper[...] = jnp.where(a < b, 1.0, 0.0).astype(BF16)
        carry[...] = jnp.zeros_like(carry)

    scores = _sigmoid(lg_ref[...])
    biased = scores + bias_ref[...]
    rows = [biased[r:r + 1, :] for r in range(N_EXPERTS)]
    srow = [scores[r:r + 1, :] for r in range(N_EXPERTS)]
    epg = EXPERTS_PER_GROUP

    def group_score(g):
        best = None
        for i in range(epg):
            for j in range(i + 1, epg):
                pair = rows[g * epg + i] + rows[g * epg + j]
                best = pair if best is None else jnp.maximum(best, pair)
        return best

    best_g = jnp.zeros((1, tr), jnp.int32)
    best_v = group_score(0)
    for g in range(1, N_EXPERT_GROUPS):
        gs = group_score(g)
        take = gs > best_v
        best_g = jnp.where(take, g, best_g)
        best_v = jnp.where(take, gs, best_v)

    def pick(table, r):
        out = table[r]
        for g in range(1, N_EXPERT_GROUPS):
            out = jnp.where(best_g == g, table[g * epg + r], out)
        return out

    in_b = [pick(rows, r) for r in range(epg)]
    in_s = [pick(srow, r) for r in range(epg)]
    l1 = jnp.zeros((1, tr), jnp.int32)
    m1 = in_b[0]
    for r in range(1, epg):
        take = in_b[r] > m1
        l1 = jnp.where(take, r, l1)
        m1 = jnp.where(take, in_b[r], m1)
    l2 = jnp.full((1, tr), -1, jnp.int32)
    m2 = jnp.full((1, tr), -jnp.inf, F32)
    for r in range(epg):
        take = jnp.logical_and(l1 != r, jnp.logical_or(l2 < 0, in_b[r] > m2))
        l2 = jnp.where(take, r, l2)
        m2 = jnp.where(take, in_b[r], m2)
    s1 = in_s[0]
    s2 = in_s[0]
    for r in range(1, epg):
        s1 = jnp.where(l1 == r, in_s[r], s1)
        s2 = jnp.where(l2 == r, in_s[r], s2)
    e1 = best_g * epg + l1
    e2 = best_g * epg + l2
    tot = s1 + s2
    wt_ref[0:1, :] = s1 / tot
    wt_ref[1:2, :] = s2 / tot

    eid = lax.broadcasted_iota(jnp.int32, (N_EXPERTS, tr), 0)
    is1 = eid == e1
    is2 = eid == e2
    memb = jnp.where(jnp.logical_or(is1, is2), 1.0, 0.0)
    prefix = jnp.dot(memb.astype(BF16), upper[...], preferred_element_type=F32) + carry[...]
    rank1 = jnp.sum(jnp.where(is1, prefix, 0.0), axis=0, keepdims=True)
    rank2 = jnp.sum(jnp.where(is2, prefix, 0.0), axis=0, keepdims=True)
    idx_ref[0:1, :] = e1
    idx_ref[1:2, :] = e2
    idx_ref[2:3, :] = rank1.astype(jnp.int32)
    idx_ref[3:4, :] = rank2.astype(jnp.int32)
    new_carry = carry[...] + jnp.sum(memb, axis=1, keepdims=True)
    carry[...] = new_carry
    cnt_ref[...] = new_carry.astype(jnp.int32)


def _route(logits_t, router_bias):
    n_e, t_rows = logits_t.shape
    tr = 512
    return pl.pallas_call(
        functools.partial(_route_kernel, tr),
        out_shape=(jax.ShapeDtypeStruct((4, t_rows), jnp.int32),
                   jax.ShapeDtypeStruct((2, t_rows), F32),
                   jax.ShapeDtypeStruct((n_e, 1), jnp.int32)),
        grid=(t_rows // tr,),
        in_specs=[pl.BlockSpec((n_e, tr), lambda t: (0, t)),
                  pl.BlockSpec((n_e, 1), lambda t: (0, 0))],
        out_specs=(pl.BlockSpec((4, tr), lambda t: (0, t)),
                   pl.BlockSpec((2, tr), lambda t: (0, t)),
                   pl.BlockSpec((n_e, 1), lambda t: (0, 0))),
        scratch_shapes=[pltpu.VMEM((tr, tr), BF16), pltpu.VMEM((n_e, 1), F32)],
        compiler_params=_params(("arbitrary",)),
        name="route",
    )(logits_t, router_bias)


def _ffn_kernel(be_ref, x_ref, wg_ref, wu_ref, wd_ref, o_ref):
    x = x_ref[...]
    gate = jnp.dot(x, wg_ref[0], preferred_element_type=F32)
    up = jnp.dot(x, wu_ref[0], preferred_element_type=F32)
    hid = (_silu(gate) * up).astype(BF16)
    o_ref[...] = jnp.dot(hid, wd_ref[0], preferred_element_type=F32)


def _expert_ffn(blk_expert, xsorted, w_gate, w_up, w_down):
    p_rows, d = xsorted.shape
    n_blk = p_rows // MOE_BLOCK
    de = w_gate.shape[-1]
    grid_spec = pltpu.PrefetchScalarGridSpec(
        num_scalar_prefetch=1,
        grid=(n_blk,),
        in_specs=[pl.BlockSpec((MOE_BLOCK, d), lambda i, be: (i, 0)),
                  pl.BlockSpec((1, d, de), lambda i, be: (be[i], 0, 0)),
                  pl.BlockSpec((1, d, de), lambda i, be: (be[i], 0, 0)),
                  pl.BlockSpec((1, de, d), lambda i, be: (be[i], 0, 0))],
        out_specs=pl.BlockSpec((MOE_BLOCK, d), lambda i, be: (i, 0)),
    )
    return pl.pallas_call(
        _ffn_kernel,
        out_shape=jax.ShapeDtypeStruct((p_rows, d), F32),
        grid_spec=grid_spec,
        compiler_params=_params(("arbitrary",)),
        name="expert_ffn",
    )(blk_expert, xsorted, w_gate, w_up, w_down)


def _final_norm_kernel(x_ref, w_ref, o_ref):
    x = x_ref[...]
    o_ref[...] = x * lax.rsqrt(jnp.mean(x * x, axis=-1, keepdims=True) + EPS) * w_ref[...]


def _final_norm(xs, w, n_rows):
    d = xs.shape[1]
    tm = 512
    return pl.pallas_call(
        _final_norm_kernel,
        out_shape=jax.ShapeDtypeStruct((n_rows, d), F32),
        grid=(n_rows // tm,),
        in_specs=[pl.BlockSpec((tm, d), lambda t: (t, 0)), pl.BlockSpec((1, d), lambda t: (0, 0))],
        out_specs=pl.BlockSpec((tm, d), lambda t: (t, 0)),
        compiler_params=_params(("parallel",)),
        name="final_norm",
    )(xs, w)


def _sincos_2d(length, dim):
    rows = length // GRID_W
    quarter = dim // 4
    omega = 1.0 / (POS_BASE ** (jnp.arange(quarter, dtype=F32) / quarter))
    ang_r = jnp.arange(rows, dtype=F32)[:, None] * omega
    ang_c = jnp.arange(GRID_W, dtype=F32)[:, None] * omega
    emb_r = jnp.concatenate([jnp.sin(ang_r), jnp.cos(ang_r)], axis=-1)
    emb_c = jnp.concatenate([jnp.sin(ang_c), jnp.cos(ang_c)], axis=-1)
    half = dim // 2
    emb = jnp.concatenate([jnp.broadcast_to(emb_r[:, None, :], (rows, GRID_W, half)),
                           jnp.broadcast_to(emb_c[None, :, :], (rows, GRID_W, half))], axis=-1)
    return emb.reshape(rows * GRID_W, dim)


def _moe(h2, idx, wts, counts, w_gate, w_up, w_down):
    t_rows, d = h2.shape
    a = 2 * t_rows
    counts = counts[:, 0]
    padded = (counts + MOE_BLOCK - 1) // MOE_BLOCK * MOE_BLOCK
    pad_end = jnp.cumsum(padded)
    pad_start = pad_end - padded
    n_blk = (a + N_EXPERTS * (MOE_BLOCK - 1) + MOE_BLOCK - 1) // MOE_BLOCK
    blk_expert = jnp.minimum(
        jnp.searchsorted(pad_end, jnp.arange(n_blk, dtype=jnp.int32) * MOE_BLOCK, side='right'),
        N_EXPERTS - 1).astype(jnp.int32)
    dest = pad_start[idx[0:2]] + idx[2:4]
    tok = jnp.arange(t_rows, dtype=jnp.int32)
    slot_tok = jnp.zeros((n_blk * MOE_BLOCK,), jnp.int32).at[dest.reshape(-1)].set(
        jnp.concatenate([tok, tok]))
    xsorted = h2[slot_tok]
    ys = _expert_ffn(blk_expert, xsorted, w_gate, w_up, w_down)
    return ys[dest[0]] * wts[0][:, None] + ys[dest[1]] * wts[1][:, None]


def kernel(x, c, ctx, c_ctx, ada_w, ada_b, norm_mix_w, norm_ffn_w, w_in, conv_w, a_log, dt_bias,
           dn_norm_w, ft_norm_w, w_out, router_w, router_bias, w_gate, w_up, w_down, final_norm_w):
    n_b, seq, d = x.shape
    ctx_len = ctx.shape[1]
    depth = ada_w.shape[0]
    n_lat = n_b * seq
    n_tok = n_lat + n_b * ctx_len
    ft_off = 4 * DN_WIDTH + N_GATE_COLS

    xs = jnp.concatenate([(x + _sincos_2d(seq, d)[None]).reshape(n_lat, d),
                          ctx.reshape(n_b * ctx_len, d)], axis=0)

    mod_rows = -(-(n_b + 1) // 8) * 8
    cvec = jnp.zeros((mod_rows, d), F32).at[:n_b].set(c).at[n_b].set(c_ctx)
    mod = _ada(cvec, ada_w, ada_b)

    cos_l, sin_l = _dft_tables(seq)
    cos_c, sin_c = _dft_tables(ctx_len)
    cc, sc = _dft_tables(FT_GROUP_DIM)
    dft_cs = jnp.concatenate([cc, sc], axis=1)
    router_wt = router_w.T
    rbias = router_bias.reshape(N_EXPERTS, 1)
    wg_b, wu_b, wd_b = w_gate.astype(BF16), w_up.astype(BF16), w_down.astype(BF16)

    col = jnp.arange(N_GATE_COLS)
    col_head, col_dir, col_ab = col // 4, (col // 2) % 2, col % 2
    gate_src = col_dir * (2 * N_DN_HEADS) + col_ab * N_DN_HEADS + col_head
    is_a = (col_ab == 0).astype(F32)
    is_bwd = (col_dir == 1).astype(F32)

    for i in range(depth):
        last = i == depth - 1
        w = w_in[i]
        wab = w[:, 4 * DN_WIDTH:ft_off][:, gate_src].astype(BF16)
        neg_a = -jnp.exp(a_log[i])[col_dir, col_head] * is_a
        dtb = dt_bias[i][col_dir, col_head] * is_a
        gpar = jnp.stack([neg_a, dtb, is_a, is_bwd])
        gpar_pad = jnp.pad(gpar, ((0, 0), (0, GATE_PAD - N_GATE_COLS)))
        wab_pad = jnp.pad(wab, ((0, 0), (0, GATE_PAD - N_GATE_COLS)))
        qkv, z, xc, xsn, gcol, grow = _in_proj(
            xs, mod, i, norm_mix_w[i].reshape(1, d),
            w[:, :3 * DN_WIDTH].astype(BF16), w[:, 3 * DN_WIDTH:4 * DN_WIDTH].astype(BF16),
            w[:, ft_off:].astype(BF16), wab_pad, wab.T, gpar_pad, gpar.T, dft_cs, n_b, seq)
        dn_l, dn_c = _deltanet(qkv, z, gcol, grow, conv_w[i], dn_norm_w[i].reshape(1, DN_HEAD_DIM),
                               n_b, seq, ctx_len)
        fnw = ft_norm_w[i].reshape(1, FT_WIDTH)
        ft_l = _fourier(xc, xsn, cos_l, sin_l, fnw, n_b, seq, 0)
        if last:
            dn, ft, rows = dn_l, ft_l, n_lat
        else:
            ft_c = _fourier(xc, xsn, cos_c, sin_c, fnw, n_b, ctx_len, n_lat // ctx_len)
            dn = jnp.concatenate([dn_l, dn_c], axis=0)
            ft = jnp.concatenate([ft_l, ft_c], axis=0)
            rows = n_tok
        wo = w_out[i].astype(BF16)
        x1, h2, logits_t = _out_proj(xs, dn, ft, wo[:DN_WIDTH], wo[DN_WIDTH:], mod, i,
                                     norm_ffn_w[i].reshape(1, d), router_wt, n_b, seq, rows)
        idx, wts, counts = _route(logits_t, rbias)
        y = _moe(h2, idx, wts, counts, wg_b[i], wu_b[i], wd_b[i])
        g2_l = jnp.repeat(mod[i, :n_b, 5 * d:], seq, axis=0)
        if last:
            xs = x1 + g2_l * y
        else:
            g2 = jnp.concatenate([g2_l, jnp.broadcast_to(mod[i, n_b, 5 * d:], (n_b * ctx_len, d))], axis=0)
            xs = x1 + g2 * y
    return _final_norm(xs, final_norm_w.reshape(1, d), n_lat).reshape(n_b, seq, d)
```

```python
import functools
import math

import jax
import jax.numpy as jnp
from jax import lax
from jax.experimental import pallas as pl
from jax.experimental.pallas import tpu as pltpu

F32 = jnp.float32
BF16 = jnp.bfloat16

GRID_W = 64
N_DN_HEADS = 4
DN_HEAD_DIM = 128
DN_WIDTH = N_DN_HEADS * DN_HEAD_DIM
N_FT_GROUPS = 4
FT_GROUP_DIM = 128
FT_WIDTH = N_FT_GROUPS * FT_GROUP_DIM
CONV_K = 5
CHUNK = 64
N_EXPERTS = 16
N_EXPERT_GROUPS = 4
EXPERTS_PER_GROUP = N_EXPERTS // N_EXPERT_GROUPS
D_EXPERT = 512
MOE_BLOCK = 256
POS_BASE = 10000.0
EPS = 1e-6
N_GATE_COLS = 4 * N_DN_HEADS
GATE_PAD = 128
CONV_BLOCK = 256
DN_UNROLL = 8

VMEM_LIMIT = 56 * 1024 * 1024


def _params(sem, vmem=VMEM_LIMIT):
    return pltpu.CompilerParams(dimension_semantics=sem, vmem_limit_bytes=vmem)


def _dot(a, b):
    return jnp.dot(a.astype(BF16), b.astype(BF16), preferred_element_type=F32)


def _dot_nt(a, b):
    return lax.dot_general(a.astype(BF16), b.astype(BF16), (((1,), (1,)), ((), ())),
                           preferred_element_type=F32)


def _sigmoid(x):
    return 1.0 / (1.0 + jnp.exp(-x))


def _silu(x):
    return x * _sigmoid(x)


def _softplus(x):
    return jnp.maximum(x, 0.0) + jnp.log(1.0 + jnp.exp(-jnp.abs(x)))


def _ada_kernel(c_ref, w_ref, b_ref, o_ref):
    act = _silu(c_ref[...])
    o_ref[0] = jnp.dot(act, w_ref[0], preferred_element_type=F32,
                       precision=lax.Precision.HIGHEST) + b_ref[0]


def _ada(cvec, ada_w, ada_b):
    depth, d, n = ada_w.shape
    rows = cvec.shape[0]
    tn = 1024
    return pl.pallas_call(
        _ada_kernel,
        out_shape=jax.ShapeDtypeStruct((depth, rows, n), F32),
        grid=(depth, n // tn),
        in_specs=[pl.BlockSpec((rows, d), lambda i, j: (0, 0)),
                  pl.BlockSpec((1, d, tn), lambda i, j: (i, 0, j)),
                  pl.BlockSpec((1, 1, tn), lambda i, j: (i, 0, j))],
        out_specs=pl.BlockSpec((1, rows, tn), lambda i, j: (i, 0, j)),
        compiler_params=_params(("parallel", "parallel")),
        name="ada_mod",
    )(cvec, ada_w, ada_b.reshape(depth, 1, n))


def _seg_scan(x, pos, axis, reverse):
    n = x.shape[axis]
    s = 1
    while s < CHUNK:
        if reverse:
            shifted = pltpu.roll(x, n - s, axis)
            x = x + jnp.where(pos < CHUNK - s, shifted, 0.0)
        else:
            shifted = pltpu.roll(x, s, axis)
            x = x + jnp.where(pos >= s, shifted, 0.0)
        s *= 2
    return x


def _in_proj_kernel(tm, n_lat_tiles_per_b, n_b, x_ref, mod_ref, nw_ref, wqkv_ref, wz_ref, wft_ref,
                    wab_ref, wabt_ref, gpar_ref, gpart_ref, dft_ref,
                    qkv_ref, z_ref, xc_ref, xs_ref, gcol_ref, grow_ref):
    t = pl.program_id(0)
    bi = jnp.minimum(t // n_lat_tiles_per_b, n_b)
    d = x_ref.shape[1]
    x = x_ref[...]
    y = x * lax.rsqrt(jnp.mean(x * x, axis=-1, keepdims=True) + EPS) * nw_ref[...]
    shift = mod_ref[0, pl.ds(bi, 1), 0:d]
    scale = mod_ref[0, pl.ds(bi, 1), d:2 * d]
    h = (y * (1.0 + scale) + shift).astype(BF16)

    for j in range(3):
        cs = slice(j * DN_WIDTH, (j + 1) * DN_WIDTH)
        qkv_ref[:, cs] = jnp.dot(h, wqkv_ref[:, cs], preferred_element_type=F32)
    z_ref[...] = jnp.dot(h, wz_ref[...], preferred_element_type=F32)

    ft = jnp.dot(h, wft_ref[...], preferred_element_type=F32).astype(BF16)
    for g in range(N_FT_GROUPS):
        cs = slice(g * FT_GROUP_DIM, (g + 1) * FT_GROUP_DIM)
        cssn = jnp.dot(ft[:, cs], dft_ref[...], preferred_element_type=F32)
        xc_ref[:, cs] = cssn[:, :FT_GROUP_DIM].astype(BF16)
        xs_ref[:, cs] = cssn[:, FT_GROUP_DIM:].astype(BF16)

    ab = jnp.dot(h, wab_ref[...], preferred_element_type=F32)
    abt = lax.dot_general(wabt_ref[...], h, (((1,), (1,)), ((), ())), preferred_element_type=F32)

    def gates(v, par, axis):
        neg_a, dtb, is_a, is_bwd = par
        g = neg_a * _softplus(v + dtb)
        pos = lax.broadcasted_iota(jnp.int32, v.shape, axis) % CHUNK
        fwd = _seg_scan(g, pos, axis, reverse=False)
        bwd = _seg_scan(g, pos, axis, reverse=True)
        cum = jnp.where(is_bwd > 0.5, bwd, fwd)
        return jnp.where(is_a > 0.5, cum, _sigmoid(v))

    gp = gpar_ref[...]
    gc = gates(ab, (gp[0:1], gp[1:2], gp[2:3], gp[3:4]), 0)
    gpt = gpart_ref[...]
    gt = gates(abt, (gpt[:, 0:1], gpt[:, 1:2], gpt[:, 2:3], gpt[:, 3:4]), 1)
    for hh in range(N_DN_HEADS):
        gcol_ref[hh] = gc[:, 4 * hh:4 * hh + 4]
        for j in range(tm // CHUNK):
            cs = slice(j * CHUNK, (j + 1) * CHUNK)
            grow_ref[hh, j] = jnp.concatenate([gt[4 * hh:4 * hh + 1, cs], gt[4 * hh + 2:4 * hh + 3, cs]], axis=1)


def _in_proj(xs, mod_i, layer, norm_w, wqkv, wz, wft, wab, wabt, gpar, gpart, dft_cs, n_b, seq):
    t_rows, d = xs.shape
    tm = 512
    n_tiles = t_rows // tm
    kern = functools.partial(_in_proj_kernel, tm, seq // tm, n_b)
    const = lambda *shape: pl.BlockSpec(shape, lambda t: tuple(0 for _ in shape))
    rows = lambda w: pl.BlockSpec((tm, w), lambda t: (t, 0))
    return pl.pallas_call(
        kern,
        out_shape=(jax.ShapeDtypeStruct((t_rows, 3 * DN_WIDTH), F32),
                   jax.ShapeDtypeStruct((t_rows, DN_WIDTH), F32),
                   jax.ShapeDtypeStruct((t_rows, FT_WIDTH), BF16),
                   jax.ShapeDtypeStruct((t_rows, FT_WIDTH), BF16),
                   jax.ShapeDtypeStruct((N_DN_HEADS, t_rows, 4), F32),
                   jax.ShapeDtypeStruct((N_DN_HEADS, t_rows // CHUNK, 1, 2 * CHUNK), F32)),
        grid=(n_tiles,),
        in_specs=[rows(d),
                  pl.BlockSpec((1,) + mod_i.shape[1:], lambda t: (layer, 0, 0)),
                  const(1, d),
                  const(d, 3 * DN_WIDTH), const(d, DN_WIDTH), const(d, FT_WIDTH),
                  const(d, GATE_PAD), const(N_GATE_COLS, d),
                  const(4, GATE_PAD), const(N_GATE_COLS, 4),
                  const(FT_GROUP_DIM, 2 * FT_GROUP_DIM)],
        out_specs=(rows(3 * DN_WIDTH), rows(DN_WIDTH), rows(FT_WIDTH), rows(FT_WIDTH),
                   pl.BlockSpec((N_DN_HEADS, tm, 4), lambda t: (0, t, 0)),
                   pl.BlockSpec((N_DN_HEADS, tm // CHUNK, 1, 2 * CHUNK), lambda t: (0, t, 0, 0))),
        compiler_params=_params(("parallel",)),
        name="in_proj",
    )(xs, mod_i, norm_w, wqkv, wz, wft, wab, wabt, gpar, gpart, dft_cs)


def _conv_prep(src_ref, cw, n_rows, dst_ref, mode):
    blk = CONV_BLOCK
    n_blk = n_rows // blk
    halo = 8
    n_win = blk + 2 * halo

    def body(c, carry):
        r0 = pl.multiple_of(c * blk, blk)
        main = src_ref[pl.ds(r0, blk), :]
        prev = src_ref[pl.ds(pl.multiple_of(jnp.maximum(r0 - halo, 0), halo), halo), :]
        nxt = src_ref[pl.ds(pl.multiple_of(jnp.minimum(r0 + blk, n_rows - halo), halo), halo), :]
        prev = jnp.where(c > 0, prev, 0.0)
        nxt = jnp.where(c < n_blk - 1, nxt, 0.0)
        win = jnp.concatenate([prev, main, nxt], axis=0)
        acc = jnp.zeros((blk, DN_HEAD_DIM), F32)
        for j in range(CONV_K):
            shift = (CONV_K // 2 - j) % n_win
            rolled = win if shift == 0 else pltpu.roll(win, shift, 0)
            acc = acc + rolled[halo:halo + blk] * cw[j:j + 1, :]
        y = _silu(acc)
        if mode != "v":
            y = y * lax.rsqrt(jnp.sum(y * y, axis=-1, keepdims=True) + EPS)
        if mode == "q":
            y = y * (DN_HEAD_DIM ** -0.5)
        dst_ref[pl.ds(r0, blk), :] = y
        return carry

    lax.fori_loop(0, n_blk, body, 0)


def _blockdiag(x, isb):
    return jnp.concatenate([jnp.where(isb, 0.0, x), jnp.where(isb, x, 0.0)], axis=0).astype(BF16)


def _blockdiag_wide(x):
    w = x.shape[1] // 2
    zero = jnp.zeros((x.shape[0], w), BF16)
    xb = x.astype(BF16)
    return jnp.concatenate([jnp.concatenate([xb[:, :w], zero], axis=1),
                            jnp.concatenate([zero, xb[:, w:]], axis=1)], axis=0)


def _tri_inverse_dual(a_list, eye, xor, isb):
    mm = lambda x, y: jnp.dot(x.astype(BF16), _blockdiag(y, isb), preferred_element_type=F32)
    ad = [jnp.where((xor >> 3) == 0, a, 0.0) for a in a_list]
    a2 = [mm(x, x) for x in ad]
    a4 = [mm(x, x) for x in a2]
    t = [eye - x for x in ad]
    t = [x + mm(x, y) for x, y in zip(t, a2)]
    t = [x + mm(x, y) for x, y in zip(t, a4)]
    for s in (3, 4, 5):
        off = [jnp.where((xor >> s) == 1, a, 0.0) for a in a_list]
        to = [mm(x, y) for x, y in zip(t, off)]
        t = [x - mm(y, x) for x, y in zip(t, to)]
    return t


def _chunks_local(loaded):
    hd = DN_HEAD_DIM
    n = len(loaded)
    q = [x[0] for x in loaded]
    k = [x[1] for x in loaded]
    v = [x[2] for x in loaded]
    gf = [x[3][:, 0:1] for x in loaded]
    bf_ = [x[3][:, 1:2] for x in loaded]
    gb = [x[3][:, 2:3] for x in loaded]
    bb = [x[3][:, 3:4] for x in loaded]
    grow = [x[4] for x in loaded]
    row = lax.broadcasted_iota(jnp.int32, (CHUNK, 2 * CHUNK), 0)
    lane = lax.broadcasted_iota(jnp.int32, (CHUNK, 2 * CHUNK), 1)
    jl = lane & (CHUNK - 1)
    isb = lane >= CHUNK
    delta = jnp.where(isb, jl - row, row - jl)
    xor = row ^ jl
    eye = jnp.where(delta == 0, 1.0, 0.0)
    nt = (((1,), (1,)), ((), ()))
    k2 = [jnp.concatenate([x, x], axis=0).astype(BF16) for x in k]
    kkd = [lax.dot_general(k[i].astype(BF16), k2[i], nt, preferred_element_type=F32) for i in range(n)]
    qkd = [lax.dot_general(q[i].astype(BF16), k2[i], nt, preferred_element_type=F32) for i in range(n)]
    dec = [jnp.where(delta >= 0, jnp.exp(jnp.where(isb, gb[i], gf[i]) - grow[i]), 0.0) for i in range(n)]
    a_mat = [jnp.where(delta > 0, kkd[i] * jnp.where(isb, bb[i], bf_[i]) * dec[i], 0.0) for i in range(n)]
    t_inv = _tri_inverse_dual(a_mat, eye, xor, isb)
    egf = [jnp.exp(x) for x in gf]
    egb = [jnp.exp(x) for x in gb]
    rhs = [jnp.concatenate([v[i] * bf_[i], k[i] * (bf_[i] * egf[i]), v[i] * bb[i], k[i] * (bb[i] * egb[i])],
                           axis=1) for i in range(n)]
    sol = [jnp.dot(t_inv[i].astype(BF16), _blockdiag_wide(rhs[i]), preferred_element_type=F32)
           for i in range(n)]
    bd_sol = [_blockdiag_wide(x) for x in sol]
    r1 = [jnp.dot((qkd[i] * dec[i]).astype(BF16), bd_sol[i], preferred_element_type=F32)
          for i in range(n)]
    glf = [x[CHUNK - 1:CHUNK, :] for x in gf]
    glb = [x[0:1, :] for x in gb]
    kdec = [jnp.concatenate([k[i] * jnp.exp(glf[i] - gf[i]), k[i] * jnp.exp(glb[i] - gb[i])], axis=0)
            for i in range(n)]
    r2 = [jnp.dot(kdec[i].T.astype(BF16), bd_sol[i], preferred_element_type=F32)
          for i in range(n)]
    out = []
    for i in range(n):
        o_loc = r1[i][:, 0:hd] + r1[i][:, 2 * hd:3 * hd]
        qt = jnp.concatenate([q[i] * egf[i] - r1[i][:, hd:2 * hd], q[i] * egb[i] - r1[i][:, 3 * hd:]],
                             axis=1).astype(BF16)
        nn = jnp.concatenate([r2[i][:, 0:hd], r2[i][:, 2 * hd:3 * hd]], axis=1)
        kw = jnp.concatenate([r2[i][:, hd:2 * hd], r2[i][:, 3 * hd:]], axis=1).astype(BF16)
        ge = jnp.concatenate([jnp.broadcast_to(jnp.exp(glf[i]), (1, hd)),
                              jnp.broadcast_to(jnp.exp(glb[i]), (1, hd))], axis=0)
        out.append((o_loc, qt, nn, kw, ge))
    return out


def _state_step(c, d, oacc, qt_ref, kw_ref, nn_ref, ge_ref, s_ref):
    hd = DN_HEAD_DIM
    r0 = pl.multiple_of(c * CHUNK, CHUNK)
    cs = slice(d * hd, (d + 1) * hd)
    s = s_ref[d]
    lhs = jnp.concatenate([qt_ref[c, :, cs], kw_ref[c, :, cs]], axis=0)
    r = jnp.dot(lhs, s.astype(BF16), preferred_element_type=F32)
    oacc[pl.ds(r0, CHUNK), :] += r[:CHUNK]
    s_ref[d] = s * ge_ref[c, d:d + 1, :] + nn_ref[c, :, cs] - r[CHUNK:]


def _dn_kernel(seq, ctx_len,
               ql_ref, kl_ref, vl_ref, qc_ref, kc_ref, vc_ref, cwq_ref, cwk_ref, cwv_ref,
               zl_ref, zc_ref, gcl_ref, gcc_ref, grl_ref, grc_ref, nw_ref,
               ol_ref, oc_ref,
               qn, kn, vn, oacc, qt_s, kw_s, nn_s, ge_s, s_s):
    nw = nw_ref[...]
    s_s[...] = jnp.zeros_like(s_s)

    def segment(n_rows, q_ref, k_ref, v_ref, z_ref, gcol_ref, grow_ref, out_ref):
        _conv_prep(q_ref, cwq_ref[...], n_rows, qn, "q")
        _conv_prep(k_ref, cwk_ref[...], n_rows, kn, "k")
        _conv_prep(v_ref, cwv_ref[...], n_rows, vn, "v")
        n_chunks = n_rows // CHUNK

        unroll = math.gcd(DN_UNROLL, n_chunks)

        def local(i, carry):
            chunks = [i * unroll + j for j in range(unroll)]
            rows = [pl.ds(pl.multiple_of(c * CHUNK, CHUNK), CHUNK) for c in chunks]
            loaded = [(qn[r, :], kn[r, :], vn[r, :], gcol_ref[0, r, :], grow_ref[0, c])
                      for c, r in zip(chunks, rows)]
            results = _chunks_local(loaded)
            for c, r, (o_loc, qt, nn, kw, ge) in zip(chunks, rows, results):
                oacc[r, :] = o_loc
                qt_s[c] = qt
                nn_s[c] = nn
                kw_s[c] = kw
                ge_s[c] = ge
            return carry

        lax.fori_loop(0, n_chunks // unroll, local, 0)

        def step(s, carry):
            _state_step(s, 0, oacc, qt_s, kw_s, nn_s, ge_s, s_s)
            _state_step(n_chunks - 1 - s, 1, oacc, qt_s, kw_s, nn_s, ge_s, s_s)
            return carry

        lax.fori_loop(0, n_chunks, step, 0)

        blk = 256

        def fin(i, carry):
            r0 = pl.multiple_of(i * blk, blk)
            o = oacc[pl.ds(r0, blk), :]
            o = o * lax.rsqrt(jnp.mean(o * o, axis=-1, keepdims=True) + EPS) * nw
            out_ref[pl.ds(r0, blk), :] = (o * _silu(z_ref[pl.ds(r0, blk), :])).astype(out_ref.dtype)
            return carry

        lax.fori_loop(0, n_rows // blk, fin, 0)

    segment(ctx_len, qc_ref, kc_ref, vc_ref, zc_ref, gcc_ref, grc_ref, oc_ref)
    segment(seq, ql_ref, kl_ref, vl_ref, zl_ref, gcl_ref, grl_ref, ol_ref)


def _deltanet(qkv, z, gcol, grow, conv_w, dn_norm_w, n_b, seq, ctx_len):
    hd = DN_HEAD_DIM
    nc = seq // CHUNK
    cb = n_b * seq // ctx_len
    lat = lambda off: pl.BlockSpec((seq, hd), lambda b, h: (b, h + off))
    ctx = lambda off: pl.BlockSpec((ctx_len, hd), lambda b, h: (cb + b, h + off))
    cws = lambda off: pl.BlockSpec((CONV_K, hd), lambda b, h: (0, h + off))
    kern = functools.partial(_dn_kernel, seq, ctx_len)
    return pl.pallas_call(
        kern,
        out_shape=(jax.ShapeDtypeStruct((n_b * seq, DN_WIDTH), BF16),
                   jax.ShapeDtypeStruct((n_b * ctx_len, DN_WIDTH), BF16)),
        grid=(n_b, N_DN_HEADS),
        in_specs=[lat(0), lat(N_DN_HEADS), lat(2 * N_DN_HEADS),
                  ctx(0), ctx(N_DN_HEADS), ctx(2 * N_DN_HEADS),
                  cws(0), cws(N_DN_HEADS), cws(2 * N_DN_HEADS),
                  lat(0), ctx(0),
                  pl.BlockSpec((1, seq, 4), lambda b, h: (h, b, 0)),
                  pl.BlockSpec((1, ctx_len, 4), lambda b, h: (h, cb + b, 0)),
                  pl.BlockSpec((1, nc, 1, 2 * CHUNK), lambda b, h: (h, b, 0, 0)),
                  pl.BlockSpec((1, ctx_len // CHUNK, 1, 2 * CHUNK), lambda b, h: (h, cb + b, 0, 0)),
                  pl.BlockSpec((1, hd), lambda b, h: (0, 0))],
        out_specs=(pl.BlockSpec((seq, hd), lambda b, h: (b, h)),
                   pl.BlockSpec((ctx_len, hd), lambda b, h: (b, h))),
        scratch_shapes=[pltpu.VMEM((seq, hd), F32) for _ in range(4)]
        + [pltpu.VMEM((nc, CHUNK, 2 * hd), BF16), pltpu.VMEM((nc, hd, 2 * hd), BF16),
           pltpu.VMEM((nc, hd, 2 * hd), F32), pltpu.VMEM((nc, 2, hd), F32),
           pltpu.VMEM((2, hd, hd), F32)],
        compiler_params=_params(("parallel", "parallel")),
        name="deltanet",
    )(qkv, qkv, qkv, qkv, qkv, qkv, conv_w, conv_w, conv_w, z, z, gcol, gcol, grow, grow, dn_norm_w)


def _ft_kernel(scale, cl_ref, sl_ref, xc_ref, xs_ref, nw_ref, o_ref):
    y = (jnp.dot(cl_ref[...], xc_ref[...], preferred_element_type=F32)
         - jnp.dot(sl_ref[...], xs_ref[...], preferred_element_type=F32)) * scale
    y = y * lax.rsqrt(jnp.mean(y * y, axis=-1, keepdims=True) + EPS) * nw_ref[...]
    o_ref[...] = y.astype(o_ref.dtype)


def _fourier(xc, xs, cos_m, sin_m, ft_norm_w, n_b, length, row_block0):
    tm = min(512, length)
    n_m = length // tm
    scale = 1.0 / math.sqrt(length * FT_GROUP_DIM)
    return pl.pallas_call(
        functools.partial(_ft_kernel, scale),
        out_shape=jax.ShapeDtypeStruct((n_b * length, FT_WIDTH), BF16),
        grid=(n_b, n_m),
        in_specs=[pl.BlockSpec((tm, length), lambda b, m: (m, 0)),
                  pl.BlockSpec((tm, length), lambda b, m: (m, 0)),
                  pl.BlockSpec((length, FT_WIDTH), lambda b, m: (row_block0 + b, 0)),
                  pl.BlockSpec((length, FT_WIDTH), lambda b, m: (row_block0 + b, 0)),
                  pl.BlockSpec((1, FT_WIDTH), lambda b, m: (0, 0))],
        out_specs=pl.BlockSpec((tm, FT_WIDTH), lambda b, m: (b * n_m + m, 0)),
        compiler_params=_params(("parallel", "parallel")),
        name="fourier_seq",
    )(cos_m, sin_m, xc, xs, ft_norm_w)


def _dft_tables(length):
    m = jnp.arange(length, dtype=jnp.int32)
    ph = (m[:, None] * m[None, :]) % length
    ang = ph.astype(F32) * (2.0 * math.pi / length)
    return jnp.cos(ang).astype(BF16), jnp.sin(ang).astype(BF16)


def _out_proj_kernel(n_lat_tiles_per_b, n_b, x_ref, dn_ref, ft_ref, wdn_ref, wft_ref, mod_ref, nw_ref,
                     rwt_ref, x1_ref, h2_ref, lg_ref):
    t = pl.program_id(0)
    bi = jnp.minimum(t // n_lat_tiles_per_b, n_b)
    d = x_ref.shape[1]
    mix = (jnp.dot(dn_ref[...], wdn_ref[...], preferred_element_type=F32)
           + jnp.dot(ft_ref[...], wft_ref[...], preferred_element_type=F32))
    g1 = mod_ref[0, pl.ds(bi, 1), 2 * d:3 * d]
    sh2 = mod_ref[0, pl.ds(bi, 1), 3 * d:4 * d]
    sc2 = mod_ref[0, pl.ds(bi, 1), 4 * d:5 * d]
    x1 = x_ref[...] + g1 * mix
    x1_ref[...] = x1
    y = x1 * lax.rsqrt(jnp.mean(x1 * x1, axis=-1, keepdims=True) + EPS) * nw_ref[...]
    h2 = y * (1.0 + sc2) + sh2
    h2_ref[...] = h2.astype(BF16)
    lg_ref[...] = lax.dot_general(rwt_ref[...], h2, (((1,), (1,)), ((), ())),
                                  preferred_element_type=F32, precision=lax.Precision.HIGHEST)


def _out_proj(xs, dn, ft, wdn, wft, mod_i, layer, norm_w, router_wt, n_b, seq, t_rows):
    d = xs.shape[1]
    tm = 512
    kern = functools.partial(_out_proj_kernel, seq // tm, n_b)
    const = lambda *shape: pl.BlockSpec(shape, lambda t: tuple(0 for _ in shape))
    rows = lambda w: pl.BlockSpec((tm, w), lambda t: (t, 0))
    return pl.pallas_call(
        kern,
        out_shape=(jax.ShapeDtypeStruct((t_rows, d), F32),
                   jax.ShapeDtypeStruct((t_rows, d), BF16),
                   jax.ShapeDtypeStruct((N_EXPERTS, t_rows), F32)),
        grid=(t_rows // tm,),
        in_specs=[rows(d), rows(DN_WIDTH), rows(FT_WIDTH),
                  const(DN_WIDTH, d), const(FT_WIDTH, d),
                  pl.BlockSpec((1,) + mod_i.shape[1:], lambda t: (layer, 0, 0)),
                  const(1, d), const(N_EXPERTS, d)],
        out_specs=(rows(d), rows(d), pl.BlockSpec((N_EXPERTS, tm), lambda t: (0, t))),
        compiler_params=_params(("parallel",)),
        name="out_proj",
    )(xs, dn, ft, wdn, wft, mod_i, norm_w, router_wt)


def _route_kernel(tr, lg_ref, bias_ref, idx_ref, wt_ref, cnt_ref, upper, carry):
    step = pl.program_id(0)

    @pl.when(step == 0)
    def _():
        a = lax.broadcasted_iota(jnp.int32, (tr, tr), 0)
        b = lax.broadcasted_iota(jnp.int32, (tr, tr), 1)
        upper[...] = jnp.where(a < b, 1.0, 0.0).astype(BF16)
        carry[...] = jnp.zeros_like(carry)

    scores = _sigmoid(lg_ref[...])
    biased = scores + bias_ref[...]
    rows = [biased[r:r + 1, :] for r in range(N_EXPERTS)]
    srow = [scores[r:r + 1, :] for r in range(N_EXPERTS)]
    epg = EXPERTS_PER_GROUP

    def group_score(g):
        best = None
        for i in range(epg):
            for j in range(i + 1, epg):
                pair = rows[g * epg + i] + rows[g * epg + j]
                best = pair if best is None else jnp.maximum(best, pair)
        return best

    best_g = jnp.zeros((1, tr), jnp.int32)
    best_v = group_score(0)
    for g in range(1, N_EXPERT_GROUPS):
        gs = group_score(g)
        take = gs > best_v
        best_g = jnp.where(take, g, best_g)
        best_v = jnp.where(take, gs, best_v)

    def pick(table, r):
        out = table[r]
        for g in range(1, N_EXPERT_GROUPS):
            out = jnp.where(best_g == g, table[g * epg + r], out)
        return out

    in_b = [pick(rows, r) for r in range(epg)]
    in_s = [pick(srow, r) for r in range(epg)]
    l1 = jnp.zeros((1, tr), jnp.int32)
    m1 = in_b[0]
    for r in range(1, epg):
        take = in_b[r] > m1
        l1 = jnp.where(take, r, l1)
        m1 = jnp.where(take, in_b[r], m1)
    l2 = jnp.full((1, tr), -1, jnp.int32)
    m2 = jnp.full((1, tr), -jnp.inf, F32)
    for r in range(epg):
        take = jnp.logical_and(l1 != r, jnp.logical_or(l2 < 0, in_b[r] > m2))
        l2 = jnp.where(take, r, l2)
        m2 = jnp.where(take, in_b[r], m2)
    s1 = in_s[0]
    s2 = in_s[0]
    for r in range(1, epg):
        s1 = jnp.where(l1 == r, in_s[r], s1)
        s2 = jnp.where(l2 == r, in_s[r], s2)
    e1 = best_g * epg + l1
    e2 = best_g * epg + l2
    tot = s1 + s2
    wt_ref[0:1, :] = s1 / tot
    wt_ref[1:2, :] = s2 / tot

    eid = lax.broadcasted_iota(jnp.int32, (N_EXPERTS, tr), 0)
    is1 = eid == e1
    is2 = eid == e2
    memb = jnp.where(jnp.logical_or(is1, is2), 1.0, 0.0)
    prefix = jnp.dot(memb.astype(BF16), upper[...], preferred_element_type=F32) + carry[...]
    rank1 = jnp.sum(jnp.where(is1, prefix, 0.0), axis=0, keepdims=True)
    rank2 = jnp.sum(jnp.where(is2, prefix, 0.0), axis=0, keepdims=True)
    idx_ref[0:1, :] = e1
    idx_ref[1:2, :] = e2
    idx_ref[2:3, :] = rank1.astype(jnp.int32)
    idx_ref[3:4, :] = rank2.astype(jnp.int32)
    new_carry = carry[...] + jnp.sum(memb, axis=1, keepdims=True)
    carry[...] = new_carry
    cnt_ref[...] = new_carry.astype(jnp.int32)


def _route(logits_t, router_bias):
    n_e, t_rows = logits_t.shape
    tr = 512
    return pl.pallas_call(
        functools.partial(_route_kernel, tr),
        out_shape=(jax.ShapeDtypeStruct((4, t_rows), jnp.int32),
                   jax.ShapeDtypeStruct((2, t_rows), F32),
                   jax.ShapeDtypeStruct((n_e, 1), jnp.int32)),
        grid=(t_rows // tr,),
        in_specs=[pl.BlockSpec((n_e, tr), lambda t: (0, t)),
                  pl.BlockSpec((n_e, 1), lambda t: (0, 0))],
        out_specs=(pl.BlockSpec((4, tr), lambda t: (0, t)),
                   pl.BlockSpec((2, tr), lambda t: (0, t)),
                   pl.BlockSpec((n_e, 1), lambda t: (0, 0))),
        scratch_shapes=[pltpu.VMEM((tr, tr), BF16), pltpu.VMEM((n_e, 1), F32)],
        compiler_params=_params(("arbitrary",)),
        name="route",
    )(logits_t, router_bias)


def _ffn_kernel(be_ref, x_ref, wg_ref, wu_ref, wd_ref, o_ref):
    x = x_ref[...]
    gate = jnp.dot(x, wg_ref[0], preferred_element_type=F32)
    up = jnp.dot(x, wu_ref[0], preferred_element_type=F32)
    hid = (_silu(gate) * up).astype(BF16)
    o_ref[...] = jnp.dot(hid, wd_ref[0], preferred_element_type=F32)


def _expert_ffn(blk_expert, xsorted, w_gate, w_up, w_down):
    p_rows, d = xsorted.shape
    n_blk = p_rows // MOE_BLOCK
    de = w_gate.shape[-1]
    grid_spec = pltpu.PrefetchScalarGridSpec(
        num_scalar_prefetch=1,
        grid=(n_blk,),
        in_specs=[pl.BlockSpec((MOE_BLOCK, d), lambda i, be: (i, 0)),
                  pl.BlockSpec((1, d, de), lambda i, be: (be[i], 0, 0)),
                  pl.BlockSpec((1, d, de), lambda i, be: (be[i], 0, 0)),
                  pl.BlockSpec((1, de, d), lambda i, be: (be[i], 0, 0))],
        out_specs=pl.BlockSpec((MOE_BLOCK, d), lambda i, be: (i, 0)),
    )
    return pl.pallas_call(
        _ffn_kernel,
        out_shape=jax.ShapeDtypeStruct((p_rows, d), F32),
        grid_spec=grid_spec,
        compiler_params=_params(("arbitrary",)),
        name="expert_ffn",
    )(blk_expert, xsorted, w_gate, w_up, w_down)


def _final_norm_kernel(x_ref, w_ref, o_ref):
    x = x_ref[...]
    o_ref[...] = x * lax.rsqrt(jnp.mean(x * x, axis=-1, keepdims=True) + EPS) * w_ref[...]


def _final_norm(xs, w, n_rows):
    d = xs.shape[1]
    tm = 512
    return pl.pallas_call(
        _final_norm_kernel,
        out_shape=jax.ShapeDtypeStruct((n_rows, d), F32),
        grid=(n_rows // tm,),
        in_specs=[pl.BlockSpec((tm, d), lambda t: (t, 0)), pl.BlockSpec((1, d), lambda t: (0, 0))],
        out_specs=pl.BlockSpec((tm, d), lambda t: (t, 0)),
        compiler_params=_params(("parallel",)),
        name="final_norm",
    )(xs, w)


def _sincos_2d(length, dim):
    rows = length // GRID_W
    quarter = dim // 4
    omega = 1.0 / (POS_BASE ** (jnp.arange(quarter, dtype=F32) / quarter))
    ang_r = jnp.arange(rows, dtype=F32)[:, None] * omega
    ang_c = jnp.arange(GRID_W, dtype=F32)[:, None] * omega
    emb_r = jnp.concatenate([jnp.sin(ang_r), jnp.cos(ang_r)], axis=-1)
    emb_c = jnp.concatenate([jnp.sin(ang_c), jnp.cos(ang_c)], axis=-1)
    half = dim // 2
    emb = jnp.concatenate([jnp.broadcast_to(emb_r[:, None, :], (rows, GRID_W, half)),
                           jnp.broadcast_to(emb_c[None, :, :], (rows, GRID_W, half))], axis=-1)
    return emb.reshape(rows * GRID_W, dim)


def _moe(h2, idx, wts, counts, w_gate, w_up, w_down):
    t_rows, d = h2.shape
    a = 2 * t_rows
    counts = counts[:, 0]
    padded = (counts + MOE_BLOCK - 1) // MOE_BLOCK * MOE_BLOCK
    pad_end = jnp.cumsum(padded)
    pad_start = pad_end - padded
    n_blk = (a + N_EXPERTS * (MOE_BLOCK - 1) + MOE_BLOCK - 1) // MOE_BLOCK
    blk_expert = jnp.minimum(
        jnp.searchsorted(pad_end, jnp.arange(n_blk, dtype=jnp.int32) * MOE_BLOCK, side='right'),
        N_EXPERTS - 1).astype(jnp.int32)
    dest = pad_start[idx[0:2]] + idx[2:4]
    tok = jnp.arange(t_rows, dtype=jnp.int32)
    slot_tok = jnp.zeros((n_blk * MOE_BLOCK,), jnp.int32).at[dest.reshape(-1)].set(
        jnp.concatenate([tok, tok]))
    xsorted = h2[slot_tok]
    ys = _expert_ffn(blk_expert, xsorted, w_gate, w_up, w_down)
    return ys[dest[0]] * wts[0][:, None] + ys[dest[1]] * wts[1][:, None]


def kernel(x, c, ctx, c_ctx, ada_w, ada_b, norm_mix_w, norm_ffn_w, w_in, conv_w, a_log, dt_bias,
           dn_norm_w, ft_norm_w, w_out, router_w, router_bias, w_gate, w_up, w_down, final_norm_w):
    n_b, seq, d = x.shape
    ctx_len = ctx.shape[1]
    depth = ada_w.shape[0]
    n_lat = n_b * seq
    n_tok = n_lat + n_b * ctx_len
    ft_off = 4 * DN_WIDTH + N_GATE_COLS

    xs = jnp.concatenate([(x + _sincos_2d(seq, d)[None]).reshape(n_lat, d),
                          ctx.reshape(n_b * ctx_len, d)], axis=0)

    mod_rows = -(-(n_b + 1) // 8) * 8
    cvec = jnp.zeros((mod_rows, d), F32).at[:n_b].set(c).at[n_b].set(c_ctx)
    mod = _ada(cvec, ada_w, ada_b)

    cos_l, sin_l = _dft_tables(seq)
    cos_c, sin_c = _dft_tables(ctx_len)
    cc, sc = _dft_tables(FT_GROUP_DIM)
    dft_cs = jnp.concatenate([cc, sc], axis=1)
    router_wt = router_w.T
    rbias = router_bias.reshape(N_EXPERTS, 1)
    wg_b, wu_b, wd_b = w_gate.astype(BF16), w_up.astype(BF16), w_down.astype(BF16)

    col = jnp.arange(N_GATE_COLS)
    col_head, col_dir, col_ab = col // 4, (col // 2) % 2, col % 2
    gate_src = col_dir * (2 * N_DN_HEADS) + col_ab * N_DN_HEADS + col_head
    is_a = (col_ab == 0).astype(F32)
    is_bwd = (col_dir == 1).astype(F32)

    for i in range(depth):
        last = i == depth - 1
        w = w_in[i]
        wab = w[:, 4 * DN_WIDTH:ft_off][:, gate_src].astype(BF16)
        neg_a = -jnp.exp(a_log[i])[col_dir, col_head] * is_a
        dtb = dt_bias[i][col_dir, col_head] * is_a
        gpar = jnp.stack([neg_a, dtb, is_a, is_bwd])
        gpar_pad = jnp.pad(gpar, ((0, 0), (0, GATE_PAD - N_GATE_COLS)))
        wab_pad = jnp.pad(wab, ((0, 0), (0, GATE_PAD - N_GATE_COLS)))
        qkv, z, xc, xsn, gcol, grow = _in_proj(
            xs, mod, i, norm_mix_w[i].reshape(1, d),
            w[:, :3 * DN_WIDTH].astype(BF16), w[:, 3 * DN_WIDTH:4 * DN_WIDTH].astype(BF16),
            w[:, ft_off:].astype(BF16), wab_pad, wab.T, gpar_pad, gpar.T, dft_cs, n_b, seq)
        dn_l, dn_c = _deltanet(qkv, z, gcol, grow, conv_w[i], dn_norm_w[i].reshape(1, DN_HEAD_DIM),
                               n_b, seq, ctx_len)
        fnw = ft_norm_w[i].reshape(1, FT_WIDTH)
        ft_l = _fourier(xc, xsn, cos_l, sin_l, fnw, n_b, seq, 0)
        if last:
            dn, ft, rows = dn_l, ft_l, n_lat
        else:
            ft_c = _fourier(xc, xsn, cos_c, sin_c, fnw, n_b, ctx_len, n_lat // ctx_len)
            dn = jnp.concatenate([dn_l, dn_c], axis=0)
            ft = jnp.concatenate([ft_l, ft_c], axis=0)
            rows = n_tok
        wo = w_out[i].astype(BF16)
        x1, h2, logits_t = _out_proj(xs, dn, ft, wo[:DN_WIDTH], wo[DN_WIDTH:], mod, i,
                                     norm_ffn_w[i].reshape(1, d), router_wt, n_b, seq, rows)
        idx, wts, counts = _route(logits_t, rbias)
        y = _moe(h2, idx, wts, counts, wg_b[i], wu_b[i], wd_b[i])
        g2_l = jnp.repeat(mod[i, :n_b, 5 * d:], seq, axis=0)
        if last:
            xs = x1 + g2_l * y
        else:
            g2 = jnp.concatenate([g2_l, jnp.broadcast_to(mod[i, n_b, 5 * d:], (n_b * ctx_len, d))], axis=0)
            xs = x1 + g2 * y
    return _final_norm(xs, final_norm_w.reshape(1, d), n_lat).reshape(n_b, seq, d)
```

```python
import functools
import math

import jax
import jax.numpy as jnp
from jax import lax
from jax.experimental import pallas as pl
from jax.experimental.pallas import tpu as pltpu

F32 = jnp.float32
BF16 = jnp.bfloat16

GRID_W = 64
N_DN_HEADS = 4
DN_HEAD_DIM = 128
DN_WIDTH = N_DN_HEADS * DN_HEAD_DIM
N_FT_GROUPS = 4
FT_GROUP_DIM = 128
FT_WIDTH = N_FT_GROUPS * FT_GROUP_DIM
CONV_K = 5
CHUNK = 64
N_EXPERTS = 16
N_EXPERT_GROUPS = 4
EXPERTS_PER_GROUP = N_EXPERTS // N_EXPERT_GROUPS
D_EXPERT = 512
MOE_BLOCK = 256
POS_BASE = 10000.0
EPS = 1e-6
N_GATE_COLS = 4 * N_DN_HEADS
GATE_PAD = 128
CONV_BLOCK = 256
DN_UNROLL = 8

VMEM_LIMIT = 56 * 1024 * 1024


def _params(sem, vmem=VMEM_LIMIT):
    return pltpu.CompilerParams(dimension_semantics=sem, vmem_limit_bytes=vmem)


def _dot(a, b):
    return jnp.dot(a.astype(BF16), b.astype(BF16), preferred_element_type=F32)


def _dot_nt(a, b):
    return lax.dot_general(a.astype(BF16), b.astype(BF16), (((1,), (1,)), ((), ())),
                           preferred_element_type=F32)


def _sigmoid(x):
    return 1.0 / (1.0 + jnp.exp(-x))


def _silu(x):
    return x * _sigmoid(x)


def _softplus(x):
    return jnp.maximum(x, 0.0) + jnp.log(1.0 + jnp.exp(-jnp.abs(x)))


def _ada_kernel(c_ref, w_ref, b_ref, o_ref):
    act = _silu(c_ref[...])
    o_ref[0] = jnp.dot(act, w_ref[0], preferred_element_type=F32,
                       precision=lax.Precision.HIGHEST) + b_ref[0]


def _ada(cvec, ada_w, ada_b):
    depth, d, n = ada_w.shape
    rows = cvec.shape[0]
    tn = 1024
    return pl.pallas_call(
        _ada_kernel,
        out_shape=jax.ShapeDtypeStruct((depth, rows, n), F32),
        grid=(depth, n // tn),
        in_specs=[pl.BlockSpec((rows, d), lambda i, j: (0, 0)),
                  pl.BlockSpec((1, d, tn), lambda i, j: (i, 0, j)),
                  pl.BlockSpec((1, 1, tn), lambda i, j: (i, 0, j))],
        out_specs=pl.BlockSpec((1, rows, tn), lambda i, j: (i, 0, j)),
        compiler_params=_params(("parallel", "parallel")),
        name="ada_mod",
    )(cvec, ada_w, ada_b.reshape(depth, 1, n))


def _seg_scan(x, pos, axis, reverse):
    n = x.shape[axis]
    s = 1
    while s < CHUNK:
        if reverse:
            shifted = pltpu.roll(x, n - s, axis)
            x = x + jnp.where(pos < CHUNK - s, shifted, 0.0)
        else:
            shifted = pltpu.roll(x, s, axis)
            x = x + jnp.where(pos >= s, shifted, 0.0)
        s *= 2
    return x


def _in_proj_kernel(tm, n_lat_tiles_per_b, n_b, x_ref, mod_ref, nw_ref, wqkv_ref, wz_ref, wft_ref,
                    wab_ref, wabt_ref, gpar_ref, gpart_ref, dft_ref,
                    qkv_ref, z_ref, xc_ref, xs_ref, gcol_ref, grow_ref):
    t = pl.program_id(0)
    bi = jnp.minimum(t // n_lat_tiles_per_b, n_b)
    d = x_ref.shape[1]
    x = x_ref[...]
    y = x * lax.rsqrt(jnp.mean(x * x, axis=-1, keepdims=True) + EPS) * nw_ref[...]
    shift = mod_ref[0, pl.ds(bi, 1), 0:d]
    scale = mod_ref[0, pl.ds(bi, 1), d:2 * d]
    h = (y * (1.0 + scale) + shift).astype(BF16)

    for j in range(3):
        cs = slice(j * DN_WIDTH, (j + 1) * DN_WIDTH)
        qkv_ref[:, cs] = jnp.dot(h, wqkv_ref[:, cs], preferred_element_type=F32)
    z_ref[...] = jnp.dot(h, wz_ref[...], preferred_element_type=F32)

    ft = jnp.dot(h, wft_ref[...], preferred_element_type=F32).astype(BF16)
    for g in range(N_FT_GROUPS):
        cs = slice(g * FT_GROUP_DIM, (g + 1) * FT_GROUP_DIM)
        cssn = jnp.dot(ft[:, cs], dft_ref[...], preferred_element_type=F32)
        xc_ref[:, cs] = cssn[:, :FT_GROUP_DIM].astype(BF16)
        xs_ref[:, cs] = cssn[:, FT_GROUP_DIM:].astype(BF16)

    ab = jnp.dot(h, wab_ref[...], preferred_element_type=F32)
    abt = lax.dot_general(wabt_ref[...], h, (((1,), (1,)), ((), ())), preferred_element_type=F32)

    def gates(v, par, axis):
        neg_a, dtb, is_a, is_bwd = par
        g = neg_a * _softplus(v + dtb)
        pos = lax.broadcasted_iota(jnp.int32, v.shape, axis) % CHUNK
        fwd = _seg_scan(g, pos, axis, reverse=False)
        bwd = _seg_scan(g, pos, axis, reverse=True)
        cum = jnp.where(is_bwd > 0.5, bwd, fwd)
        return jnp.where(is_a > 0.5, cum, _sigmoid(v))

    gp = gpar_ref[...]
    gc = gates(ab, (gp[0:1], gp[1:2], gp[2:3], gp[3:4]), 0)
    gpt = gpart_ref[...]
    gt = gates(abt, (gpt[:, 0:1], gpt[:, 1:2], gpt[:, 2:3], gpt[:, 3:4]), 1)
    for hh in range(N_DN_HEADS):
        gcol_ref[hh] = gc[:, 4 * hh:4 * hh + 4]
        for j in range(tm // CHUNK):
            cs = slice(j * CHUNK, (j + 1) * CHUNK)
            grow_ref[hh, j] = jnp.concatenate([gt[4 * hh:4 * hh + 1, cs], gt[4 * hh + 2:4 * hh + 3, cs]], axis=1)


def _in_proj(xs, mod_i, layer, norm_w, wqkv, wz, wft, wab, wabt, gpar, gpart, dft_cs, n_b, seq):
    t_rows, d = xs.shape
    tm = 512
    n_tiles = t_rows // tm
    kern = functools.partial(_in_proj_kernel, tm, seq // tm, n_b)
    const = lambda *shape: pl.BlockSpec(shape, lambda t: tuple(0 for _ in shape))
    rows = lambda w: pl.BlockSpec((tm, w), lambda t: (t, 0))
    return pl.pallas_call(
        kern,
        out_shape=(jax.ShapeDtypeStruct((t_rows, 3 * DN_WIDTH), F32),
                   jax.ShapeDtypeStruct((t_rows, DN_WIDTH), F32),
                   jax.ShapeDtypeStruct((t_rows, FT_WIDTH), BF16),
                   jax.ShapeDtypeStruct((t_rows, FT_WIDTH), BF16),
                   jax.ShapeDtypeStruct((N_DN_HEADS, t_rows, 4), F32),
                   jax.ShapeDtypeStruct((N_DN_HEADS, t_rows // CHUNK, 1, 2 * CHUNK), F32)),
        grid=(n_tiles,),
        in_specs=[rows(d),
                  pl.BlockSpec((1,) + mod_i.shape[1:], lambda t: (layer, 0, 0)),
                  const(1, d),
                  const(d, 3 * DN_WIDTH), const(d, DN_WIDTH), const(d, FT_WIDTH),
                  const(d, GATE_PAD), const(N_GATE_COLS, d),
                  const(4, GATE_PAD), const(N_GATE_COLS, 4),
                  const(FT_GROUP_DIM, 2 * FT_GROUP_DIM)],
        out_specs=(rows(3 * DN_WIDTH), rows(DN_WIDTH), rows(FT_WIDTH), rows(FT_WIDTH),
                   pl.BlockSpec((N_DN_HEADS, tm, 4), lambda t: (0, t, 0)),
                   pl.BlockSpec((N_DN_HEADS, tm // CHUNK, 1, 2 * CHUNK), lambda t: (0, t, 0, 0))),
        compiler_params=_params(("parallel",)),
        name="in_proj",
    )(xs, mod_i, norm_w, wqkv, wz, wft, wab, wabt, gpar, gpart, dft_cs)


def _conv_prep(src_ref, cw, n_rows, dst_ref, mode):
    blk = CONV_BLOCK
    n_blk = n_rows // blk
    halo = 8
    n_win = blk + 2 * halo

    def body(c, carry):
        r0 = pl.multiple_of(c * blk, blk)
        main = src_ref[pl.ds(r0, blk), :]
        prev = src_ref[pl.ds(pl.multiple_of(jnp.maximum(r0 - halo, 0), halo), halo), :]
        nxt = src_ref[pl.ds(pl.multiple_of(jnp.minimum(r0 + blk, n_rows - halo), halo), halo), :]
        prev = jnp.where(c > 0, prev, 0.0)
        nxt = jnp.where(c < n_blk - 1, nxt, 0.0)
        win = jnp.concatenate([prev, main, nxt], axis=0)
        acc = jnp.zeros((blk, DN_HEAD_DIM), F32)
        for j in range(CONV_K):
            shift = (CONV_K // 2 - j) % n_win
            rolled = win if shift == 0 else pltpu.roll(win, shift, 0)
            acc = acc + rolled[halo:halo + blk] * cw[j:j + 1, :]
        y = _silu(acc)
        if mode != "v":
            y = y * lax.rsqrt(jnp.sum(y * y, axis=-1, keepdims=True) + EPS)
        if mode == "q":
            y = y * (DN_HEAD_DIM ** -0.5)
        dst_ref[pl.ds(r0, blk), :] = y
        return carry

    lax.fori_loop(0, n_blk, body, 0)


def _blockdiag(x, isb):
    return jnp.concatenate([jnp.where(isb, 0.0, x), jnp.where(isb, x, 0.0)], axis=0).astype(BF16)


def _blockdiag_wide(x):
    w = x.shape[1] // 2
    zero = jnp.zeros((x.shape[0], w), BF16)
    xb = x.astype(BF16)
    return jnp.concatenate([jnp.concatenate([xb[:, :w], zero], axis=1),
                            jnp.concatenate([zero, xb[:, w:]], axis=1)], axis=0)


def _tri_inverse_dual(a_list, eye, xor, isb):
    mm = lambda x, y: jnp.dot(x.astype(BF16), _blockdiag(y, isb), preferred_element_type=F32)
    ad = [jnp.where((xor >> 3) == 0, a, 0.0) for a in a_list]
    a2 = [mm(x, x) for x in ad]
    a4 = [mm(x, x) for x in a2]
    t = [eye - x for x in ad]
    t = [x + mm(x, y) for x, y in zip(t, a2)]
    t = [x + mm(x, y) for x, y in zip(t, a4)]
    for s in (3, 4, 5):
        off = [jnp.where((xor >> s) == 1, a, 0.0) for a in a_list]
        to = [mm(x, y) for x, y in zip(t, off)]
        t = [x - mm(y, x) for x, y in zip(t, to)]
    return t


def _chunks_local(loaded):
    hd = DN_HEAD_DIM
    n = len(loaded)
    q = [x[0] for x in loaded]
    k = [x[1] for x in loaded]
    v = [x[2] for x in loaded]
    gf = [x[3][:, 0:1] for x in loaded]
    bf_ = [x[3][:, 1:2] for x in loaded]
    gb = [x[3][:, 2:3] for x in loaded]
    bb = [x[3][:, 3:4] for x in loaded]
    grow = [x[4] for x in loaded]
    row = lax.broadcasted_iota(jnp.int32, (CHUNK, 2 * CHUNK), 0)
    lane = lax.broadcasted_iota(jnp.int32, (CHUNK, 2 * CHUNK), 1)
    jl = lane & (CHUNK - 1)
    isb = lane >= CHUNK
    delta = jnp.where(isb, jl - row, row - jl)
    xor = row ^ jl
    eye = jnp.where(delta == 0, 1.0, 0.0)
    nt = (((1,), (1,)), ((), ()))
    k2 = [jnp.concatenate([x, x], axis=0).astype(BF16) for x in k]
    kkd = [lax.dot_general(k[i].astype(BF16), k2[i], nt, preferred_element_type=F32) for i in range(n)]
    qkd = [lax.dot_general(q[i].astype(BF16), k2[i], nt, preferred_element_type=F32) for i in range(n)]
    dec = [jnp.where(delta >= 0, jnp.exp(jnp.where(isb, gb[i], gf[i]) - grow[i]), 0.0) for i in range(n)]
    a_mat = [jnp.where(delta > 0, kkd[i] * jnp.where(isb, bb[i], bf_[i]) * dec[i], 0.0) for i in range(n)]
    t_inv = _tri_inverse_dual(a_mat, eye, xor, isb)
    egf = [jnp.exp(x) for x in gf]
    egb = [jnp.exp(x) for x in gb]
    rhs = [jnp.concatenate([v[i] * bf_[i], k[i] * (bf_[i] * egf[i]), v[i] * bb[i], k[i] * (bb[i] * egb[i])],
                           axis=1) for i in range(n)]
    sol = [jnp.dot(t_inv[i].astype(BF16), _blockdiag_wide(rhs[i]), preferred_element_type=F32)
           for i in range(n)]
    bd_sol = [_blockdiag_wide(x) for x in sol]
    r1 = [jnp.dot((qkd[i] * dec[i]).astype(BF16), bd_sol[i], preferred_element_type=F32)
          for i in range(n)]
    glf = [x[CHUNK - 1:CHUNK, :] for x in gf]
    glb = [x[0:1, :] for x in gb]
    kdec = [jnp.concatenate([k[i] * jnp.exp(glf[i] - gf[i]), k[i] * jnp.exp(glb[i] - gb[i])], axis=0)
            for i in range(n)]
    r2 = [jnp.dot(kdec[i].T.astype(BF16), bd_sol[i], preferred_element_type=F32)
          for i in range(n)]
    out = []
    for i in range(n):
        o_loc = r1[i][:, 0:hd] + r1[i][:, 2 * hd:3 * hd]
        qt = jnp.concatenate([q[i] * egf[i] - r1[i][:, hd:2 * hd], q[i] * egb[i] - r1[i][:, 3 * hd:]],
                             axis=1).astype(BF16)
        nn = jnp.concatenate([r2[i][:, 0:hd], r2[i][:, 2 * hd:3 * hd]], axis=1)
        kw = jnp.concatenate([r2[i][:, hd:2 * hd], r2[i][:, 3 * hd:]], axis=1).astype(BF16)
        ge = jnp.concatenate([jnp.broadcast_to(jnp.exp(glf[i]), (1, hd)),
                              jnp.broadcast_to(jnp.exp(glb[i]), (1, hd))], axis=0)
        out.append((o_loc, qt, nn, kw, ge))
    return out


def _state_step(c, d, oacc, qt_ref, kw_ref, nn_ref, ge_ref, s_ref):
    hd = DN_HEAD_DIM
    r0 = pl.multiple_of(c * CHUNK, CHUNK)
    cs = slice(d * hd, (d + 1) * hd)
    s = s_ref[d]
    lhs = jnp.concatenate([qt_ref[c, :, cs], kw_ref[c, :, cs]], axis=0)
    r = jnp.dot(lhs, s.astype(BF16), preferred_element_type=F32)
    oacc[pl.ds(r0, CHUNK), :] += r[:CHUNK]
    s_ref[d] = s * ge_ref[c, d:d + 1, :] + nn_ref[c, :, cs] - r[CHUNK:]


def _dn_kernel(seq, ctx_len,
               ql_ref, kl_ref, vl_ref, qc_ref, kc_ref, vc_ref, cwq_ref, cwk_ref, cwv_ref,
               zl_ref, zc_ref, gcl_ref, gcc_ref, grl_ref, grc_ref, nw_ref,
               ol_ref, oc_ref,
               qn, kn, vn, oacc, qt_s, kw_s, nn_s, ge_s, s_s):
    nw = nw_ref[...]
    s_s[...] = jnp.zeros_like(s_s)

    def segment(n_rows, q_ref, k_ref, v_ref, z_ref, gcol_ref, grow_ref, out_ref):
        _conv_prep(q_ref, cwq_ref[...], n_rows, qn, "q")
        _conv_prep(k_ref, cwk_ref[...], n_rows, kn, "k")
        _conv_prep(v_ref, cwv_ref[...], n_rows, vn, "v")
        n_chunks = n_rows // CHUNK

        unroll = math.gcd(DN_UNROLL, n_chunks)

        def local(i, carry):
            chunks = [i * unroll + j for j in range(unroll)]
            rows = [pl.ds(pl.multiple_of(c * CHUNK, CHUNK), CHUNK) for c in chunks]
            loaded = [(qn[r, :], kn[r, :], vn[r, :], gcol_ref[0, r, :], grow_ref[0, c])
                      for c, r in zip(chunks, rows)]
            results = _chunks_local(loaded)
            for c, r, (o_loc, qt, nn, kw, ge) in zip(chunks, rows, results):
                oacc[r, :] = o_loc
                qt_s[c] = qt
                nn_s[c] = nn
                kw_s[c] = kw
                ge_s[c] = ge
            return carry

        lax.fori_loop(0, n_chunks // unroll, local, 0)

        def step(s, carry):
            _state_step(s, 0, oacc, qt_s, kw_s, nn_s, ge_s, s_s)
            _state_step(n_chunks - 1 - s, 1, oacc, qt_s, kw_s, nn_s, ge_s, s_s)
            return carry

        lax.fori_loop(0, n_chunks, step, 0)

        blk = 256

        def fin(i, carry):
            r0 = pl.multiple_of(i * blk, blk)
            o = oacc[pl.ds(r0, blk), :]
            o = o * lax.rsqrt(jnp.mean(o * o, axis=-1, keepdims=True) + EPS) * nw
            out_ref[pl.ds(r0, blk), :] = (o * _silu(z_ref[pl.ds(r0, blk), :])).astype(out_ref.dtype)
            return carry

        lax.fori_loop(0, n_rows // blk, fin, 0)

    segment(ctx_len, qc_ref, kc_ref, vc_ref, zc_ref, gcc_ref, grc_ref, oc_ref)
    segment(seq, ql_ref, kl_ref, vl_ref, zl_ref, gcl_ref, grl_ref, ol_ref)


def _deltanet(qkv, z, gcol, grow, conv_w, dn_norm_w, n_b, seq, ctx_len):
    hd = DN_HEAD_DIM
    nc = seq // CHUNK
    cb = n_b * seq // ctx_len
    lat = lambda off: pl.BlockSpec((seq, hd), lambda b, h: (b, h + off))
    ctx = lambda off: pl.BlockSpec((ctx_len, hd), lambda b, h: (cb + b, h + off))
    cws = lambda off: pl.BlockSpec((CONV_K, hd), lambda b, h: (0, h + off))
    kern = functools.partial(_dn_kernel, seq, ctx_len)
    return pl.pallas_call(
        kern,
        out_shape=(jax.ShapeDtypeStruct((n_b * seq, DN_WIDTH), BF16),
                   jax.ShapeDtypeStruct((n_b * ctx_len, DN_WIDTH), BF16)),
        grid=(n_b, N_DN_HEADS),
        in_specs=[lat(0), lat(N_DN_HEADS), lat(2 * N_DN_HEADS),
                  ctx(0), ctx(N_DN_HEADS), ctx(2 * N_DN_HEADS),
                  cws(0), cws(N_DN_HEADS), cws(2 * N_DN_HEADS),
                  lat(0), ctx(0),
                  pl.BlockSpec((1, seq, 4), lambda b, h: (h, b, 0)),
                  pl.BlockSpec((1, ctx_len, 4), lambda b, h: (h, cb + b, 0)),
                  pl.BlockSpec((1, nc, 1, 2 * CHUNK), lambda b, h: (h, b, 0, 0)),
                  pl.BlockSpec((1, ctx_len // CHUNK, 1, 2 * CHUNK), lambda b, h: (h, cb + b, 0, 0)),
                  pl.BlockSpec((1, hd), lambda b, h: (0, 0))],
        out_specs=(pl.BlockSpec((seq, hd), lambda b, h: (b, h)),
                   pl.BlockSpec((ctx_len, hd), lambda b, h: (b, h))),
        scratch_shapes=[pltpu.VMEM((seq, hd), F32) for _ in range(4)]
        + [pltpu.VMEM((nc, CHUNK, 2 * hd), BF16), pltpu.VMEM((nc, hd, 2 * hd), BF16),
           pltpu.VMEM((nc, hd, 2 * hd), F32), pltpu.VMEM((nc, 2, hd), F32),
           pltpu.VMEM((2, hd, hd), F32)],
        compiler_params=_params(("parallel", "parallel")),
        name="deltanet",
    )(qkv, qkv, qkv, qkv, qkv, qkv, conv_w, conv_w, conv_w, z, z, gcol, gcol, grow, grow, dn_norm_w)


def _ft_kernel(scale, cl_ref, sl_ref, xc_ref, xs_ref, nw_ref, o_ref):
    y = (jnp.dot(cl_ref[...], xc_ref[...], preferred_element_type=F32)
         - jnp.dot(sl_ref[...], xs_ref[...], preferred_element_type=F32)) * scale
    y = y * lax.rsqrt(jnp.mean(y * y, axis=-1, keepdims=True) + EPS) * nw_ref[...]
    o_ref[...] = y.astype(o_ref.dtype)


def _fourier(xc, xs, cos_m, sin_m, ft_norm_w, n_b, length, row_block0):
    tm = min(512, length)
    n_m = length // tm
    scale = 1.0 / math.sqrt(length * FT_GROUP_DIM)
    return pl.pallas_call(
        functools.partial(_ft_kernel, scale),
        out_shape=jax.ShapeDtypeStruct((n_b * length, FT_WIDTH), BF16),
        grid=(n_b, n_m),
        in_specs=[pl.BlockSpec((tm, length), lambda b, m: (m, 0)),
                  pl.BlockSpec((tm, length), lambda b, m: (m, 0)),
                  pl.BlockSpec((length, FT_WIDTH), lambda b, m: (row_block0 + b, 0)),
                  pl.BlockSpec((length, FT_WIDTH), lambda b, m: (row_block0 + b, 0)),
                  pl.BlockSpec((1, FT_WIDTH), lambda b, m: (0, 0))],
        out_specs=pl.BlockSpec((tm, FT_WIDTH), lambda b, m: (b * n_m + m, 0)),
        compiler_params=_params(("parallel", "parallel")),
        name="fourier_seq",
    )(cos_m, sin_m, xc, xs, ft_norm_w)


def _dft_tables(length):
    m = jnp.arange(length, dtype=jnp.int32)
    ph = (m[:, None] * m[None, :]) % length
    ang = ph.astype(F32) * (2.0 * math.pi / length)
    return jnp.cos(ang).astype(BF16), jnp.sin(ang).astype(BF16)


def _out_proj_kernel(n_lat_tiles_per_b, n_b, x_ref, dn_ref, ft_ref, wdn_ref, wft_ref, mod_ref, nw_ref,
                     rwt_ref, x1_ref, h2_ref, lg_ref):
    t = pl.program_id(0)
    bi = jnp.minimum(t // n_lat_tiles_per_b, n_b)
    d = x_ref.shape[1]
    mix = (jnp.dot(dn_ref[...], wdn_ref[...], preferred_element_type=F32)
           + jnp.dot(ft_ref[...], wft_ref[...], preferred_element_type=F32))
    g1 = mod_ref[0, pl.ds(bi, 1), 2 * d:3 * d]
    sh2 = mod_ref[0, pl.ds(bi, 1), 3 * d:4 * d]
    sc2 = mod_ref[0, pl.ds(bi, 1), 4 * d:5 * d]
    x1 = x_ref[...] + g1 * mix
    x1_ref[...] = x1
    y = x1 * lax.rsqrt(jnp.mean(x1 * x1, axis=-1, keepdims=True) + EPS) * nw_ref[...]
    h2 = y * (1.0 + sc2) + sh2
    h2_ref[...] = h2
    lg_ref[...] = lax.dot_general(rwt_ref[...], h2, (((1,), (1,)), ((), ())),
                                  preferred_element_type=F32, precision=lax.Precision.HIGHEST)


def _out_proj(xs, dn, ft, wdn, wft, mod_i, layer, norm_w, router_wt, n_b, seq, t_rows):
    d = xs.shape[1]
    tm = 512
    kern = functools.partial(_out_proj_kernel, seq // tm, n_b)
    const = lambda *shape: pl.BlockSpec(shape, lambda t: tuple(0 for _ in shape))
    rows = lambda w: pl.BlockSpec((tm, w), lambda t: (t, 0))
    return pl.pallas_call(
        kern,
        out_shape=(jax.ShapeDtypeStruct((t_rows, d), F32),
                   jax.ShapeDtypeStruct((t_rows, d), F32),
                   jax.ShapeDtypeStruct((N_EXPERTS, t_rows), F32)),
        grid=(t_rows // tm,),
        in_specs=[rows(d), rows(DN_WIDTH), rows(FT_WIDTH),
                  const(DN_WIDTH, d), const(FT_WIDTH, d),
                  pl.BlockSpec((1,) + mod_i.shape[1:], lambda t: (layer, 0, 0)),
                  const(1, d), const(N_EXPERTS, d)],
        out_specs=(rows(d), rows(d), pl.BlockSpec((N_EXPERTS, tm), lambda t: (0, t))),
        compiler_params=_params(("parallel",)),
        name="out_proj",
    )(xs, dn, ft, wdn, wft, mod_i, norm_w, router_wt)


def _route_kernel(tr, lg_ref, bias_ref, idx_ref, wt_ref, cnt_ref, upper, carry):
    step = pl.program_id(0)

    @pl.when(step == 0)
    def _():
        a = lax.broadcasted_iota(jnp.int32, (tr, tr), 0)
        b = lax.broadcasted_iota(jnp.int32, (tr, tr), 1)
        upper[...] = jnp.where(a < b, 1.0, 0.0).astype(BF16)
        carry[...] = jnp.zeros_like(carry)

    scores = _sigmoid(lg_ref[...])
    biased = scores + bias_ref[...]
    rows = [biased[r:r + 1, :] for r in range(N_EXPERTS)]
    srow = [scores[r:r + 1, :] for r in range(N_EXPERTS)]
    epg = EXPERTS_PER_GROUP

    def group_score(g):
        best = None
        for i in range(epg):
            for j in range(i + 1, epg):
                pair = rows[g * epg + i] + rows[g * epg + j]
                best = pair if best is None else jnp.maximum(best, pair)
        return best

    best_g = jnp.zeros((1, tr), jnp.int32)
    best_v = group_score(0)
    for g in range(1, N_EXPERT_GROUPS):
        gs = group_score(g)
        take = gs > best_v
        best_g = jnp.where(take, g, best_g)
        best_v = jnp.where(take, gs, best_v)

    def pick(table, r):
        out = table[r]
        for g in range(1, N_EXPERT_GROUPS):
            out = jnp.where(best_g == g, table[g * epg + r], out)
        return out

    in_b = [pick(rows, r) for r in range(epg)]
    in_s = [pick(srow, r) for r in range(epg)]
    l1 = jnp.zeros((1, tr), jnp.int32)
    m1 = in_b[0]
    for r in range(1, epg):
        take = in_b[r] > m1
        l1 = jnp.where(take, r, l1)
        m1 = jnp.where(take, in_b[r], m1)
    l2 = jnp.full((1, tr), -1, jnp.int32)
    m2 = jnp.full((1, tr), -jnp.inf, F32)
    for r in range(epg):
        take = jnp.logical_and(l1 != r, jnp.logical_or(l2 < 0, in_b[r] > m2))
        l2 = jnp.where(take, r, l2)
        m2 = jnp.where(take, in_b[r], m2)
    s1 = in_s[0]
    s2 = in_s[0]
    for r in range(1, epg):
        s1 = jnp.where(l1 == r, in_s[r], s1)
        s2 = jnp.where(l2 == r, in_s[r], s2)
    e1 = best_g * epg + l1
    e2 = best_g * epg + l2
    tot = s1 + s2
    wt_ref[0:1, :] = s1 / tot
    wt_ref[1:2, :] = s2 / tot

    eid = lax.broadcasted_iota(jnp.int32, (N_EXPERTS, tr), 0)
    is1 = eid == e1
    is2 = eid == e2
    memb = jnp.where(jnp.logical_or(is1, is2), 1.0, 0.0)
    prefix = jnp.dot(memb.astype(BF16), upper[...], preferred_element_type=F32) + carry[...]
    rank1 = jnp.sum(jnp.where(is1, prefix, 0.0), axis=0, keepdims=True)
    rank2 = jnp.sum(jnp.where(is2, prefix, 0.0), axis=0, keepdims=True)
    idx_ref[0:1, :] = e1
    idx_ref[1:2, :] = e2
    idx_ref[2:3, :] = rank1.astype(jnp.int32)
    idx_ref[3:4, :] = rank2.astype(jnp.int32)
    new_carry = carry[...] + jnp.sum(memb, axis=1, keepdims=True)
    carry[...] = new_carry
    cnt_ref[...] = new_carry.astype(jnp.int32)


def _route(logits_t, router_bias):
    n_e, t_rows = logits_t.shape
    tr = 512
    return pl.pallas_call(
        functools.partial(_route_kernel, tr),
        out_shape=(jax.ShapeDtypeStruct((4, t_rows), jnp.int32),
                   jax.ShapeDtypeStruct((2, t_rows), F32),
                   jax.ShapeDtypeStruct((n_e, 1), jnp.int32)),
        grid=(t_rows // tr,),
        in_specs=[pl.BlockSpec((n_e, tr), lambda t: (0, t)),
                  pl.BlockSpec((n_e, 1), lambda t: (0, 0))],
        out_specs=(pl.BlockSpec((4, tr), lambda t: (0, t)),
                   pl.BlockSpec((2, tr), lambda t: (0, t)),
                   pl.BlockSpec((n_e, 1), lambda t: (0, 0))),
        scratch_shapes=[pltpu.VMEM((tr, tr), BF16), pltpu.VMEM((n_e, 1), F32)],
        compiler_params=_params(("arbitrary",)),
        name="route",
    )(logits_t, router_bias)


def _row_copy(src, src_row, dst, dst_row, sem):
    return pltpu.make_async_copy(src.at[pl.ds(src_row, 1), :], dst.at[pl.ds(dst_row, 1), :], sem)


def _dispatch_kernel(tm, t_rows, dest_ref, lo_ref, hi_ref, h_ref, xs_hbm, sem, pad_sem):
    i = pl.program_id(0)
    base = i * tm

    @pl.when(i == 0)
    def _():
        def per_expert(e, carry):
            lo, hi = lo_ref[e], hi_ref[e]

            def start(s, c):
                _row_copy(h_ref, 0, xs_hbm, s, pad_sem).start()
                return c

            def wait(s, c):
                _row_copy(h_ref, 0, xs_hbm, lo, pad_sem).wait()
                return c

            lax.fori_loop(lo, hi, start, 0)
            lax.fori_loop(lo, hi, wait, 0)
            return carry

        lax.fori_loop(0, N_EXPERTS, per_expert, 0)

    def body(r, carry):
        _row_copy(h_ref, r, xs_hbm, dest_ref[base + r], sem).start()
        _row_copy(h_ref, r, xs_hbm, dest_ref[t_rows + base + r], sem).start()
        return carry

    lax.fori_loop(0, tm, body, 0, unroll=8)
    whole = pltpu.make_async_copy(h_ref, xs_hbm.at[pl.ds(0, tm), :], sem)
    whole.wait()
    whole.wait()


def _dispatch(dest, pad_lo, pad_hi, h2, p_rows):
    t_rows, d = h2.shape
    tm = 512
    grid_spec = pltpu.PrefetchScalarGridSpec(
        num_scalar_prefetch=3,
        grid=(t_rows // tm,),
        in_specs=[pl.BlockSpec((tm, d), lambda i, *_: (i, 0))],
        out_specs=pl.BlockSpec(memory_space=pl.ANY),
        scratch_shapes=[pltpu.SemaphoreType.DMA, pltpu.SemaphoreType.DMA],
    )
    return pl.pallas_call(
        functools.partial(_dispatch_kernel, tm, t_rows),
        out_shape=jax.ShapeDtypeStruct((p_rows, d), F32),
        grid_spec=grid_spec,
        compiler_params=_params(("arbitrary",)),
        name="moe_dispatch",
    )(dest, pad_lo, pad_hi, h2)


def _combine_kernel(tm, t_rows, n_lat_tiles_per_b, n_b, dest_ref, x1_ref, wt_ref, mod_ref, ys_hbm, o_ref,
                    buf, sem):
    i = pl.program_id(0)
    n_steps = pl.num_programs(0)
    d = x1_ref.shape[1]

    def issue(step, slot):
        base = step * tm

        def body(r, carry):
            _row_copy(ys_hbm, dest_ref[base + r], buf.at[slot, 0], r, sem.at[slot]).start()
            _row_copy(ys_hbm, dest_ref[t_rows + base + r], buf.at[slot, 1], r, sem.at[slot]).start()
            return carry

        lax.fori_loop(0, tm, body, 0, unroll=8)

    @pl.when(i == 0)
    def _():
        issue(0, 0)

    @pl.when(i + 1 < n_steps)
    def _():
        issue(i + 1, (i + 1) % 2)

    slot = i % 2
    for k in range(2):
        pltpu.make_async_copy(ys_hbm.at[pl.ds(0, tm), :], buf.at[slot, k], sem.at[slot]).wait()
    bi = jnp.minimum(i // n_lat_tiles_per_b, n_b)
    g2 = mod_ref[0, pl.ds(bi, 1), 5 * d:6 * d]
    wt = wt_ref[...]
    y = buf[slot, 0] * wt[:, 0:1] + buf[slot, 1] * wt[:, 1:2]
    o_ref[...] = x1_ref[...] + g2 * y


def _combine(dest, x1, wts_t, mod, layer, ys, n_b, seq):
    t_rows, d = x1.shape
    tm = 256
    grid_spec = pltpu.PrefetchScalarGridSpec(
        num_scalar_prefetch=1,
        grid=(t_rows // tm,),
        in_specs=[pl.BlockSpec((tm, d), lambda i, *_: (i, 0)),
                  pl.BlockSpec((tm, 2), lambda i, *_: (i, 0)),
                  pl.BlockSpec((1,) + mod.shape[1:], lambda i, *_: (layer, 0, 0)),
                  pl.BlockSpec(memory_space=pl.ANY)],
        out_specs=pl.BlockSpec((tm, d), lambda i, *_: (i, 0)),
        scratch_shapes=[pltpu.VMEM((2, 2, tm, d), F32), pltpu.SemaphoreType.DMA((2,))],
    )
    return pl.pallas_call(
        functools.partial(_combine_kernel, tm, t_rows, seq // tm, n_b),
        out_shape=jax.ShapeDtypeStruct((t_rows, d), F32),
        grid_spec=grid_spec,
        compiler_params=_params(("arbitrary",)),
        name="moe_combine",
    )(dest, x1, wts_t, mod, ys)


def _ffn_kernel(be_ref, x_ref, wg_ref, wu_ref, wd_ref, o_ref):
    x = x_ref[...].astype(BF16)
    gate = jnp.dot(x, wg_ref[0], preferred_element_type=F32)
    up = jnp.dot(x, wu_ref[0], preferred_element_type=F32)
    hid = (_silu(gate) * up).astype(BF16)
    o_ref[...] = jnp.dot(hid, wd_ref[0], preferred_element_type=F32)


def _expert_ffn(blk_expert, xsorted, w_gate, w_up, w_down):
    p_rows, d = xsorted.shape
    n_blk = p_rows // MOE_BLOCK
    de = w_gate.shape[-1]
    grid_spec = pltpu.PrefetchScalarGridSpec(
        num_scalar_prefetch=1,
        grid=(n_blk,),
        in_specs=[pl.BlockSpec((MOE_BLOCK, d), lambda i, be: (i, 0)),
                  pl.BlockSpec((1, d, de), lambda i, be: (be[i], 0, 0)),
                  pl.BlockSpec((1, d, de), lambda i, be: (be[i], 0, 0)),
                  pl.BlockSpec((1, de, d), lambda i, be: (be[i], 0, 0))],
        out_specs=pl.BlockSpec((MOE_BLOCK, d), lambda i, be: (i, 0)),
    )
    return pl.pallas_call(
        _ffn_kernel,
        out_shape=jax.ShapeDtypeStruct((p_rows, d), F32),
        grid_spec=grid_spec,
        compiler_params=_params(("arbitrary",)),
        name="expert_ffn",
    )(blk_expert, xsorted, w_gate, w_up, w_down)


def _final_norm_kernel(x_ref, w_ref, o_ref):
    x = x_ref[...]
    o_ref[...] = x * lax.rsqrt(jnp.mean(x * x, axis=-1, keepdims=True) + EPS) * w_ref[...]


def _final_norm(xs, w, n_rows):
    d = xs.shape[1]
    tm = 512
    return pl.pallas_call(
        _final_norm_kernel,
        out_shape=jax.ShapeDtypeStruct((n_rows, d), F32),
        grid=(n_rows // tm,),
        in_specs=[pl.BlockSpec((tm, d), lambda t: (t, 0)), pl.BlockSpec((1, d), lambda t: (0, 0))],
        out_specs=pl.BlockSpec((tm, d), lambda t: (t, 0)),
        compiler_params=_params(("parallel",)),
        name="final_norm",
    )(xs, w)


def _sincos_2d(length, dim):
    rows = length // GRID_W
    quarter = dim // 4
    omega = 1.0 / (POS_BASE ** (jnp.arange(quarter, dtype=F32) / quarter))
    ang_r = jnp.arange(rows, dtype=F32)[:, None] * omega
    ang_c = jnp.arange(GRID_W, dtype=F32)[:, None] * omega
    emb_r = jnp.concatenate([jnp.sin(ang_r), jnp.cos(ang_r)], axis=-1)
    emb_c = jnp.concatenate([jnp.sin(ang_c), jnp.cos(ang_c)], axis=-1)
    half = dim // 2
    emb = jnp.concatenate([jnp.broadcast_to(emb_r[:, None, :], (rows, GRID_W, half)),
                           jnp.broadcast_to(emb_c[None, :, :], (rows, GRID_W, half))], axis=-1)
    return emb.reshape(rows * GRID_W, dim)


def _moe(h2, x1, idx, wts, counts, mod, layer, w_gate, w_up, w_down, n_b, seq):
    t_rows, d = h2.shape
    a = 2 * t_rows
    counts = counts[:, 0]
    padded = (counts + MOE_BLOCK - 1) // MOE_BLOCK * MOE_BLOCK
    pad_end = jnp.cumsum(padded)
    pad_start = pad_end - padded
    n_blk = (a + N_EXPERTS * (MOE_BLOCK - 1) + MOE_BLOCK - 1) // MOE_BLOCK
    p_rows = n_blk * MOE_BLOCK
    blk_start = jnp.arange(n_blk, dtype=jnp.int32) * MOE_BLOCK
    blk_expert = jnp.minimum(jnp.sum(blk_start[:, None] >= pad_end[None, :], axis=1),
                             N_EXPERTS - 1).astype(jnp.int32)
    e_ids = jnp.arange(N_EXPERTS, dtype=jnp.int32)
    seg_start = jnp.sum(jnp.where(idx[0:2, :, None] == e_ids, pad_start, 0), axis=-1)
    dest = (seg_start + idx[2:4]).reshape(-1).astype(jnp.int32)
    pad_lo = (pad_start + counts).astype(jnp.int32)
    pad_hi = pad_end.at[N_EXPERTS - 1].set(p_rows).astype(jnp.int32)
    xsorted = _dispatch(dest, pad_lo, pad_hi, h2, p_rows)
    ys = _expert_ffn(blk_expert, xsorted, w_gate, w_up, w_down)
    return _combine(dest, x1, wts.T, mod, layer, ys, n_b, seq)


def kernel(x, c, ctx, c_ctx, ada_w, ada_b, norm_mix_w, norm_ffn_w, w_in, conv_w, a_log, dt_bias,
           dn_norm_w, ft_norm_w, w_out, router_w, router_bias, w_gate, w_up, w_down, final_norm_w):
    n_b, seq, d = x.shape
    ctx_len = ctx.shape[1]
    depth = ada_w.shape[0]
    n_lat = n_b * seq
    n_tok = n_lat + n_b * ctx_len
    ft_off = 4 * DN_WIDTH + N_GATE_COLS

    xs = jnp.concatenate([(x + _sincos_2d(seq, d)[None]).reshape(n_lat, d),
                          ctx.reshape(n_b * ctx_len, d)], axis=0)

    mod_rows = -(-(n_b + 1) // 8) * 8
    cvec = jnp.zeros((mod_rows, d), F32).at[:n_b].set(c).at[n_b].set(c_ctx)
    mod = _ada(cvec, ada_w, ada_b)

    cos_l, sin_l = _dft_tables(seq)
    cos_c, sin_c = _dft_tables(ctx_len)
    cc, sc = _dft_tables(FT_GROUP_DIM)
    dft_cs = jnp.concatenate([cc, sc], axis=1)
    router_wt = router_w.T
    rbias = router_bias.reshape(N_EXPERTS, 1)
    wg_b, wu_b, wd_b = w_gate.astype(BF16), w_up.astype(BF16), w_down.astype(BF16)

    col = jnp.arange(N_GATE_COLS)
    col_head, col_dir, col_ab = col // 4, (col // 2) % 2, col % 2
    gate_src = col_dir * (2 * N_DN_HEADS) + col_ab * N_DN_HEADS + col_head
    is_a = (col_ab == 0).astype(F32)
    is_bwd = (col_dir == 1).astype(F32)

    for i in range(depth):
        last = i == depth - 1
        w = w_in[i]
        wab = w[:, 4 * DN_WIDTH:ft_off][:, gate_src].astype(BF16)
        neg_a = -jnp.exp(a_log[i])[col_dir, col_head] * is_a
        dtb = dt_bias[i][col_dir, col_head] * is_a
        gpar = jnp.stack([neg_a, dtb, is_a, is_bwd])
        gpar_pad = jnp.pad(gpar, ((0, 0), (0, GATE_PAD - N_GATE_COLS)))
        wab_pad = jnp.pad(wab, ((0, 0), (0, GATE_PAD - N_GATE_COLS)))
        qkv, z, xc, xsn, gcol, grow = _in_proj(
            xs, mod, i, norm_mix_w[i].reshape(1, d),
            w[:, :3 * DN_WIDTH].astype(BF16), w[:, 3 * DN_WIDTH:4 * DN_WIDTH].astype(BF16),
            w[:, ft_off:].astype(BF16), wab_pad, wab.T, gpar_pad, gpar.T, dft_cs, n_b, seq)
        dn_l, dn_c = _deltanet(qkv, z, gcol, grow, conv_w[i], dn_norm_w[i].reshape(1, DN_HEAD_DIM),
                               n_b, seq, ctx_len)
        fnw = ft_norm_w[i].reshape(1, FT_WIDTH)
        ft_l = _fourier(xc, xsn, cos_l, sin_l, fnw, n_b, seq, 0)
        if last:
            dn, ft, rows = dn_l, ft_l, n_lat
        else:
            ft_c = _fourier(xc, xsn, cos_c, sin_c, fnw, n_b, ctx_len, n_lat // ctx_len)
            dn = jnp.concatenate([dn_l, dn_c], axis=0)
            ft = jnp.concatenate([ft_l, ft_c], axis=0)
            rows = n_tok
        wo = w_out[i].astype(BF16)
        x1, h2, logits_t = _out_proj(xs, dn, ft, wo[:DN_WIDTH], wo[DN_WIDTH:], mod, i,
                                     norm_ffn_w[i].reshape(1, d), router_wt, n_b, seq, rows)
        idx, wts, counts = _route(logits_t, rbias)
        xs = _moe(h2, x1, idx, wts, counts, mod, i, wg_b[i], wu_b[i], wd_b[i], n_b, seq)
    return _final_norm(xs, final_norm_w.reshape(1, d), n_lat).reshape(n_b, seq, d)
```

```python
import functools
import math

import jax
import jax.numpy as jnp
from jax import lax
from jax.experimental import pallas as pl
from jax.experimental.pallas import tpu as pltpu

F32 = jnp.float32
BF16 = jnp.bfloat16

GRID_W = 64
N_DN_HEADS = 4
DN_HEAD_DIM = 128
DN_WIDTH = N_DN_HEADS * DN_HEAD_DIM
N_FT_GROUPS = 4
FT_GROUP_DIM = 128
FT_WIDTH = N_FT_GROUPS * FT_GROUP_DIM
CONV_K = 5
CHUNK = 64
N_EXPERTS = 16
N_EXPERT_GROUPS = 4
EXPERTS_PER_GROUP = N_EXPERTS // N_EXPERT_GROUPS
D_EXPERT = 512
MOE_BLOCK = 256
POS_BASE = 10000.0
EPS = 1e-6
N_GATE_COLS = 4 * N_DN_HEADS
GATE_PAD = 128
CONV_BLOCK = 256
DN_UNROLL = 16

LANES = 128
SUBLANES = 8
VMEM_LIMIT = 56 * 1024 * 1024


def _params(sem, vmem=VMEM_LIMIT):
    return pltpu.CompilerParams(dimension_semantics=sem, vmem_limit_bytes=vmem)


def _dot(a, b):
    return jnp.dot(a.astype(BF16), b.astype(BF16), preferred_element_type=F32)


def _dot_nt(a, b):
    return lax.dot_general(a.astype(BF16), b.astype(BF16), (((1,), (1,)), ((), ())),
                           preferred_element_type=F32)


def _sigmoid(x):
    return 1.0 / (1.0 + jnp.exp(-x))


def _silu(x):
    return x * _sigmoid(x)


def _softplus(x):
    return jnp.maximum(x, 0.0) + jnp.log(1.0 + jnp.exp(-jnp.abs(x)))


def _ada_kernel(c_ref, w_ref, b_ref, o_ref):
    act = _silu(c_ref[...])
    o_ref[0] = jnp.dot(act, w_ref[0], preferred_element_type=F32,
                       precision=lax.Precision.HIGHEST) + b_ref[0]


def _ada(cvec, ada_w, ada_b):
    depth, d, n = ada_w.shape
    rows = cvec.shape[0]
    tn = 1024
    return pl.pallas_call(
        _ada_kernel,
        out_shape=jax.ShapeDtypeStruct((depth, rows, n), F32),
        grid=(depth, n // tn),
        in_specs=[pl.BlockSpec((rows, d), lambda i, j: (0, 0)),
                  pl.BlockSpec((1, d, tn), lambda i, j: (i, 0, j)),
                  pl.BlockSpec((1, 1, tn), lambda i, j: (i, 0, j))],
        out_specs=pl.BlockSpec((1, rows, tn), lambda i, j: (i, 0, j)),
        compiler_params=_params(("parallel", "parallel")),
        name="ada_mod",
    )(cvec, ada_w, ada_b.reshape(depth, 1, n))


def _seg_scan(x, pos, axis, reverse):
    n = x.shape[axis]
    s = 1
    while s < CHUNK:
        if reverse:
            shifted = pltpu.roll(x, n - s, axis)
            x = x + jnp.where(pos < CHUNK - s, shifted, 0.0)
        else:
            shifted = pltpu.roll(x, s, axis)
            x = x + jnp.where(pos >= s, shifted, 0.0)
        s *= 2
    return x


def _in_proj_kernel(tm, n_lat_tiles_per_b, n_b, x_ref, mod_ref, nw_ref, wqkv_ref, wz_ref, wft_ref,
                    wab_ref, wabt_ref, gpar_ref, gpart_ref, dft_ref,
                    qkv_ref, z_ref, xc_ref, xs_ref, gcol_ref, grow_ref):
    t = pl.program_id(0)
    bi = jnp.minimum(t // n_lat_tiles_per_b, n_b)
    d = x_ref.shape[1]
    x = x_ref[...]
    y = x * lax.rsqrt(jnp.mean(x * x, axis=-1, keepdims=True) + EPS) * nw_ref[...]
    shift = mod_ref[0, pl.ds(bi, 1), 0:d]
    scale = mod_ref[0, pl.ds(bi, 1), d:2 * d]
    h = (y * (1.0 + scale) + shift).astype(BF16)

    for j in range(3):
        cs = slice(j * DN_WIDTH, (j + 1) * DN_WIDTH)
        qkv_ref[:, cs] = jnp.dot(h, wqkv_ref[:, cs], preferred_element_type=F32)
    z_ref[...] = jnp.dot(h, wz_ref[...], preferred_element_type=F32)

    ft = jnp.dot(h, wft_ref[...], preferred_element_type=F32).astype(BF16)
    for g in range(N_FT_GROUPS):
        cs = slice(g * FT_GROUP_DIM, (g + 1) * FT_GROUP_DIM)
        cssn = jnp.dot(ft[:, cs], dft_ref[...], preferred_element_type=F32)
        xc_ref[:, cs] = cssn[:, :FT_GROUP_DIM].astype(BF16)
        xs_ref[:, cs] = cssn[:, FT_GROUP_DIM:].astype(BF16)

    ab = jnp.dot(h, wab_ref[...], preferred_element_type=F32)
    abt = lax.dot_general(wabt_ref[...], h, (((1,), (1,)), ((), ())), preferred_element_type=F32)

    def gates(v, par, axis):
        neg_a, dtb, is_a, is_bwd = par
        g = neg_a * _softplus(v + dtb)
        pos = lax.broadcasted_iota(jnp.int32, v.shape, axis) % CHUNK
        fwd = _seg_scan(g, pos, axis, reverse=False)
        bwd = _seg_scan(g, pos, axis, reverse=True)
        cum = jnp.where(is_bwd > 0.5, bwd, fwd)
        return jnp.where(is_a > 0.5, cum, _sigmoid(v))

    gp = gpar_ref[...]
    gc = gates(ab, (gp[0:1], gp[1:2], gp[2:3], gp[3:4]), 0)
    gpt = gpart_ref[...]
    gt = gates(abt, (gpt[:, 0:1], gpt[:, 1:2], gpt[:, 2:3], gpt[:, 3:4]), 1)
    for hh in range(N_DN_HEADS):
        gcol_ref[hh] = gc[:, 4 * hh:4 * hh + 4]
        for j in range(tm // CHUNK):
            cs = slice(j * CHUNK, (j + 1) * CHUNK)
            grow_ref[hh, j] = jnp.concatenate([gt[4 * hh:4 * hh + 1, cs], gt[4 * hh + 2:4 * hh + 3, cs]], axis=1)


def _in_proj(xs, mod_i, layer, norm_w, wqkv, wz, wft, wab, wabt, gpar, gpart, dft_cs, n_b, seq):
    t_rows, d = xs.shape
    tm = 512
    n_tiles = t_rows // tm
    kern = functools.partial(_in_proj_kernel, tm, seq // tm, n_b)
    const = lambda *shape: pl.BlockSpec(shape, lambda t: tuple(0 for _ in shape))
    rows = lambda w: pl.BlockSpec((tm, w), lambda t: (t, 0))
    return pl.pallas_call(
        kern,
        out_shape=(jax.ShapeDtypeStruct((t_rows, 3 * DN_WIDTH), F32),
                   jax.ShapeDtypeStruct((t_rows, DN_WIDTH), F32),
                   jax.ShapeDtypeStruct((t_rows, FT_WIDTH), BF16),
                   jax.ShapeDtypeStruct((t_rows, FT_WIDTH), BF16),
                   jax.ShapeDtypeStruct((N_DN_HEADS, t_rows, 4), F32),
                   jax.ShapeDtypeStruct((N_DN_HEADS, t_rows // CHUNK, 1, 2 * CHUNK), F32)),
        grid=(n_tiles,),
        in_specs=[rows(d),
                  pl.BlockSpec((1,) + mod_i.shape[1:], lambda t: (layer, 0, 0)),
                  const(1, d),
                  const(d, 3 * DN_WIDTH), const(d, DN_WIDTH), const(d, FT_WIDTH),
                  const(d, GATE_PAD), const(N_GATE_COLS, d),
                  const(4, GATE_PAD), const(N_GATE_COLS, 4),
                  const(FT_GROUP_DIM, 2 * FT_GROUP_DIM)],
        out_specs=(rows(3 * DN_WIDTH), rows(DN_WIDTH), rows(FT_WIDTH), rows(FT_WIDTH),
                   pl.BlockSpec((N_DN_HEADS, tm, 4), lambda t: (0, t, 0)),
                   pl.BlockSpec((N_DN_HEADS, tm // CHUNK, 1, 2 * CHUNK), lambda t: (0, t, 0, 0))),
        compiler_params=_params(("parallel",)),
        name="in_proj",
    )(xs, mod_i, norm_w, wqkv, wz, wft, wab, wabt, gpar, gpart, dft_cs)


def _conv_prep(src_ref, cw, n_rows, dst_ref, mode):
    blk = CONV_BLOCK
    n_blk = n_rows // blk
    halo = 8
    n_win = blk + 2 * halo

    def body(c, carry):
        r0 = pl.multiple_of(c * blk, blk)
        main = src_ref[pl.ds(r0, blk), :]
        prev = src_ref[pl.ds(pl.multiple_of(jnp.maximum(r0 - halo, 0), halo), halo), :]
        nxt = src_ref[pl.ds(pl.multiple_of(jnp.minimum(r0 + blk, n_rows - halo), halo), halo), :]
        prev = jnp.where(c > 0, prev, 0.0)
        nxt = jnp.where(c < n_blk - 1, nxt, 0.0)
        win = jnp.concatenate([prev, main, nxt], axis=0)
        acc = jnp.zeros((blk, DN_HEAD_DIM), F32)
        for j in range(CONV_K):
            shift = (CONV_K // 2 - j) % n_win
            rolled = win if shift == 0 else pltpu.roll(win, shift, 0)
            acc = acc + rolled[halo:halo + blk] * cw[j:j + 1, :]
        y = _silu(acc)
        if mode != "v":
            y = y * lax.rsqrt(jnp.sum(y * y, axis=-1, keepdims=True) + EPS)
        if mode == "q":
            y = y * (DN_HEAD_DIM ** -0.5)
        dst_ref[pl.ds(r0, blk), :] = y
        return carry

    lax.fori_loop(0, n_blk, body, 0)


def _blockdiag(x, isb):
    return jnp.concatenate([jnp.where(isb, 0.0, x), jnp.where(isb, x, 0.0)], axis=0).astype(BF16)


def _blockdiag_wide(x):
    w = x.shape[1] // 2
    zero = jnp.zeros((x.shape[0], w), BF16)
    xb = x.astype(BF16)
    return jnp.concatenate([jnp.concatenate([xb[:, :w], zero], axis=1),
                            jnp.concatenate([zero, xb[:, w:]], axis=1)], axis=0)


def _tri_inverse_dual(a_list, eye, xor, isb):
    mm = lambda x, y: jnp.dot(x.astype(BF16), _blockdiag(y, isb), preferred_element_type=F32)
    ad = [jnp.where((xor >> 3) == 0, a, 0.0) for a in a_list]
    a2 = [mm(x, x) for x in ad]
    a4 = [mm(x, x) for x in a2]
    t = [eye - x for x in ad]
    t = [x + mm(x, y) for x, y in zip(t, a2)]
    t = [x + mm(x, y) for x, y in zip(t, a4)]
    for s in (3, 4, 5):
        off = [jnp.where((xor >> s) == 1, a, 0.0) for a in a_list]
        to = [mm(x, y) for x, y in zip(t, off)]
        t = [x - mm(y, x) for x, y in zip(t, to)]
    return t


def _chunks_local(loaded):
    hd = DN_HEAD_DIM
    n = len(loaded)
    q = [x[0] for x in loaded]
    k = [x[1] for x in loaded]
    v = [x[2] for x in loaded]
    wide = lambda col: jnp.broadcast_to(col, (CHUNK, hd))
    gf = [wide(x[3][:, 0:1]) for x in loaded]
    bf_ = [wide(x[3][:, 1:2]) for x in loaded]
    gb = [wide(x[3][:, 2:3]) for x in loaded]
    bb = [wide(x[3][:, 3:4]) for x in loaded]
    grow = [x[4] for x in loaded]
    row = lax.broadcasted_iota(jnp.int32, (CHUNK, 2 * CHUNK), 0)
    lane = lax.broadcasted_iota(jnp.int32, (CHUNK, 2 * CHUNK), 1)
    jl = lane & (CHUNK - 1)
    isb = lane >= CHUNK
    delta = jnp.where(isb, jl - row, row - jl)
    xor = row ^ jl
    eye = jnp.where(delta == 0, 1.0, 0.0)
    nt = (((1,), (1,)), ((), ()))
    k2 = [jnp.concatenate([x, x], axis=0).astype(BF16) for x in k]
    kkd = [lax.dot_general(k[i].astype(BF16), k2[i], nt, preferred_element_type=F32) for i in range(n)]
    qkd = [lax.dot_general(q[i].astype(BF16), k2[i], nt, preferred_element_type=F32) for i in range(n)]
    dec = [jnp.where(delta >= 0, jnp.exp(jnp.where(isb, gb[i], gf[i]) - grow[i]), 0.0) for i in range(n)]
    a_mat = [jnp.where(delta > 0, kkd[i] * jnp.where(isb, bb[i], bf_[i]) * dec[i], 0.0) for i in range(n)]
    t_inv = _tri_inverse_dual(a_mat, eye, xor, isb)
    egf = [jnp.exp(x) for x in gf]
    egb = [jnp.exp(x) for x in gb]
    rhs = [jnp.concatenate([v[i] * bf_[i], k[i] * (bf_[i] * egf[i]), v[i] * bb[i], k[i] * (bb[i] * egb[i])],
                           axis=1) for i in range(n)]
    sol = [jnp.dot(t_inv[i].astype(BF16), _blockdiag_wide(rhs[i]), preferred_element_type=F32)
           for i in range(n)]
    bd_sol = [_blockdiag_wide(x) for x in sol]
    r1 = [jnp.dot((qkd[i] * dec[i]).astype(BF16), bd_sol[i], preferred_element_type=F32)
          for i in range(n)]
    glf = [x[CHUNK - 1:CHUNK, :] for x in gf]
    glb = [x[0:1, :] for x in gb]
    kdec = [jnp.concatenate([k[i] * jnp.exp(glf[i] - gf[i]), k[i] * jnp.exp(glb[i] - gb[i])], axis=0)
            for i in range(n)]
    r2 = [jnp.dot(kdec[i].T.astype(BF16), bd_sol[i], preferred_element_type=F32)
          for i in range(n)]
    out = []
    for i in range(n):
        o_loc = r1[i][:, 0:hd] + r1[i][:, 2 * hd:3 * hd]
        qt = jnp.concatenate([q[i] * egf[i] - r1[i][:, hd:2 * hd], q[i] * egb[i] - r1[i][:, 3 * hd:]],
                             axis=1).astype(BF16)
        nn = jnp.concatenate([r2[i][:, 0:hd], r2[i][:, 2 * hd:3 * hd]], axis=1)
        kw = jnp.concatenate([r2[i][:, hd:2 * hd], r2[i][:, 3 * hd:]], axis=1).astype(BF16)
        ge = jnp.concatenate([jnp.exp(glf[i]), jnp.exp(glb[i])], axis=0)
        out.append((o_loc, qt, nn, kw, ge))
    return out


def _state_step(c, d, oacc, qt_ref, kw_ref, nn_ref, ge_ref, s_ref):
    hd = DN_HEAD_DIM
    r0 = pl.multiple_of(c * CHUNK, CHUNK)
    cs = slice(d * hd, (d + 1) * hd)
    s = s_ref[d]
    lhs = jnp.concatenate([qt_ref[c, :, cs], kw_ref[c, :, cs]], axis=0)
    r = jnp.dot(lhs, s.astype(BF16), preferred_element_type=F32)
    oacc[pl.ds(r0, CHUNK), :] += r[:CHUNK]
    s_ref[d] = s * ge_ref[c, d:d + 1, :] + nn_ref[c, :, cs] - r[CHUNK:]


def _dn_kernel(seq, ctx_len,
               ql_ref, kl_ref, vl_ref, qc_ref, kc_ref, vc_ref, cwq_ref, cwk_ref, cwv_ref,
               zl_ref, zc_ref, gcl_ref, gcc_ref, grl_ref, grc_ref, nw_ref,
               ol_ref, oc_ref,
               qn, kn, vn, oacc, qt_s, kw_s, nn_s, ge_s, s_s):
    nw = nw_ref[...]
    s_s[...] = jnp.zeros_like(s_s)

    def segment(n_rows, q_ref, k_ref, v_ref, z_ref, gcol_ref, grow_ref, out_ref):
        _conv_prep(q_ref, cwq_ref[...], n_rows, qn, "q")
        _conv_prep(k_ref, cwk_ref[...], n_rows, kn, "k")
        _conv_prep(v_ref, cwv_ref[...], n_rows, vn, "v")
        n_chunks = n_rows // CHUNK

        unroll = math.gcd(DN_UNROLL, n_chunks)

        def local(i, carry):
            chunks = [i * unroll + j for j in range(unroll)]
            rows = [pl.ds(pl.multiple_of(c * CHUNK, CHUNK), CHUNK) for c in chunks]
            loaded = [(qn[r, :], kn[r, :], vn[r, :], gcol_ref[0, r, :], grow_ref[0, c])
                      for c, r in zip(chunks, rows)]
            results = _chunks_local(loaded)
            for c, r, (o_loc, qt, nn, kw, ge) in zip(chunks, rows, results):
                oacc[r, :] = o_loc
                qt_s[c] = qt
                nn_s[c] = nn
                kw_s[c] = kw
                ge_s[c] = ge
            return carry

        lax.fori_loop(0, n_chunks // unroll, local, 0)

        def step(s, carry):
            _state_step(s, 0, oacc, qt_s, kw_s, nn_s, ge_s, s_s)
            _state_step(n_chunks - 1 - s, 1, oacc, qt_s, kw_s, nn_s, ge_s, s_s)
            return carry

        lax.fori_loop(0, n_chunks, step, 0)

        blk = 256

        def fin(i, carry):
            r0 = pl.multiple_of(i * blk, blk)
            o = oacc[pl.ds(r0, blk), :]
            o = o * lax.rsqrt(jnp.mean(o * o, axis=-1, keepdims=True) + EPS) * nw
            out_ref[pl.ds(r0, blk), :] = (o * _silu(z_ref[pl.ds(r0, blk), :])).astype(out_ref.dtype)
            return carry

        lax.fori_loop(0, n_rows // blk, fin, 0)

    segment(ctx_len, qc_ref, kc_ref, vc_ref, zc_ref, gcc_ref, grc_ref, oc_ref)
    segment(seq, ql_ref, kl_ref, vl_ref, zl_ref, gcl_ref, grl_ref, ol_ref)


def _deltanet(qkv, z, gcol, grow, conv_w, dn_norm_w, n_b, seq, ctx_len):
    hd = DN_HEAD_DIM
    nc = seq // CHUNK
    cb = n_b * seq // ctx_len
    lat = lambda off: pl.BlockSpec((seq, hd), lambda b, h: (b, h + off))
    ctx = lambda off: pl.BlockSpec((ctx_len, hd), lambda b, h: (cb + b, h + off))
    cws = lambda off: pl.BlockSpec((CONV_K, hd), lambda b, h: (0, h + off))
    kern = functools.partial(_dn_kernel, seq, ctx_len)
    return pl.pallas_call(
        kern,
        out_shape=(jax.ShapeDtypeStruct((n_b * seq, DN_WIDTH), BF16),
                   jax.ShapeDtypeStruct((n_b * ctx_len, DN_WIDTH), BF16)),
        grid=(n_b, N_DN_HEADS),
        in_specs=[lat(0), lat(N_DN_HEADS), lat(2 * N_DN_HEADS),
                  ctx(0), ctx(N_DN_HEADS), ctx(2 * N_DN_HEADS),
                  cws(0), cws(N_DN_HEADS), cws(2 * N_DN_HEADS),
                  lat(0), ctx(0),
                  pl.BlockSpec((1, seq, 4), lambda b, h: (h, b, 0)),
                  pl.BlockSpec((1, ctx_len, 4), lambda b, h: (h, cb + b, 0)),
                  pl.BlockSpec((1, nc, 1, 2 * CHUNK), lambda b, h: (h, b, 0, 0)),
                  pl.BlockSpec((1, ctx_len // CHUNK, 1, 2 * CHUNK), lambda b, h: (h, cb + b, 0, 0)),
                  pl.BlockSpec((1, hd), lambda b, h: (0, 0))],
        out_specs=(pl.BlockSpec((seq, hd), lambda b, h: (b, h)),
                   pl.BlockSpec((ctx_len, hd), lambda b, h: (b, h))),
        scratch_shapes=[pltpu.VMEM((seq, hd), F32) for _ in range(4)]
        + [pltpu.VMEM((nc, CHUNK, 2 * hd), BF16), pltpu.VMEM((nc, hd, 2 * hd), BF16),
           pltpu.VMEM((nc, hd, 2 * hd), F32), pltpu.VMEM((nc, 2, hd), F32),
           pltpu.VMEM((2, hd, hd), F32)],
        compiler_params=_params(("parallel", "parallel")),
        name="deltanet",
    )(qkv, qkv, qkv, qkv, qkv, qkv, conv_w, conv_w, conv_w, z, z, gcol, gcol, grow, grow, dn_norm_w)


def _ft_kernel(scale, cl_ref, sl_ref, xc_ref, xs_ref, nw_ref, o_ref):
    y = (jnp.dot(cl_ref[...], xc_ref[...], preferred_element_type=F32)
         - jnp.dot(sl_ref[...], xs_ref[...], preferred_element_type=F32)) * scale
    y = y * lax.rsqrt(jnp.mean(y * y, axis=-1, keepdims=True) + EPS) * nw_ref[...]
    o_ref[...] = y.astype(o_ref.dtype)


def _fourier(xc, xs, cos_m, sin_m, ft_norm_w, n_b, length, row_block0):
    tm = min(512, length)
    n_m = length // tm
    scale = 1.0 / math.sqrt(length * FT_GROUP_DIM)
    return pl.pallas_call(
        functools.partial(_ft_kernel, scale),
        out_shape=jax.ShapeDtypeStruct((n_b * length, FT_WIDTH), BF16),
        grid=(n_b, n_m),
        in_specs=[pl.BlockSpec((tm, length), lambda b, m: (m, 0)),
                  pl.BlockSpec((tm, length), lambda b, m: (m, 0)),
                  pl.BlockSpec((length, FT_WIDTH), lambda b, m: (row_block0 + b, 0)),
                  pl.BlockSpec((length, FT_WIDTH), lambda b, m: (row_block0 + b, 0)),
                  pl.BlockSpec((1, FT_WIDTH), lambda b, m: (0, 0))],
        out_specs=pl.BlockSpec((tm, FT_WIDTH), lambda b, m: (b * n_m + m, 0)),
        compiler_params=_params(("parallel", "parallel")),
        name="fourier_seq",
    )(cos_m, sin_m, xc, xs, ft_norm_w)


def _dft_tables(length):
    m = jnp.arange(length, dtype=jnp.int32)
    ph = (m[:, None] * m[None, :]) % length
    ang = ph.astype(F32) * (2.0 * math.pi / length)
    return jnp.cos(ang).astype(BF16), jnp.sin(ang).astype(BF16)


def _store_token_tiles(ref, val):
    rows, d = val.shape
    n_lt = d // LANES
    for j in range(n_lt):
        ref[pl.ds(j, rows, stride=n_lt), :] = val[:, j * LANES:(j + 1) * LANES]


def _load_token_tiles(ref, rows, n_lt):
    return jnp.concatenate([ref[pl.ds(j, rows, stride=n_lt), :] for j in range(n_lt)], axis=1)


def _out_proj_kernel(n_lat_tiles, n_lat_tiles_per_b, n_b, x_ref, dnl_ref, dnc_ref, ftl_ref, ftc_ref,
                     wdn_ref, wft_ref, mod_ref, nw_ref, rwt_ref, x1_ref, h2_ref, lg_ref):
    t = pl.program_id(0)
    bi = jnp.minimum(t // n_lat_tiles_per_b, n_b)
    d = x_ref.shape[1]
    is_lat = t < n_lat_tiles
    dn = jnp.where(is_lat, dnl_ref[...], dnc_ref[...])
    ft = jnp.where(is_lat, ftl_ref[...], ftc_ref[...])
    mix = (jnp.dot(dn, wdn_ref[...], preferred_element_type=F32)
           + jnp.dot(ft, wft_ref[...], preferred_element_type=F32))
    g1 = mod_ref[0, pl.ds(bi, 1), 2 * d:3 * d]
    sh2 = mod_ref[0, pl.ds(bi, 1), 3 * d:4 * d]
    sc2 = mod_ref[0, pl.ds(bi, 1), 4 * d:5 * d]
    x1 = x_ref[...] + g1 * mix
    x1_ref[...] = x1
    y = x1 * lax.rsqrt(jnp.mean(x1 * x1, axis=-1, keepdims=True) + EPS) * nw_ref[...]
    h2 = y * (1.0 + sc2) + sh2
    _store_token_tiles(h2_ref, h2)
    lg_ref[...] = lax.dot_general(rwt_ref[...], h2, (((1,), (1,)), ((), ())),
                                  preferred_element_type=F32, precision=lax.Precision.HIGHEST)


def _out_proj(xs, dn_l, dn_c, ft_l, ft_c, wdn, wft, mod_i, layer, norm_w, router_wt, n_b, seq, t_rows):
    d = xs.shape[1]
    tm = 512
    n_lt = d // LANES
    n_lat_tiles = n_b * seq // tm
    kern = functools.partial(_out_proj_kernel, n_lat_tiles, seq // tm, n_b)
    const = lambda *shape: pl.BlockSpec(shape, lambda t: tuple(0 for _ in shape))
    rows = lambda w: pl.BlockSpec((tm, w), lambda t: (t, 0))
    lat = lambda w: pl.BlockSpec((tm, w), lambda t: (jnp.minimum(t, n_lat_tiles - 1), 0))
    ctx = lambda w: pl.BlockSpec((tm, w), lambda t: (jnp.maximum(t - n_lat_tiles, 0), 0))
    return pl.pallas_call(
        kern,
        out_shape=(jax.ShapeDtypeStruct((t_rows, d), F32),
                   jax.ShapeDtypeStruct((t_rows * n_lt, LANES), F32),
                   jax.ShapeDtypeStruct((N_EXPERTS, t_rows), F32)),
        grid=(t_rows // tm,),
        in_specs=[rows(d), lat(DN_WIDTH), ctx(DN_WIDTH), lat(FT_WIDTH), ctx(FT_WIDTH),
                  const(DN_WIDTH, d), const(FT_WIDTH, d),
                  pl.BlockSpec((1,) + mod_i.shape[1:], lambda t: (layer, 0, 0)),
                  const(1, d), const(N_EXPERTS, d)],
        out_specs=(rows(d), pl.BlockSpec((tm * n_lt, LANES), lambda t: (t, 0)),
                   pl.BlockSpec((N_EXPERTS, tm), lambda t: (0, t))),
        compiler_params=_params(("parallel",)),
        name="out_proj",
    )(xs, dn_l, dn_c, ft_l, ft_c, wdn, wft, mod_i, norm_w, router_wt)


def _route_kernel(tr, lg_ref, bias_ref, idx_ref, wt_ref, cnt_ref, upper, carry):
    step = pl.program_id(0)

    @pl.when(step == 0)
    def _():
        a = lax.broadcasted_iota(jnp.int32, (tr, tr), 0)
        b = lax.broadcasted_iota(jnp.int32, (tr, tr), 1)
        upper[...] = jnp.where(a < b, 1.0, 0.0).astype(BF16)
        carry[...] = jnp.zeros_like(carry)

    scores = _sigmoid(lg_ref[...])
    biased = scores + bias_ref[...]
    rows = [biased[r:r + 1, :] for r in range(N_EXPERTS)]
    srow = [scores[r:r + 1, :] for r in range(N_EXPERTS)]
    epg = EXPERTS_PER_GROUP

    def group_score(g):
        best = None
        for i in range(epg):
            for j in range(i + 1, epg):
                pair = rows[g * epg + i] + rows[g * epg + j]
                best = pair if best is None else jnp.maximum(best, pair)
        return best

    best_g = jnp.zeros((1, tr), jnp.int32)
    best_v = group_score(0)
    for g in range(1, N_EXPERT_GROUPS):
        gs = group_score(g)
        take = gs > best_v
        best_g = jnp.where(take, g, best_g)
        best_v = jnp.where(take, gs, best_v)

    def pick(table, r):
        out = table[r]
        for g in range(1, N_EXPERT_GROUPS):
            out = jnp.where(best_g == g, table[g * epg + r], out)
        return out

    in_b = [pick(rows, r) for r in range(epg)]
    in_s = [pick(srow, r) for r in range(epg)]
    l1 = jnp.zeros((1, tr), jnp.int32)
    m1 = in_b[0]
    for r in range(1, epg):
        take = in_b[r] > m1
        l1 = jnp.where(take, r, l1)
        m1 = jnp.where(take, in_b[r], m1)
    l2 = jnp.full((1, tr), -1, jnp.int32)
    m2 = jnp.full((1, tr), -jnp.inf, F32)
    for r in range(epg):
        take = jnp.logical_and(l1 != r, jnp.logical_or(l2 < 0, in_b[r] > m2))
        l2 = jnp.where(take, r, l2)
        m2 = jnp.where(take, in_b[r], m2)
    s1 = in_s[0]
    s2 = in_s[0]
    for r in range(1, epg):
        s1 = jnp.where(l1 == r, in_s[r], s1)
        s2 = jnp.where(l2 == r, in_s[r], s2)
    e1 = best_g * epg + l1
    e2 = best_g * epg + l2
    tot = s1 + s2
    wt_ref[0:1, :] = s1 / tot
    wt_ref[1:2, :] = s2 / tot

    eid = lax.broadcasted_iota(jnp.int32, (N_EXPERTS, tr), 0)
    is1 = eid == e1
    is2 = eid == e2
    memb = jnp.where(jnp.logical_or(is1, is2), 1.0, 0.0)
    prefix = jnp.dot(memb.astype(BF16), upper[...], preferred_element_type=F32) + carry[...]
    rank1 = jnp.sum(jnp.where(is1, prefix, 0.0), axis=0, keepdims=True)
    rank2 = jnp.sum(jnp.where(is2, prefix, 0.0), axis=0, keepdims=True)
    idx_ref[0:1, :] = e1
    idx_ref[1:2, :] = e2
    idx_ref[2:3, :] = rank1.astype(jnp.int32)
    idx_ref[3:4, :] = rank2.astype(jnp.int32)
    new_carry = carry[...] + jnp.sum(memb, axis=1, keepdims=True)
    carry[...] = new_carry
    cnt_ref[...] = new_carry.astype(jnp.int32)


def _route(logits_t, router_bias):
    n_e, t_rows = logits_t.shape
    tr = 512
    return pl.pallas_call(
        functools.partial(_route_kernel, tr),
        out_shape=(jax.ShapeDtypeStruct((4, t_rows), jnp.int32),
                   jax.ShapeDtypeStruct((2, t_rows), F32),
                   jax.ShapeDtypeStruct((n_e, 1), jnp.int32)),
        grid=(t_rows // tr,),
        in_specs=[pl.BlockSpec((n_e, tr), lambda t: (0, t)),
                  pl.BlockSpec((n_e, 1), lambda t: (0, 0))],
        out_specs=(pl.BlockSpec((4, tr), lambda t: (0, t)),
                   pl.BlockSpec((2, tr), lambda t: (0, t)),
                   pl.BlockSpec((n_e, 1), lambda t: (0, 0))),
        scratch_shapes=[pltpu.VMEM((tr, tr), BF16), pltpu.VMEM((n_e, 1), F32)],
        compiler_params=_params(("arbitrary",)),
        name="route",
    )(logits_t, router_bias)


def _row_copy(src, src_row, dst, dst_row, sem):
    s0 = pl.multiple_of(src_row * SUBLANES, SUBLANES)
    d0 = pl.multiple_of(dst_row * SUBLANES, SUBLANES)
    return pltpu.make_async_copy(src.at[pl.ds(s0, SUBLANES), :], dst.at[pl.ds(d0, SUBLANES), :], sem)


def _dispatch_kernel(tm, t_rows, dest_ref, lo_ref, hi_ref, h_ref, xs_hbm, sem, pad_sem):
    i = pl.program_id(0)
    base = i * tm

    @pl.when(i == 0)
    def _():
        def per_expert(e, carry):
            lo, hi = lo_ref[e], hi_ref[e]

            def start(s, c):
                _row_copy(h_ref, 0, xs_hbm, s, pad_sem).start()
                return c

            def wait(s, c):
                _row_copy(h_ref, 0, xs_hbm, lo, pad_sem).wait()
                return c

            lax.fori_loop(lo, hi, start, 0)
            lax.fori_loop(lo, hi, wait, 0)
            return carry

        lax.fori_loop(0, N_EXPERTS, per_expert, 0)

    def body(r, carry):
        _row_copy(h_ref, r, xs_hbm, dest_ref[base + r], sem).start()
        _row_copy(h_ref, r, xs_hbm, dest_ref[t_rows + base + r], sem).start()
        return carry

    lax.fori_loop(0, tm, body, 0, unroll=8)
    whole = pltpu.make_async_copy(h_ref, xs_hbm.at[pl.ds(0, tm * SUBLANES), :], sem)
    whole.wait()
    whole.wait()


def _dispatch(dest, pad_lo, pad_hi, h2, t_rows, p_rows):
    tm = 512
    grid_spec = pltpu.PrefetchScalarGridSpec(
        num_scalar_prefetch=3,
        grid=(t_rows // tm,),
        in_specs=[pl.BlockSpec((tm * SUBLANES, LANES), lambda i, *_: (i, 0))],
        out_specs=pl.BlockSpec(memory_space=pl.ANY),
        scratch_shapes=[pltpu.SemaphoreType.DMA, pltpu.SemaphoreType.DMA],
    )
    return pl.pallas_call(
        functools.partial(_dispatch_kernel, tm, t_rows),
        out_shape=jax.ShapeDtypeStruct((p_rows * SUBLANES, LANES), F32),
        grid_spec=grid_spec,
        compiler_params=_params(("arbitrary",)),
        name="moe_dispatch",
    )(dest, pad_lo, pad_hi, h2)


def _combine_kernel(tm, t_rows, n_lat_tiles_per_b, n_b, final, dest_ref, x1_ref, wt_ref, mod_ref, fw_ref,
                    ys_hbm, o_ref, buf, sem):
    i = pl.program_id(0)
    n_steps = pl.num_programs(0)
    d = x1_ref.shape[1]
    n_lt = d // LANES

    def issue(step, slot):
        base = step * tm

        def body(r, carry):
            _row_copy(ys_hbm, dest_ref[base + r], buf.at[slot, 0], r, sem.at[slot]).start()
            _row_copy(ys_hbm, dest_ref[t_rows + base + r], buf.at[slot, 1], r, sem.at[slot]).start()
            return carry

        lax.fori_loop(0, tm, body, 0, unroll=8)

    @pl.when(i == 0)
    def _():
        issue(0, 0)

    @pl.when(i + 1 < n_steps)
    def _():
        issue(i + 1, (i + 1) % 2)

    slot = i % 2
    for k in range(2):
        pltpu.make_async_copy(ys_hbm.at[pl.ds(0, tm * n_lt), :], buf.at[slot, k], sem.at[slot]).wait()
    bi = jnp.minimum(i // n_lat_tiles_per_b, n_b)
    g2 = mod_ref[0, pl.ds(bi, 1), 5 * d:6 * d]
    wt = wt_ref[...]
    y = (_load_token_tiles(buf.at[slot, 0], tm, n_lt) * wt[:, 0:1]
         + _load_token_tiles(buf.at[slot, 1], tm, n_lt) * wt[:, 1:2])
    out = x1_ref[...] + g2 * y
    if final:
        out = out * lax.rsqrt(jnp.mean(out * out, axis=-1, keepdims=True) + EPS) * fw_ref[...]
    o_ref[...] = out


def _combine(dest, x1, wts_t, mod, layer, final_w, ys, n_b, seq, final):
    t_rows, d = x1.shape
    tm = 256
    n_lt = d // LANES
    grid_spec = pltpu.PrefetchScalarGridSpec(
        num_scalar_prefetch=1,
        grid=(t_rows // tm,),
        in_specs=[pl.BlockSpec((tm, d), lambda i, *_: (i, 0)),
                  pl.BlockSpec((tm, 2), lambda i, *_: (i, 0)),
                  pl.BlockSpec((1,) + mod.shape[1:], lambda i, *_: (layer, 0, 0)),
                  pl.BlockSpec((1, d), lambda i, *_: (0, 0)),
                  pl.BlockSpec(memory_space=pl.ANY)],
        out_specs=pl.BlockSpec((tm, d), lambda i, *_: (i, 0)),
        scratch_shapes=[pltpu.VMEM((2, 2, tm * n_lt, LANES), F32), pltpu.SemaphoreType.DMA((2,))],
    )
    return pl.pallas_call(
        functools.partial(_combine_kernel, tm, t_rows, seq // tm, n_b, final),
        out_shape=jax.ShapeDtypeStruct((t_rows, d), F32),
        grid_spec=grid_spec,
        compiler_params=_params(("arbitrary",)),
        name="moe_combine",
    )(dest, x1, wts_t, mod, final_w, ys)


def _ffn_kernel(n_lt, be_ref, x_ref, wg_ref, wu_ref, wd_ref, o_ref):
    x = _load_token_tiles(x_ref, MOE_BLOCK, n_lt).astype(BF16)
    gate = jnp.dot(x, wg_ref[0], preferred_element_type=F32)
    up = jnp.dot(x, wu_ref[0], preferred_element_type=F32)
    hid = (_silu(gate) * up).astype(BF16)
    _store_token_tiles(o_ref, jnp.dot(hid, wd_ref[0], preferred_element_type=F32))


def _expert_ffn(blk_expert, xsorted, w_gate, w_up, w_down):
    d, de = w_gate.shape[-2:]
    n_lt = d // LANES
    n_blk = xsorted.shape[0] // (MOE_BLOCK * n_lt)
    grid_spec = pltpu.PrefetchScalarGridSpec(
        num_scalar_prefetch=1,
        grid=(n_blk,),
        in_specs=[pl.BlockSpec((MOE_BLOCK * n_lt, LANES), lambda i, be: (i, 0)),
                  pl.BlockSpec((1, d, de), lambda i, be: (be[i], 0, 0)),
                  pl.BlockSpec((1, d, de), lambda i, be: (be[i], 0, 0)),
                  pl.BlockSpec((1, de, d), lambda i, be: (be[i], 0, 0))],
        out_specs=pl.BlockSpec((MOE_BLOCK * n_lt, LANES), lambda i, be: (i, 0)),
    )
    return pl.pallas_call(
        functools.partial(_ffn_kernel, n_lt),
        out_shape=jax.ShapeDtypeStruct(xsorted.shape, F32),
        grid_spec=grid_spec,
        compiler_params=_params(("arbitrary",)),
        name="expert_ffn",
    )(blk_expert, xsorted, w_gate, w_up, w_down)


def _sincos_2d(length, dim):
    rows = length // GRID_W
    quarter = dim // 4
    omega = 1.0 / (POS_BASE ** (jnp.arange(quarter, dtype=F32) / quarter))
    ang_r = jnp.arange(rows, dtype=F32)[:, None] * omega
    ang_c = jnp.arange(GRID_W, dtype=F32)[:, None] * omega
    emb_r = jnp.concatenate([jnp.sin(ang_r), jnp.cos(ang_r)], axis=-1)
    emb_c = jnp.concatenate([jnp.sin(ang_c), jnp.cos(ang_c)], axis=-1)
    half = dim // 2
    emb = jnp.concatenate([jnp.broadcast_to(emb_r[:, None, :], (rows, GRID_W, half)),
                           jnp.broadcast_to(emb_c[None, :, :], (rows, GRID_W, half))], axis=-1)
    return emb.reshape(rows * GRID_W, dim)


def _moe(h2, x1, idx, wts, counts, mod, layer, final_w, w_gate, w_up, w_down, n_b, seq, final):
    t_rows, d = x1.shape
    a = 2 * t_rows
    counts = counts[:, 0]
    padded = (counts + MOE_BLOCK - 1) // MOE_BLOCK * MOE_BLOCK
    pad_end = jnp.cumsum(padded)
    pad_start = pad_end - padded
    n_blk = (a + N_EXPERTS * (MOE_BLOCK - 1) + MOE_BLOCK - 1) // MOE_BLOCK
    p_rows = n_blk * MOE_BLOCK
    blk_start = jnp.arange(n_blk, dtype=jnp.int32) * MOE_BLOCK
    blk_expert = jnp.minimum(jnp.sum(blk_start[:, None] >= pad_end[None, :], axis=1),
                             N_EXPERTS - 1).astype(jnp.int32)
    e_ids = jnp.arange(N_EXPERTS, dtype=jnp.int32)
    seg_start = jnp.sum(jnp.where(idx[0:2, :, None] == e_ids, pad_start, 0), axis=-1)
    dest = (seg_start + idx[2:4]).reshape(-1).astype(jnp.int32)
    pad_lo = (pad_start + counts).astype(jnp.int32)
    pad_hi = pad_end.at[N_EXPERTS - 1].set(p_rows).astype(jnp.int32)
    xsorted = _dispatch(dest, pad_lo, pad_hi, h2, t_rows, p_rows)
    ys = _expert_ffn(blk_expert, xsorted, w_gate, w_up, w_down)
    return _combine(dest, x1, wts.T, mod, layer, final_w, ys, n_b, seq, final)


def kernel(x, c, ctx, c_ctx, ada_w, ada_b, norm_mix_w, norm_ffn_w, w_in, conv_w, a_log, dt_bias,
           dn_norm_w, ft_norm_w, w_out, router_w, router_bias, w_gate, w_up, w_down, final_norm_w):
    n_b, seq, d = x.shape
    ctx_len = ctx.shape[1]
    depth = ada_w.shape[0]
    n_lat = n_b * seq
    n_tok = n_lat + n_b * ctx_len
    ft_off = 4 * DN_WIDTH + N_GATE_COLS

    xs = jnp.concatenate([(x + _sincos_2d(seq, d)[None]).reshape(n_lat, d),
                          ctx.reshape(n_b * ctx_len, d)], axis=0)

    mod_rows = -(-(n_b + 1) // 8) * 8
    cvec = jnp.zeros((mod_rows, d), F32).at[:n_b].set(c).at[n_b].set(c_ctx)
    mod = _ada(cvec, ada_w, ada_b)

    cos_l, sin_l = _dft_tables(seq)
    cos_c, sin_c = _dft_tables(ctx_len)
    cc, sc = _dft_tables(FT_GROUP_DIM)
    dft_cs = jnp.concatenate([cc, sc], axis=1)
    assert d == SUBLANES * LANES, "token tiles assume one (8, 128) tile per token"
    final_w = final_norm_w.reshape(1, d)
    router_wt = router_w.T
    rbias = router_bias.reshape(N_EXPERTS, 1)
    wg_b, wu_b, wd_b = w_gate.astype(BF16), w_up.astype(BF16), w_down.astype(BF16)

    col = jnp.arange(N_GATE_COLS)
    col_head, col_dir, col_ab = col // 4, (col // 2) % 2, col % 2
    gate_src = col_dir * (2 * N_DN_HEADS) + col_ab * N_DN_HEADS + col_head
    is_a = (col_ab == 0).astype(F32)
    is_bwd = (col_dir == 1).astype(F32)

    for i in range(depth):
        last = i == depth - 1
        w = w_in[i]
        wab = w[:, 4 * DN_WIDTH:ft_off][:, gate_src].astype(BF16)
        neg_a = -jnp.exp(a_log[i])[col_dir, col_head] * is_a
        dtb = dt_bias[i][col_dir, col_head] * is_a
        gpar = jnp.stack([neg_a, dtb, is_a, is_bwd])
        gpar_pad = jnp.pad(gpar, ((0, 0), (0, GATE_PAD - N_GATE_COLS)))
        wab_pad = jnp.pad(wab, ((0, 0), (0, GATE_PAD - N_GATE_COLS)))
        qkv, z, xc, xsn, gcol, grow = _in_proj(
            xs, mod, i, norm_mix_w[i].reshape(1, d),
            w[:, :3 * DN_WIDTH].astype(BF16), w[:, 3 * DN_WIDTH:4 * DN_WIDTH].astype(BF16),
            w[:, ft_off:].astype(BF16), wab_pad, wab.T, gpar_pad, gpar.T, dft_cs, n_b, seq)
        dn_l, dn_c = _deltanet(qkv, z, gcol, grow, conv_w[i], dn_norm_w[i].reshape(1, DN_HEAD_DIM),
                               n_b, seq, ctx_len)
        fnw = ft_norm_w[i].reshape(1, FT_WIDTH)
        ft_l = _fourier(xc, xsn, cos_l, sin_l, fnw, n_b, seq, 0)
        if last:
            ft_c, rows = dn_c, n_lat
        else:
            ft_c = _fourier(xc, xsn, cos_c, sin_c, fnw, n_b, ctx_len, n_lat // ctx_len)
            rows = n_tok
        wo = w_out[i].astype(BF16)
        x1, h2, logits_t = _out_proj(xs, dn_l, dn_c, ft_l, ft_c, wo[:DN_WIDTH], wo[DN_WIDTH:], mod, i,
                                     norm_ffn_w[i].reshape(1, d), router_wt, n_b, seq, rows)
        idx, wts, counts = _route(logits_t, rbias)
        xs = _moe(h2, x1, idx, wts, counts, mod, i, final_w, wg_b[i], wu_b[i], wd_b[i], n_b, seq, last)
    return xs.reshape(n_b, seq, d)
```

```python
import functools
import math

import jax
import jax.numpy as jnp
from jax import lax
from jax.experimental import pallas as pl
from jax.experimental.pallas import tpu as pltpu

F32 = jnp.float32
BF16 = jnp.bfloat16

GRID_W = 64
N_DN_HEADS = 4
DN_HEAD_DIM = 128
DN_WIDTH = N_DN_HEADS * DN_HEAD_DIM
N_FT_GROUPS = 4
FT_GROUP_DIM = 128
FT_WIDTH = N_FT_GROUPS * FT_GROUP_DIM
CONV_K = 5
CHUNK = 64
N_EXPERTS = 16
N_EXPERT_GROUPS = 4
EXPERTS_PER_GROUP = N_EXPERTS // N_EXPERT_GROUPS
D_EXPERT = 512
MOE_BLOCK = 256
POS_BASE = 10000.0
EPS = 1e-6
N_GATE_COLS = 4 * N_DN_HEADS
GATE_PAD = 128
CONV_BLOCK = 256
DN_UNROLL = 16

LANES = 128
SUBLANES = 8
VMEM_LIMIT = 56 * 1024 * 1024


def _params(sem, vmem=VMEM_LIMIT):
    return pltpu.CompilerParams(dimension_semantics=sem, vmem_limit_bytes=vmem)


def _dot(a, b):
    return jnp.dot(a.astype(BF16), b.astype(BF16), preferred_element_type=F32)


def _dot_nt(a, b):
    return lax.dot_general(a.astype(BF16), b.astype(BF16), (((1,), (1,)), ((), ())),
                           preferred_element_type=F32)


def _sigmoid(x):
    return 1.0 / (1.0 + jnp.exp(-x))


def _silu(x):
    return x * _sigmoid(x)


def _softplus(x):
    return jnp.maximum(x, 0.0) + jnp.log(1.0 + jnp.exp(-jnp.abs(x)))


def _ada_kernel(c_ref, w_ref, b_ref, o_ref):
    act = _silu(c_ref[...])
    o_ref[0] = jnp.dot(act, w_ref[0], preferred_element_type=F32,
                       precision=lax.Precision.HIGHEST) + b_ref[0]


def _ada(cvec, ada_w, ada_b):
    depth, d, n = ada_w.shape
    rows = cvec.shape[0]
    tn = 1024
    return pl.pallas_call(
        _ada_kernel,
        out_shape=jax.ShapeDtypeStruct((depth, rows, n), F32),
        grid=(depth, n // tn),
        in_specs=[pl.BlockSpec((rows, d), lambda i, j: (0, 0)),
                  pl.BlockSpec((1, d, tn), lambda i, j: (i, 0, j)),
                  pl.BlockSpec((1, 1, tn), lambda i, j: (i, 0, j))],
        out_specs=pl.BlockSpec((1, rows, tn), lambda i, j: (i, 0, j)),
        compiler_params=_params(("parallel", "parallel")),
        name="ada_mod",
    )(cvec, ada_w, ada_b.reshape(depth, 1, n))


def _seg_scan(x, pos, axis, reverse):
    n = x.shape[axis]
    s = 1
    while s < CHUNK:
        if reverse:
            shifted = pltpu.roll(x, n - s, axis)
            x = x + jnp.where(pos < CHUNK - s, shifted, 0.0)
        else:
            shifted = pltpu.roll(x, s, axis)
            x = x + jnp.where(pos >= s, shifted, 0.0)
        s *= 2
    return x


def _in_proj_kernel(tm, n_lat_tiles_per_b, n_b, x_ref, mod_ref, nw_ref, wqkv_ref, wz_ref, wft_ref,
                    wab_ref, wabt_ref, gpar_ref, gpart_ref, dft_ref,
                    qkv_ref, z_ref, xc_ref, xs_ref, gcol_ref, grow_ref):
    t = pl.program_id(0)
    bi = jnp.minimum(t // n_lat_tiles_per_b, n_b)
    d = x_ref.shape[1]
    x = x_ref[...]
    y = x * lax.rsqrt(jnp.mean(x * x, axis=-1, keepdims=True) + EPS) * nw_ref[...]
    shift = mod_ref[0, pl.ds(bi, 1), 0:d]
    scale = mod_ref[0, pl.ds(bi, 1), d:2 * d]
    h = (y * (1.0 + scale) + shift).astype(BF16)

    for j in range(3):
        cs = slice(j * DN_WIDTH, (j + 1) * DN_WIDTH)
        qkv_ref[:, cs] = jnp.dot(h, wqkv_ref[:, cs], preferred_element_type=F32)
    z_ref[...] = jnp.dot(h, wz_ref[...], preferred_element_type=F32)

    ft = jnp.dot(h, wft_ref[...], preferred_element_type=F32).astype(BF16)
    for g in range(N_FT_GROUPS):
        cs = slice(g * FT_GROUP_DIM, (g + 1) * FT_GROUP_DIM)
        cssn = jnp.dot(ft[:, cs], dft_ref[...], preferred_element_type=F32)
        xc_ref[:, cs] = cssn[:, :FT_GROUP_DIM].astype(BF16)
        xs_ref[:, cs] = cssn[:, FT_GROUP_DIM:].astype(BF16)

    ab = jnp.dot(h, wab_ref[...], preferred_element_type=F32)
    abt = lax.dot_general(wabt_ref[...], h, (((1,), (1,)), ((), ())), preferred_element_type=F32)

    def gates(v, par, axis):
        neg_a, dtb, is_a, is_bwd = par
        g = neg_a * _softplus(v + dtb)
        pos = lax.broadcasted_iota(jnp.int32, v.shape, axis) % CHUNK
        fwd = _seg_scan(g, pos, axis, reverse=False)
        bwd = _seg_scan(g, pos, axis, reverse=True)
        cum = jnp.where(is_bwd > 0.5, bwd, fwd)
        return jnp.where(is_a > 0.5, cum, _sigmoid(v))

    gp = gpar_ref[...]
    gc = gates(ab, (gp[0:1], gp[1:2], gp[2:3], gp[3:4]), 0)
    gpt = gpart_ref[...]
    gt = gates(abt, (gpt[:, 0:1], gpt[:, 1:2], gpt[:, 2:3], gpt[:, 3:4]), 1)
    for hh in range(N_DN_HEADS):
        gcol_ref[hh] = gc[:, 4 * hh:4 * hh + 4]
        for j in range(tm // CHUNK):
            cs = slice(j * CHUNK, (j + 1) * CHUNK)
            grow_ref[hh, j] = jnp.concatenate([gt[4 * hh:4 * hh + 1, cs], gt[4 * hh + 2:4 * hh + 3, cs]], axis=1)


def _in_proj(xs, mod_i, layer, norm_w, wqkv, wz, wft, wab, wabt, gpar, gpart, dft_cs, n_b, seq):
    t_rows, d = xs.shape
    tm = 512
    n_tiles = t_rows // tm
    kern = functools.partial(_in_proj_kernel, tm, seq // tm, n_b)
    const = lambda *shape: pl.BlockSpec(shape, lambda t: tuple(0 for _ in shape))
    rows = lambda w: pl.BlockSpec((tm, w), lambda t: (t, 0))
    return pl.pallas_call(
        kern,
        out_shape=(jax.ShapeDtypeStruct((t_rows, 3 * DN_WIDTH), F32),
                   jax.ShapeDtypeStruct((t_rows, DN_WIDTH), F32),
                   jax.ShapeDtypeStruct((t_rows, FT_WIDTH), BF16),
                   jax.ShapeDtypeStruct((t_rows, FT_WIDTH), BF16),
                   jax.ShapeDtypeStruct((N_DN_HEADS, t_rows, 4), F32),
                   jax.ShapeDtypeStruct((N_DN_HEADS, t_rows // CHUNK, 1, 2 * CHUNK), F32)),
        grid=(n_tiles,),
        in_specs=[rows(d),
                  pl.BlockSpec((1,) + mod_i.shape[1:], lambda t: (layer, 0, 0)),
                  const(1, d),
                  const(d, 3 * DN_WIDTH), const(d, DN_WIDTH), const(d, FT_WIDTH),
                  const(d, GATE_PAD), const(N_GATE_COLS, d),
                  const(4, GATE_PAD), const(N_GATE_COLS, 4),
                  const(FT_GROUP_DIM, 2 * FT_GROUP_DIM)],
        out_specs=(rows(3 * DN_WIDTH), rows(DN_WIDTH), rows(FT_WIDTH), rows(FT_WIDTH),
                   pl.BlockSpec((N_DN_HEADS, tm, 4), lambda t: (0, t, 0)),
                   pl.BlockSpec((N_DN_HEADS, tm // CHUNK, 1, 2 * CHUNK), lambda t: (0, t, 0, 0))),
        compiler_params=_params(("parallel",)),
        name="in_proj",
    )(xs, mod_i, norm_w, wqkv, wz, wft, wab, wabt, gpar, gpart, dft_cs)


def _conv_prep(src_ref, cw, n_rows, dst_ref, mode):
    blk = CONV_BLOCK
    n_blk = n_rows // blk
    halo = 8
    n_win = blk + 2 * halo

    def body(c, carry):
        r0 = pl.multiple_of(c * blk, blk)
        main = src_ref[pl.ds(r0, blk), :]
        prev = src_ref[pl.ds(pl.multiple_of(jnp.maximum(r0 - halo, 0), halo), halo), :]
        nxt = src_ref[pl.ds(pl.multiple_of(jnp.minimum(r0 + blk, n_rows - halo), halo), halo), :]
        prev = jnp.where(c > 0, prev, 0.0)
        nxt = jnp.where(c < n_blk - 1, nxt, 0.0)
        win = jnp.concatenate([prev, main, nxt], axis=0)
        acc = jnp.zeros((blk, DN_HEAD_DIM), F32)
        for j in range(CONV_K):
            shift = (CONV_K // 2 - j) % n_win
            rolled = win if shift == 0 else pltpu.roll(win, shift, 0)
            acc = acc + rolled[halo:halo + blk] * cw[j:j + 1, :]
        y = _silu(acc)
        if mode != "v":
            y = y * lax.rsqrt(jnp.sum(y * y, axis=-1, keepdims=True) + EPS)
        if mode == "q":
            y = y * (DN_HEAD_DIM ** -0.5)
        dst_ref[pl.ds(r0, blk), :] = y
        return carry

    lax.fori_loop(0, n_blk, body, 0)


def _blockdiag(x, isb):
    return jnp.concatenate([jnp.where(isb, 0.0, x), jnp.where(isb, x, 0.0)], axis=0).astype(BF16)


def _blockdiag_wide(x):
    w = x.shape[1] // 2
    zero = jnp.zeros((x.shape[0], w), BF16)
    xb = x.astype(BF16)
    return jnp.concatenate([jnp.concatenate([xb[:, :w], zero], axis=1),
                            jnp.concatenate([zero, xb[:, w:]], axis=1)], axis=0)


def _tri_inverse_dual(a_list, eye, xor, isb):
    mm = lambda x, y: jnp.dot(x.astype(BF16), _blockdiag(y, isb), preferred_element_type=F32)
    ad = [jnp.where((xor >> 3) == 0, a, 0.0) for a in a_list]
    a2 = [mm(x, x) for x in ad]
    a4 = [mm(x, x) for x in a2]
    t = [eye - x for x in ad]
    t = [x + mm(x, y) for x, y in zip(t, a2)]
    t = [x + mm(x, y) for x, y in zip(t, a4)]
    for s in (3, 4, 5):
        off = [jnp.where((xor >> s) == 1, a, 0.0) for a in a_list]
        to = [mm(x, y) for x, y in zip(t, off)]
        t = [x - mm(y, x) for x, y in zip(t, to)]
    return t


def _chunks_local(loaded):
    hd = DN_HEAD_DIM
    n = len(loaded)
    q = [x[0] for x in loaded]
    k = [x[1] for x in loaded]
    v = [x[2] for x in loaded]
    wide = lambda col: jnp.broadcast_to(col, (CHUNK, hd))
    gf = [wide(x[3][:, 0:1]) for x in loaded]
    bf_ = [wide(x[3][:, 1:2]) for x in loaded]
    gb = [wide(x[3][:, 2:3]) for x in loaded]
    bb = [wide(x[3][:, 3:4]) for x in loaded]
    grow = [x[4] for x in loaded]
    row = lax.broadcasted_iota(jnp.int32, (CHUNK, 2 * CHUNK), 0)
    lane = lax.broadcasted_iota(jnp.int32, (CHUNK, 2 * CHUNK), 1)
    jl = lane & (CHUNK - 1)
    isb = lane >= CHUNK
    delta = jnp.where(isb, jl - row, row - jl)
    xor = row ^ jl
    eye = jnp.where(delta == 0, 1.0, 0.0)
    nt = (((1,), (1,)), ((), ()))
    k2 = [jnp.concatenate([x, x], axis=0).astype(BF16) for x in k]
    kkd = [lax.dot_general(k[i].astype(BF16), k2[i], nt, preferred_element_type=F32) for i in range(n)]
    qkd = [lax.dot_general(q[i].astype(BF16), k2[i], nt, preferred_element_type=F32) for i in range(n)]
    dec = [jnp.where(delta >= 0, jnp.exp(jnp.where(isb, gb[i], gf[i]) - grow[i]), 0.0) for i in range(n)]
    a_mat = [jnp.where(delta > 0, kkd[i] * jnp.where(isb, bb[i], bf_[i]) * dec[i], 0.0) for i in range(n)]
    t_inv = _tri_inverse_dual(a_mat, eye, xor, isb)
    egf = [jnp.exp(x) for x in gf]
    egb = [jnp.exp(x) for x in gb]
    rhs = [jnp.concatenate([v[i] * bf_[i], k[i] * (bf_[i] * egf[i]), v[i] * bb[i], k[i] * (bb[i] * egb[i])],
                           axis=1) for i in range(n)]
    sol = [jnp.dot(t_inv[i].astype(BF16), _blockdiag_wide(rhs[i]), preferred_element_type=F32)
           for i in range(n)]
    bd_sol = [_blockdiag_wide(x) for x in sol]
    r1 = [jnp.dot((qkd[i] * dec[i]).astype(BF16), bd_sol[i], preferred_element_type=F32)
          for i in range(n)]
    glf = [x[CHUNK - 1:CHUNK, :] for x in gf]
    glb = [x[0:1, :] for x in gb]
    kdec = [jnp.concatenate([k[i] * jnp.exp(glf[i] - gf[i]), k[i] * jnp.exp(glb[i] - gb[i])], axis=0)
            for i in range(n)]
    r2 = [jnp.dot(kdec[i].T.astype(BF16), bd_sol[i], preferred_element_type=F32)
          for i in range(n)]
    out = []
    for i in range(n):
        o_loc = r1[i][:, 0:hd] + r1[i][:, 2 * hd:3 * hd]
        qt = jnp.concatenate([q[i] * egf[i] - r1[i][:, hd:2 * hd], q[i] * egb[i] - r1[i][:, 3 * hd:]],
                             axis=1).astype(BF16)
        nn = jnp.concatenate([r2[i][:, 0:hd], r2[i][:, 2 * hd:3 * hd]], axis=1)
        kw = jnp.concatenate([r2[i][:, hd:2 * hd], r2[i][:, 3 * hd:]], axis=1).astype(BF16)
        ge = jnp.concatenate([jnp.exp(glf[i]), jnp.exp(glb[i])], axis=0)
        out.append((o_loc, qt, nn, kw, ge))
    return out


def _state_step(c, d, oacc, qt_ref, kw_ref, nn_ref, ge_ref, s_ref):
    hd = DN_HEAD_DIM
    r0 = pl.multiple_of(c * CHUNK, CHUNK)
    cs = slice(d * hd, (d + 1) * hd)
    s = s_ref[d]
    lhs = jnp.concatenate([qt_ref[c, :, cs], kw_ref[c, :, cs]], axis=0)
    r = jnp.dot(lhs, s.astype(BF16), preferred_element_type=F32)
    oacc[pl.ds(r0, CHUNK), :] += r[:CHUNK]
    s_ref[d] = s * ge_ref[c, d:d + 1, :] + nn_ref[c, :, cs] - r[CHUNK:]


def _dn_kernel(seq, ctx_len,
               ql_ref, kl_ref, vl_ref, qc_ref, kc_ref, vc_ref, cwq_ref, cwk_ref, cwv_ref,
               zl_ref, zc_ref, gcl_ref, gcc_ref, grl_ref, grc_ref, nw_ref,
               ol_ref, oc_ref,
               qn, kn, vn, oacc, qt_s, kw_s, nn_s, ge_s, s_s):
    nw = nw_ref[...]
    s_s[...] = jnp.zeros_like(s_s)

    def segment(n_rows, q_ref, k_ref, v_ref, z_ref, gcol_ref, grow_ref, out_ref):
        _conv_prep(q_ref, cwq_ref[...], n_rows, qn, "q")
        _conv_prep(k_ref, cwk_ref[...], n_rows, kn, "k")
        _conv_prep(v_ref, cwv_ref[...], n_rows, vn, "v")
        n_chunks = n_rows // CHUNK

        unroll = math.gcd(DN_UNROLL, n_chunks)

        def local(i, carry):
            chunks = [i * unroll + j for j in range(unroll)]
            rows = [pl.ds(pl.multiple_of(c * CHUNK, CHUNK), CHUNK) for c in chunks]
            loaded = [(qn[r, :], kn[r, :], vn[r, :], gcol_ref[0, r, :], grow_ref[0, c])
                      for c, r in zip(chunks, rows)]
            results = _chunks_local(loaded)
            for c, r, (o_loc, qt, nn, kw, ge) in zip(chunks, rows, results):
                oacc[r, :] = o_loc
                qt_s[c] = qt
                nn_s[c] = nn
                kw_s[c] = kw
                ge_s[c] = ge
            return carry

        lax.fori_loop(0, n_chunks // unroll, local, 0)

        def step(s, carry):
            _state_step(s, 0, oacc, qt_s, kw_s, nn_s, ge_s, s_s)
            _state_step(n_chunks - 1 - s, 1, oacc, qt_s, kw_s, nn_s, ge_s, s_s)
            return carry

        lax.fori_loop(0, n_chunks, step, 0)

        blk = 256

        def fin(i, carry):
            r0 = pl.multiple_of(i * blk, blk)
            o = oacc[pl.ds(r0, blk), :]
            o = o * lax.rsqrt(jnp.mean(o * o, axis=-1, keepdims=True) + EPS) * nw
            out_ref[pl.ds(r0, blk), :] = (o * _silu(z_ref[pl.ds(r0, blk), :])).astype(out_ref.dtype)
            return carry

        lax.fori_loop(0, n_rows // blk, fin, 0)

    segment(ctx_len, qc_ref, kc_ref, vc_ref, zc_ref, gcc_ref, grc_ref, oc_ref)
    segment(seq, ql_ref, kl_ref, vl_ref, zl_ref, gcl_ref, grl_ref, ol_ref)


def _deltanet(qkv, z, gcol, grow, conv_w, dn_norm_w, n_b, seq, ctx_len):
    hd = DN_HEAD_DIM
    nc = seq // CHUNK
    cb = n_b * seq // ctx_len
    lat = lambda off: pl.BlockSpec((seq, hd), lambda b, h: (b, h + off))
    ctx = lambda off: pl.BlockSpec((ctx_len, hd), lambda b, h: (cb + b, h + off))
    cws = lambda off: pl.BlockSpec((CONV_K, hd), lambda b, h: (0, h + off))
    kern = functools.partial(_dn_kernel, seq, ctx_len)
    return pl.pallas_call(
        kern,
        out_shape=(jax.ShapeDtypeStruct((n_b * seq, DN_WIDTH), BF16),
                   jax.ShapeDtypeStruct((n_b * ctx_len, DN_WIDTH), BF16)),
        grid=(n_b, N_DN_HEADS),
        in_specs=[lat(0), lat(N_DN_HEADS), lat(2 * N_DN_HEADS),
                  ctx(0), ctx(N_DN_HEADS), ctx(2 * N_DN_HEADS),
                  cws(0), cws(N_DN_HEADS), cws(2 * N_DN_HEADS),
                  lat(0), ctx(0),
                  pl.BlockSpec((1, seq, 4), lambda b, h: (h, b, 0)),
                  pl.BlockSpec((1, ctx_len, 4), lambda b, h: (h, cb + b, 0)),
                  pl.BlockSpec((1, nc, 1, 2 * CHUNK), lambda b, h: (h, b, 0, 0)),
                  pl.BlockSpec((1, ctx_len // CHUNK, 1, 2 * CHUNK), lambda b, h: (h, cb + b, 0, 0)),
                  pl.BlockSpec((1, hd), lambda b, h: (0, 0))],
        out_specs=(pl.BlockSpec((seq, hd), lambda b, h: (b, h)),
                   pl.BlockSpec((ctx_len, hd), lambda b, h: (b, h))),
        scratch_shapes=[pltpu.VMEM((seq, hd), F32) for _ in range(4)]
        + [pltpu.VMEM((nc, CHUNK, 2 * hd), BF16), pltpu.VMEM((nc, hd, 2 * hd), BF16),
           pltpu.VMEM((nc, hd, 2 * hd), F32), pltpu.VMEM((nc, 2, hd), F32),
           pltpu.VMEM((2, hd, hd), F32)],
        compiler_params=_params(("parallel", "parallel")),
        name="deltanet",
    )(qkv, qkv, qkv, qkv, qkv, qkv, conv_w, conv_w, conv_w, z, z, gcol, gcol, grow, grow, dn_norm_w)


def _ft_kernel(tm, scale, cl_ref, sl_ref, xc_ref, xs_ref, xcr_ref, xsr_ref, nyq_ref, nw_ref,
               lo_ref, hi_ref, xcf, xsf):
    mt = pl.program_id(1)
    half = xcf.shape[0]

    @pl.when(mt == 0)
    def _():
        xcf[...] = xc_ref[...] + xcr_ref[...]
        xsf[...] = xs_ref[...] - xsr_ref[...]

    a = jnp.dot(cl_ref[...], xcf[...], preferred_element_type=F32)
    b = jnp.dot(sl_ref[...], xsf[...], preferred_element_type=F32)
    x_nyq = nyq_ref[0:1, :].astype(F32)
    m_idx = mt * tm + lax.broadcasted_iota(jnp.int32, (tm, 1), 0)
    corr = jnp.where((m_idx & 1) == 0, 1.0, -1.0) * x_nyq
    nw = nw_ref[...]

    def norm(y):
        y = y * scale
        return y * lax.rsqrt(jnp.mean(y * y, axis=-1, keepdims=True) + EPS) * nw

    lane = lax.broadcasted_iota(jnp.int32, (SUBLANES, half), 1)
    alt = jnp.where((lane & 1) == 0, 1.0, -1.0).astype(BF16)
    y_nyq = jnp.dot(alt, xcf[...], preferred_element_type=F32)[0:1, :] + x_nyq
    lo_ref[...] = norm(a - b + corr).astype(lo_ref.dtype)
    hi_ref[...] = norm(jnp.where(m_idx == 0, y_nyq, a + b + corr)).astype(hi_ref.dtype)


def _fourier(xc, xs, cos_h, sin_h, ft_norm_w, n_b, length, row_block0):
    half = length // 2
    tm = min(512, half)
    n_m = half // tm
    scale = 1.0 / math.sqrt(length * FT_GROUP_DIM)
    r0 = row_block0 * length

    def reversed_half(x):
        xb = x[r0:r0 + n_b * length].reshape(n_b, length, FT_WIDTH)
        rev = jnp.flip(xb[:, half + 1:], axis=1)
        return jnp.concatenate([jnp.zeros((n_b, 1, FT_WIDTH), x.dtype), rev], axis=1).reshape(n_b * half, FT_WIDTH)

    hb0 = 2 * row_block0
    lo, hi = pl.pallas_call(
        functools.partial(_ft_kernel, tm, scale),
        out_shape=(jax.ShapeDtypeStruct((n_b * half, FT_WIDTH), BF16),
                   jax.ShapeDtypeStruct((n_b * half, FT_WIDTH), BF16)),
        grid=(n_b, n_m),
        in_specs=[pl.BlockSpec((tm, half), lambda b, m: (m, 0)),
                  pl.BlockSpec((tm, half), lambda b, m: (m, 0)),
                  pl.BlockSpec((half, FT_WIDTH), lambda b, m: (hb0 + 2 * b, 0)),
                  pl.BlockSpec((half, FT_WIDTH), lambda b, m: (hb0 + 2 * b, 0)),
                  pl.BlockSpec((half, FT_WIDTH), lambda b, m: (b, 0)),
                  pl.BlockSpec((half, FT_WIDTH), lambda b, m: (b, 0)),
                  pl.BlockSpec((2 * SUBLANES, FT_WIDTH),
                               lambda b, m: ((hb0 + 2 * b + 1) * (half // (2 * SUBLANES)), 0)),
                  pl.BlockSpec((1, FT_WIDTH), lambda b, m: (0, 0))],
        out_specs=(pl.BlockSpec((tm, FT_WIDTH), lambda b, m: (b * n_m + m, 0)),
                   pl.BlockSpec((tm, FT_WIDTH), lambda b, m: (b * n_m + m, 0))),
        scratch_shapes=[pltpu.VMEM((half, FT_WIDTH), BF16), pltpu.VMEM((half, FT_WIDTH), BF16)],
        compiler_params=_params(("parallel", "arbitrary")),
        name="fourier_seq",
    )(cos_h, sin_h, xc, xs, reversed_half(xc), reversed_half(xs), xc, ft_norm_w)
    lo = lo.reshape(n_b, half, FT_WIDTH)
    hi = hi.reshape(n_b, half, FT_WIDTH)
    y = jnp.concatenate([lo, hi[:, 0:1], jnp.flip(hi[:, 1:], axis=1)], axis=1)
    return y.reshape(n_b * length, FT_WIDTH)


def _dft_tables(length, n=None):
    m = jnp.arange(length if n is None else n, dtype=jnp.int32)
    ph = (m[:, None] * m[None, :]) % length
    ang = ph.astype(F32) * (2.0 * math.pi / length)
    return jnp.cos(ang).astype(BF16), jnp.sin(ang).astype(BF16)


def _store_token_tiles(ref, val):
    rows, d = val.shape
    n_lt = d // LANES
    for j in range(n_lt):
        ref[pl.ds(j, rows, stride=n_lt), :] = val[:, j * LANES:(j + 1) * LANES]


def _load_token_tiles(ref, rows, n_lt):
    return jnp.concatenate([ref[pl.ds(j, rows, stride=n_lt), :] for j in range(n_lt)], axis=1)


def _out_proj_kernel(n_lat_tiles, n_lat_tiles_per_b, n_b, x_ref, dnl_ref, dnc_ref, ftl_ref, ftc_ref,
                     wdn_ref, wft_ref, mod_ref, nw_ref, rwt_ref, x1_ref, h2_ref, lg_ref):
    t = pl.program_id(0)
    bi = jnp.minimum(t // n_lat_tiles_per_b, n_b)
    d = x_ref.shape[1]
    is_lat = t < n_lat_tiles
    dn = jnp.where(is_lat, dnl_ref[...], dnc_ref[...])
    ft = jnp.where(is_lat, ftl_ref[...], ftc_ref[...])
    mix = (jnp.dot(dn, wdn_ref[...], preferred_element_type=F32)
           + jnp.dot(ft, wft_ref[...], preferred_element_type=F32))
    g1 = mod_ref[0, pl.ds(bi, 1), 2 * d:3 * d]
    sh2 = mod_ref[0, pl.ds(bi, 1), 3 * d:4 * d]
    sc2 = mod_ref[0, pl.ds(bi, 1), 4 * d:5 * d]
    x1 = x_ref[...] + g1 * mix
    x1_ref[...] = x1
    y = x1 * lax.rsqrt(jnp.mean(x1 * x1, axis=-1, keepdims=True) + EPS) * nw_ref[...]
    h2 = y * (1.0 + sc2) + sh2
    _store_token_tiles(h2_ref, h2)
    lg_ref[...] = lax.dot_general(rwt_ref[...], h2, (((1,), (1,)), ((), ())),
                                  preferred_element_type=F32, precision=lax.Precision.HIGHEST)


def _out_proj(xs, dn_l, dn_c, ft_l, ft_c, wdn, wft, mod_i, layer, norm_w, router_wt, n_b, seq, t_rows):
    d = xs.shape[1]
    tm = 512
    n_lt = d // LANES
    n_lat_tiles = n_b * seq // tm
    kern = functools.partial(_out_proj_kernel, n_lat_tiles, seq // tm, n_b)
    const = lambda *shape: pl.BlockSpec(shape, lambda t: tuple(0 for _ in shape))
    rows = lambda w: pl.BlockSpec((tm, w), lambda t: (t, 0))
    lat = lambda w: pl.BlockSpec((tm, w), lambda t: (jnp.minimum(t, n_lat_tiles - 1), 0))
    ctx = lambda w: pl.BlockSpec((tm, w), lambda t: (jnp.maximum(t - n_lat_tiles, 0), 0))
    return pl.pallas_call(
        kern,
        out_shape=(jax.ShapeDtypeStruct((t_rows, d), F32),
                   jax.ShapeDtypeStruct((t_rows * n_lt, LANES), F32),
                   jax.ShapeDtypeStruct((N_EXPERTS, t_rows), F32)),
        grid=(t_rows // tm,),
        in_specs=[rows(d), lat(DN_WIDTH), ctx(DN_WIDTH), lat(FT_WIDTH), ctx(FT_WIDTH),
                  const(DN_WIDTH, d), const(FT_WIDTH, d),
                  pl.BlockSpec((1,) + mod_i.shape[1:], lambda t: (layer, 0, 0)),
                  const(1, d), const(N_EXPERTS, d)],
        out_specs=(rows(d), pl.BlockSpec((tm * n_lt, LANES), lambda t: (t, 0)),
                   pl.BlockSpec((N_EXPERTS, tm), lambda t: (0, t))),
        compiler_params=_params(("parallel",)),
        name="out_proj",
    )(xs, dn_l, dn_c, ft_l, ft_c, wdn, wft, mod_i, norm_w, router_wt)


def _route_kernel(tr, lg_ref, bias_ref, idx_ref, wt_ref, cnt_ref, upper, carry):
    step = pl.program_id(0)

    @pl.when(step == 0)
    def _():
        a = lax.broadcasted_iota(jnp.int32, (tr, tr), 0)
        b = lax.broadcasted_iota(jnp.int32, (tr, tr), 1)
        upper[...] = jnp.where(a < b, 1.0, 0.0).astype(BF16)
        carry[...] = jnp.zeros_like(carry)

    scores = _sigmoid(lg_ref[...])
    biased = scores + bias_ref[...]
    rows = [biased[r:r + 1, :] for r in range(N_EXPERTS)]
    srow = [scores[r:r + 1, :] for r in range(N_EXPERTS)]
    epg = EXPERTS_PER_GROUP

    def group_score(g):
        best = None
        for i in range(epg):
            for j in range(i + 1, epg):
                pair = rows[g * epg + i] + rows[g * epg + j]
                best = pair if best is None else jnp.maximum(best, pair)
        return best

    best_g = jnp.zeros((1, tr), jnp.int32)
    best_v = group_score(0)
    for g in range(1, N_EXPERT_GROUPS):
        gs = group_score(g)
        take = gs > best_v
        best_g = jnp.where(take, g, best_g)
        best_v = jnp.where(take, gs, best_v)

    def pick(table, r):
        out = table[r]
        for g in range(1, N_EXPERT_GROUPS):
            out = jnp.where(best_g == g, table[g * epg + r], out)
        return out

    in_b = [pick(rows, r) for r in range(epg)]
    in_s = [pick(srow, r) for r in range(epg)]
    l1 = jnp.zeros((1, tr), jnp.int32)
    m1 = in_b[0]
    for r in range(1, epg):
        take = in_b[r] > m1
        l1 = jnp.where(take, r, l1)
        m1 = jnp.where(take, in_b[r], m1)
    l2 = jnp.full((1, tr), -1, jnp.int32)
    m2 = jnp.full((1, tr), -jnp.inf, F32)
    for r in range(epg):
        take = jnp.logical_and(l1 != r, jnp.logical_or(l2 < 0, in_b[r] > m2))
        l2 = jnp.where(take, r, l2)
        m2 = jnp.where(take, in_b[r], m2)
    s1 = in_s[0]
    s2 = in_s[0]
    for r in range(1, epg):
        s1 = jnp.where(l1 == r, in_s[r], s1)
        s2 = jnp.where(l2 == r, in_s[r], s2)
    e1 = best_g * epg + l1
    e2 = best_g * epg + l2
    tot = s1 + s2
    wt_ref[0:1, :] = s1 / tot
    wt_ref[1:2, :] = s2 / tot

    eid = lax.broadcasted_iota(jnp.int32, (N_EXPERTS, tr), 0)
    is1 = eid == e1
    is2 = eid == e2
    memb = jnp.where(jnp.logical_or(is1, is2), 1.0, 0.0)
    prefix = jnp.dot(memb.astype(BF16), upper[...], preferred_element_type=F32) + carry[...]
    rank1 = jnp.sum(jnp.where(is1, prefix, 0.0), axis=0, keepdims=True)
    rank2 = jnp.sum(jnp.where(is2, prefix, 0.0), axis=0, keepdims=True)
    idx_ref[0:1, :] = e1
    idx_ref[1:2, :] = e2
    idx_ref[2:3, :] = rank1.astype(jnp.int32)
    idx_ref[3:4, :] = rank2.astype(jnp.int32)
    new_carry = carry[...] + jnp.sum(memb, axis=1, keepdims=True)
    carry[...] = new_carry
    cnt_ref[...] = new_carry.astype(jnp.int32)


def _route(logits_t, router_bias):
    n_e, t_rows = logits_t.shape
    tr = 512
    return pl.pallas_call(
        functools.partial(_route_kernel, tr),
        out_shape=(jax.ShapeDtypeStruct((4, t_rows), jnp.int32),
                   jax.ShapeDtypeStruct((2, t_rows), F32),
                   jax.ShapeDtypeStruct((n_e, 1), jnp.int32)),
        grid=(t_rows // tr,),
        in_specs=[pl.BlockSpec((n_e, tr), lambda t: (0, t)),
                  pl.BlockSpec((n_e, 1), lambda t: (0, 0))],
        out_specs=(pl.BlockSpec((4, tr), lambda t: (0, t)),
                   pl.BlockSpec((2, tr), lambda t: (0, t)),
                   pl.BlockSpec((n_e, 1), lambda t: (0, 0))),
        scratch_shapes=[pltpu.VMEM((tr, tr), BF16), pltpu.VMEM((n_e, 1), F32)],
        compiler_params=_params(("arbitrary",)),
        name="route",
    )(logits_t, router_bias)


def _row_copy(src, src_row, dst, dst_row, sem):
    s0 = pl.multiple_of(src_row * SUBLANES, SUBLANES)
    d0 = pl.multiple_of(dst_row * SUBLANES, SUBLANES)
    return pltpu.make_async_copy(src.at[pl.ds(s0, SUBLANES), :], dst.at[pl.ds(d0, SUBLANES), :], sem)


def _dispatch_kernel(tm, t_rows, dest_ref, lo_ref, hi_ref, h_ref, xs_hbm, sem, pad_sem):
    i = pl.program_id(0)
    base = i * tm

    @pl.when(i == 0)
    def _():
        def per_expert(e, carry):
            lo, hi = lo_ref[e], hi_ref[e]

            def start(s, c):
                _row_copy(h_ref, 0, xs_hbm, s, pad_sem).start()
                return c

            def wait(s, c):
                _row_copy(h_ref, 0, xs_hbm, lo, pad_sem).wait()
                return c

            lax.fori_loop(lo, hi, start, 0)
            lax.fori_loop(lo, hi, wait, 0)
            return carry

        lax.fori_loop(0, N_EXPERTS, per_expert, 0)

    def body(r, carry):
        _row_copy(h_ref, r, xs_hbm, dest_ref[base + r], sem).start(priority=0)
        _row_copy(h_ref, r, xs_hbm, dest_ref[t_rows + base + r], sem).start(priority=1)
        return carry

    lax.fori_loop(0, tm, body, 0, unroll=8)
    whole = pltpu.make_async_copy(h_ref, xs_hbm.at[pl.ds(0, tm * SUBLANES), :], sem)
    whole.wait()
    whole.wait()


def _dispatch(dest, pad_lo, pad_hi, h2, t_rows, p_rows):
    tm = 512
    grid_spec = pltpu.PrefetchScalarGridSpec(
        num_scalar_prefetch=3,
        grid=(t_rows // tm,),
        in_specs=[pl.BlockSpec((tm * SUBLANES, LANES), lambda i, *_: (i, 0))],
        out_specs=pl.BlockSpec(memory_space=pl.ANY),
        scratch_shapes=[pltpu.SemaphoreType.DMA, pltpu.SemaphoreType.DMA],
    )
    return pl.pallas_call(
        functools.partial(_dispatch_kernel, tm, t_rows),
        out_shape=jax.ShapeDtypeStruct((p_rows * SUBLANES, LANES), F32),
        grid_spec=grid_spec,
        compiler_params=_params(("arbitrary",)),
        name="moe_dispatch",
    )(dest, pad_lo, pad_hi, h2)


def _combine_kernel(tm, t_rows, n_lat_tiles_per_b, n_b, final, dest_ref, x1_ref, wt_ref, mod_ref, fw_ref,
                    ys_hbm, o_ref, buf, sem):
    i = pl.program_id(0)
    n_steps = pl.num_programs(0)
    d = x1_ref.shape[1]
    n_lt = d // LANES

    def issue(step, slot):
        base = step * tm

        def body(r, carry):
            _row_copy(ys_hbm, dest_ref[base + r], buf.at[slot, 0], r, sem.at[slot]).start(priority=0)
            _row_copy(ys_hbm, dest_ref[t_rows + base + r], buf.at[slot, 1], r, sem.at[slot]).start(priority=1)
            return carry

        lax.fori_loop(0, tm, body, 0, unroll=8)

    @pl.when(i == 0)
    def _():
        issue(0, 0)

    @pl.when(i + 1 < n_steps)
    def _():
        issue(i + 1, (i + 1) % 2)

    slot = i % 2
    for k in range(2):
        pltpu.make_async_copy(ys_hbm.at[pl.ds(0, tm * n_lt), :], buf.at[slot, k], sem.at[slot]).wait()
    bi = jnp.minimum(i // n_lat_tiles_per_b, n_b)
    g2 = mod_ref[0, pl.ds(bi, 1), 5 * d:6 * d]
    wt = wt_ref[...]
    y = (_load_token_tiles(buf.at[slot, 0], tm, n_lt) * wt[:, 0:1]
         + _load_token_tiles(buf.at[slot, 1], tm, n_lt) * wt[:, 1:2])
    out = x1_ref[...] + g2 * y
    if final:
        out = out * lax.rsqrt(jnp.mean(out * out, axis=-1, keepdims=True) + EPS) * fw_ref[...]
    o_ref[...] = out


def _combine(dest, x1, wts_t, mod, layer, final_w, ys, n_b, seq, final):
    t_rows, d = x1.shape
    tm = 256
    n_lt = d // LANES
    grid_spec = pltpu.PrefetchScalarGridSpec(
        num_scalar_prefetch=1,
        grid=(t_rows // tm,),
        in_specs=[pl.BlockSpec((tm, d), lambda i, *_: (i, 0)),
                  pl.BlockSpec((tm, 2), lambda i, *_: (i, 0)),
                  pl.BlockSpec((1,) + mod.shape[1:], lambda i, *_: (layer, 0, 0)),
                  pl.BlockSpec((1, d), lambda i, *_: (0, 0)),
                  pl.BlockSpec(memory_space=pl.ANY)],
        out_specs=pl.BlockSpec((tm, d), lambda i, *_: (i, 0)),
        scratch_shapes=[pltpu.VMEM((2, 2, tm * n_lt, LANES), F32), pltpu.SemaphoreType.DMA((2,))],
    )
    return pl.pallas_call(
        functools.partial(_combine_kernel, tm, t_rows, seq // tm, n_b, final),
        out_shape=jax.ShapeDtypeStruct((t_rows, d), F32),
        grid_spec=grid_spec,
        compiler_params=_params(("arbitrary",)),
        name="moe_combine",
    )(dest, x1, wts_t, mod, final_w, ys)


def _ffn_kernel(n_lt, be_ref, x_ref, wg_ref, wu_ref, wd_ref, o_ref):
    x = _load_token_tiles(x_ref, MOE_BLOCK, n_lt).astype(BF16)
    gate = jnp.dot(x, wg_ref[0], preferred_element_type=F32)
    up = jnp.dot(x, wu_ref[0], preferred_element_type=F32)
    hid = (_silu(gate) * up).astype(BF16)
    _store_token_tiles(o_ref, jnp.dot(hid, wd_ref[0], preferred_element_type=F32))


def _expert_ffn(blk_expert, xsorted, w_gate, w_up, w_down):
    d, de = w_gate.shape[-2:]
    n_lt = d // LANES
    n_blk = xsorted.shape[0] // (MOE_BLOCK * n_lt)
    grid_spec = pltpu.PrefetchScalarGridSpec(
        num_scalar_prefetch=1,
        grid=(n_blk,),
        in_specs=[pl.BlockSpec((MOE_BLOCK * n_lt, LANES), lambda i, be: (i, 0)),
                  pl.BlockSpec((1, d, de), lambda i, be: (be[i], 0, 0)),
                  pl.BlockSpec((1, d, de), lambda i, be: (be[i], 0, 0)),
                  pl.BlockSpec((1, de, d), lambda i, be: (be[i], 0, 0))],
        out_specs=pl.BlockSpec((MOE_BLOCK * n_lt, LANES), lambda i, be: (i, 0)),
    )
    return pl.pallas_call(
        functools.partial(_ffn_kernel, n_lt),
        out_shape=jax.ShapeDtypeStruct(xsorted.shape, F32),
        grid_spec=grid_spec,
        compiler_params=_params(("arbitrary",)),
        name="expert_ffn",
    )(blk_expert, xsorted, w_gate, w_up, w_down)


def _sincos_2d(length, dim):
    rows = length // GRID_W
    quarter = dim // 4
    omega = 1.0 / (POS_BASE ** (jnp.arange(quarter, dtype=F32) / quarter))
    ang_r = jnp.arange(rows, dtype=F32)[:, None] * omega
    ang_c = jnp.arange(GRID_W, dtype=F32)[:, None] * omega
    emb_r = jnp.concatenate([jnp.sin(ang_r), jnp.cos(ang_r)], axis=-1)
    emb_c = jnp.concatenate([jnp.sin(ang_c), jnp.cos(ang_c)], axis=-1)
    half = dim // 2
    emb = jnp.concatenate([jnp.broadcast_to(emb_r[:, None, :], (rows, GRID_W, half)),
                           jnp.broadcast_to(emb_c[None, :, :], (rows, GRID_W, half))], axis=-1)
    return emb.reshape(rows * GRID_W, dim)


def _moe(h2, x1, idx, wts, counts, mod, layer, final_w, w_gate, w_up, w_down, n_b, seq, final):
    t_rows, d = x1.shape
    a = 2 * t_rows
    counts = counts[:, 0]
    padded = (counts + MOE_BLOCK - 1) // MOE_BLOCK * MOE_BLOCK
    pad_end = jnp.cumsum(padded)
    pad_start = pad_end - padded
    n_blk = (a + N_EXPERTS * (MOE_BLOCK - 1) + MOE_BLOCK - 1) // MOE_BLOCK
    p_rows = n_blk * MOE_BLOCK
    blk_start = jnp.arange(n_blk, dtype=jnp.int32) * MOE_BLOCK
    blk_expert = jnp.minimum(jnp.sum(blk_start[:, None] >= pad_end[None, :], axis=1),
                             N_EXPERTS - 1).astype(jnp.int32)
    e_ids = jnp.arange(N_EXPERTS, dtype=jnp.int32)
    seg_start = jnp.sum(jnp.where(idx[0:2, :, None] == e_ids, pad_start, 0), axis=-1)
    dest = (seg_start + idx[2:4]).reshape(-1).astype(jnp.int32)
    pad_lo = (pad_start + counts).astype(jnp.int32)
    pad_hi = pad_end.at[N_EXPERTS - 1].set(p_rows).astype(jnp.int32)
    xsorted = _dispatch(dest, pad_lo, pad_hi, h2, t_rows, p_rows)
    ys = _expert_ffn(blk_expert, xsorted, w_gate, w_up, w_down)
    return _combine(dest, x1, wts.T, mod, layer, final_w, ys, n_b, seq, final)


def kernel(x, c, ctx, c_ctx, ada_w, ada_b, norm_mix_w, norm_ffn_w, w_in, conv_w, a_log, dt_bias,
           dn_norm_w, ft_norm_w, w_out, router_w, router_bias, w_gate, w_up, w_down, final_norm_w):
    n_b, seq, d = x.shape
    ctx_len = ctx.shape[1]
    depth = ada_w.shape[0]
    n_lat = n_b * seq
    n_tok = n_lat + n_b * ctx_len
    ft_off = 4 * DN_WIDTH + N_GATE_COLS

    xs = jnp.concatenate([(x + _sincos_2d(seq, d)[None]).reshape(n_lat, d),
                          ctx.reshape(n_b * ctx_len, d)], axis=0)

    mod_rows = -(-(n_b + 1) // 8) * 8
    cvec = jnp.zeros((mod_rows, d), F32).at[:n_b].set(c).at[n_b].set(c_ctx)
    mod = _ada(cvec, ada_w, ada_b)

    cos_l, sin_l = _dft_tables(seq, seq // 2)
    cos_c, sin_c = _dft_tables(ctx_len, ctx_len // 2)
    cc, sc = _dft_tables(FT_GROUP_DIM)
    dft_cs = jnp.concatenate([cc, sc], axis=1)
    assert d == SUBLANES * LANES, "token tiles assume one (8, 128) tile per token"
    final_w = final_norm_w.reshape(1, d)
    router_wt = router_w.T
    rbias = router_bias.reshape(N_EXPERTS, 1)
    wg_b, wu_b, wd_b = w_gate.astype(BF16), w_up.astype(BF16), w_down.astype(BF16)

    col = jnp.arange(N_GATE_COLS)
    col_head, col_dir, col_ab = col // 4, (col // 2) % 2, col % 2
    gate_src = col_dir * (2 * N_DN_HEADS) + col_ab * N_DN_HEADS + col_head
    is_a = (col_ab == 0).astype(F32)
    is_bwd = (col_dir == 1).astype(F32)

    for i in range(depth):
        last = i == depth - 1
        w = w_in[i]
        wab = w[:, 4 * DN_WIDTH:ft_off][:, gate_src].astype(BF16)
        neg_a = -jnp.exp(a_log[i])[col_dir, col_head] * is_a
        dtb = dt_bias[i][col_dir, col_head] * is_a
        gpar = jnp.stack([neg_a, dtb, is_a, is_bwd])
        gpar_pad = jnp.pad(gpar, ((0, 0), (0, GATE_PAD - N_GATE_COLS)))
        wab_pad = jnp.pad(wab, ((0, 0), (0, GATE_PAD - N_GATE_COLS)))
        qkv, z, xc, xsn, gcol, grow = _in_proj(
            xs, mod, i, norm_mix_w[i].reshape(1, d),
            w[:, :3 * DN_WIDTH].astype(BF16), w[:, 3 * DN_WIDTH:4 * DN_WIDTH].astype(BF16),
            w[:, ft_off:].astype(BF16), wab_pad, wab.T, gpar_pad, gpar.T, dft_cs, n_b, seq)
        dn_l, dn_c = _deltanet(qkv, z, gcol, grow, conv_w[i], dn_norm_w[i].reshape(1, DN_HEAD_DIM),
                               n_b, seq, ctx_len)
        fnw = ft_norm_w[i].reshape(1, FT_WIDTH)
        ft_l = _fourier(xc, xsn, cos_l, sin_l, fnw, n_b, seq, 0)
        if last:
            ft_c, rows = dn_c, n_lat
        else:
            ft_c = _fourier(xc, xsn, cos_c, sin_c, fnw, n_b, ctx_len, n_lat // ctx_len)
            rows = n_tok
        wo = w_out[i].astype(BF16)
        x1, h2, logits_t = _out_proj(xs, dn_l, dn_c, ft_l, ft_c, wo[:DN_WIDTH], wo[DN_WIDTH:], mod, i,
                                     norm_ffn_w[i].reshape(1, d), router_wt, n_b, seq, rows)
        idx, wts, counts = _route(logits_t, rbias)
        xs = _moe(h2, x1, idx, wts, counts, mod, i, final_w, wg_b[i], wu_b[i], wd_b[i], n_b, seq, last)
    return xs.reshape(n_b, seq, d)
```

```python
import functools
import math

import jax
import jax.numpy as jnp
from jax import lax
from jax.experimental import pallas as pl
from jax.experimental.pallas import tpu as pltpu

F32 = jnp.float32
BF16 = jnp.bfloat16

GRID_W = 64
N_DN_HEADS = 4
DN_HEAD_DIM = 128
DN_WIDTH = N_DN_HEADS * DN_HEAD_DIM
N_FT_GROUPS = 4
FT_GROUP_DIM = 128
FT_WIDTH = N_FT_GROUPS * FT_GROUP_DIM
CONV_K = 5
CHUNK = 64
N_EXPERTS = 16
N_EXPERT_GROUPS = 4
EXPERTS_PER_GROUP = N_EXPERTS // N_EXPERT_GROUPS
D_EXPERT = 512
MOE_BLOCK = 256
POS_BASE = 10000.0
EPS = 1e-6
N_GATE_COLS = 4 * N_DN_HEADS
GATE_PAD = 128
CONV_BLOCK = 256
DN_UNROLL = 16

LANES = 128
SUBLANES = 8
VMEM_LIMIT = 56 * 1024 * 1024


def _params(sem, vmem=VMEM_LIMIT):
    return pltpu.CompilerParams(dimension_semantics=sem, vmem_limit_bytes=vmem)


def _dot(a, b):
    return jnp.dot(a.astype(BF16), b.astype(BF16), preferred_element_type=F32)


def _dot_nt(a, b):
    return lax.dot_general(a.astype(BF16), b.astype(BF16), (((1,), (1,)), ((), ())),
                           preferred_element_type=F32)


def _sigmoid(x):
    return 1.0 / (1.0 + jnp.exp(-x))


def _silu(x):
    return x * _sigmoid(x)


def _softplus(x):
    return jnp.maximum(x, 0.0) + jnp.log(1.0 + jnp.exp(-jnp.abs(x)))


def _ada_kernel(c_ref, w_ref, b_ref, o_ref):
    act = _silu(c_ref[...])
    o_ref[0] = jnp.dot(act, w_ref[0], preferred_element_type=F32,
                       precision=lax.Precision.HIGHEST) + b_ref[0]


def _ada(cvec, ada_w, ada_b):
    depth, d, n = ada_w.shape
    rows = cvec.shape[0]
    tn = 1024
    return pl.pallas_call(
        _ada_kernel,
        out_shape=jax.ShapeDtypeStruct((depth, rows, n), F32),
        grid=(depth, n // tn),
        in_specs=[pl.BlockSpec((rows, d), lambda i, j: (0, 0)),
                  pl.BlockSpec((1, d, tn), lambda i, j: (i, 0, j)),
                  pl.BlockSpec((1, 1, tn), lambda i, j: (i, 0, j))],
        out_specs=pl.BlockSpec((1, rows, tn), lambda i, j: (i, 0, j)),
        compiler_params=_params(("parallel", "parallel")),
        name="ada_mod",
    )(cvec, ada_w, ada_b.reshape(depth, 1, n))


def _seg_scan(x, pos, axis, reverse):
    n = x.shape[axis]
    s = 1
    while s < CHUNK:
        if reverse:
            shifted = pltpu.roll(x, n - s, axis)
            x = x + jnp.where(pos < CHUNK - s, shifted, 0.0)
        else:
            shifted = pltpu.roll(x, s, axis)
            x = x + jnp.where(pos >= s, shifted, 0.0)
        s *= 2
    return x


def _in_proj_kernel(tm, n_lat_tiles_per_b, n_b, x_ref, mod_ref, nw_ref, wqkv_ref, wz_ref, wft_ref,
                    wab_ref, wabt_ref, gpar_ref, gpart_ref, dft_ref,
                    qkv_ref, z_ref, xc_ref, xs_ref, gcol_ref, grow_ref):
    t = pl.program_id(0)
    bi = jnp.minimum(t // n_lat_tiles_per_b, n_b)
    d = x_ref.shape[1]
    x = x_ref[...]
    y = x * lax.rsqrt(jnp.mean(x * x, axis=-1, keepdims=True) + EPS) * nw_ref[...]
    shift = mod_ref[0, pl.ds(bi, 1), 0:d]
    scale = mod_ref[0, pl.ds(bi, 1), d:2 * d]
    h = (y * (1.0 + scale) + shift).astype(BF16)

    for j in range(3):
        cs = slice(j * DN_WIDTH, (j + 1) * DN_WIDTH)
        qkv_ref[:, cs] = jnp.dot(h, wqkv_ref[:, cs], preferred_element_type=F32)
    z_ref[...] = jnp.dot(h, wz_ref[...], preferred_element_type=F32)

    ft = jnp.dot(h, wft_ref[...], preferred_element_type=F32).astype(BF16)
    for g in range(N_FT_GROUPS):
        cs = slice(g * FT_GROUP_DIM, (g + 1) * FT_GROUP_DIM)
        cssn = jnp.dot(ft[:, cs], dft_ref[...], preferred_element_type=F32)
        xc_ref[:, cs] = cssn[:, :FT_GROUP_DIM].astype(BF16)
        xs_ref[:, cs] = cssn[:, FT_GROUP_DIM:].astype(BF16)

    ab = jnp.dot(h, wab_ref[...], preferred_element_type=F32)
    abt = lax.dot_general(wabt_ref[...], h, (((1,), (1,)), ((), ())), preferred_element_type=F32)

    def gates(v, par, axis):
        neg_a, dtb, is_a, is_bwd = par
        g = neg_a * _softplus(v + dtb)
        pos = lax.broadcasted_iota(jnp.int32, v.shape, axis) % CHUNK
        fwd = _seg_scan(g, pos, axis, reverse=False)
        bwd = _seg_scan(g, pos, axis, reverse=True)
        cum = jnp.where(is_bwd > 0.5, bwd, fwd)
        return jnp.where(is_a > 0.5, cum, _sigmoid(v))

    gp = gpar_ref[...]
    gc = gates(ab, (gp[0:1], gp[1:2], gp[2:3], gp[3:4]), 0)
    gpt = gpart_ref[...]
    gt = gates(abt, (gpt[:, 0:1], gpt[:, 1:2], gpt[:, 2:3], gpt[:, 3:4]), 1)
    for hh in range(N_DN_HEADS):
        gcol_ref[hh] = gc[:, 4 * hh:4 * hh + 4]
        for j in range(tm // CHUNK):
            cs = slice(j * CHUNK, (j + 1) * CHUNK)
            grow_ref[hh, j] = jnp.concatenate([gt[4 * hh:4 * hh + 1, cs], gt[4 * hh + 2:4 * hh + 3, cs]], axis=1)


def _in_proj(xs, mod_i, layer, norm_w, wqkv, wz, wft, wab, wabt, gpar, gpart, dft_cs, n_b, seq):
    t_rows, d = xs.shape
    tm = 512
    n_tiles = t_rows // tm
    kern = functools.partial(_in_proj_kernel, tm, seq // tm, n_b)
    const = lambda *shape: pl.BlockSpec(shape, lambda t: tuple(0 for _ in shape))
    rows = lambda w: pl.BlockSpec((tm, w), lambda t: (t, 0))
    return pl.pallas_call(
        kern,
        out_shape=(jax.ShapeDtypeStruct((t_rows, 3 * DN_WIDTH), F32),
                   jax.ShapeDtypeStruct((t_rows, DN_WIDTH), F32),
                   jax.ShapeDtypeStruct((t_rows, FT_WIDTH), BF16),
                   jax.ShapeDtypeStruct((t_rows, FT_WIDTH), BF16),
                   jax.ShapeDtypeStruct((N_DN_HEADS, t_rows, 4), F32),
                   jax.ShapeDtypeStruct((N_DN_HEADS, t_rows // CHUNK, 1, 2 * CHUNK), F32)),
        grid=(n_tiles,),
        in_specs=[rows(d),
                  pl.BlockSpec((1,) + mod_i.shape[1:], lambda t: (layer, 0, 0)),
                  const(1, d),
                  const(d, 3 * DN_WIDTH), const(d, DN_WIDTH), const(d, FT_WIDTH),
                  const(d, GATE_PAD), const(N_GATE_COLS, d),
                  const(4, GATE_PAD), const(N_GATE_COLS, 4),
                  const(FT_GROUP_DIM, 2 * FT_GROUP_DIM)],
        out_specs=(rows(3 * DN_WIDTH), rows(DN_WIDTH), rows(FT_WIDTH), rows(FT_WIDTH),
                   pl.BlockSpec((N_DN_HEADS, tm, 4), lambda t: (0, t, 0)),
                   pl.BlockSpec((N_DN_HEADS, tm // CHUNK, 1, 2 * CHUNK), lambda t: (0, t, 0, 0))),
        compiler_params=_params(("parallel",)),
        name="in_proj",
    )(xs, mod_i, norm_w, wqkv, wz, wft, wab, wabt, gpar, gpart, dft_cs)


def _conv_prep(src_ref, cw, n_rows, dst_ref, mode):
    blk = CONV_BLOCK
    n_blk = n_rows // blk
    halo = 8
    n_win = blk + 2 * halo

    def body(c, carry):
        r0 = pl.multiple_of(c * blk, blk)
        main = src_ref[pl.ds(r0, blk), :]
        prev = src_ref[pl.ds(pl.multiple_of(jnp.maximum(r0 - halo, 0), halo), halo), :]
        nxt = src_ref[pl.ds(pl.multiple_of(jnp.minimum(r0 + blk, n_rows - halo), halo), halo), :]
        prev = jnp.where(c > 0, prev, 0.0)
        nxt = jnp.where(c < n_blk - 1, nxt, 0.0)
        win = jnp.concatenate([prev, main, nxt], axis=0)
        acc = jnp.zeros((blk, DN_HEAD_DIM), F32)
        for j in range(CONV_K):
            shift = (CONV_K // 2 - j) % n_win
            rolled = win if shift == 0 else pltpu.roll(win, shift, 0)
            acc = acc + rolled[halo:halo + blk] * cw[j:j + 1, :]
        y = _silu(acc)
        if mode != "v":
            y = y * lax.rsqrt(jnp.sum(y * y, axis=-1, keepdims=True) + EPS)
        if mode == "q":
            y = y * (DN_HEAD_DIM ** -0.5)
        dst_ref[pl.ds(r0, blk), :] = y
        return carry

    lax.fori_loop(0, n_blk, body, 0)


def _blockdiag(x, isb):
    return jnp.concatenate([jnp.where(isb, 0.0, x), jnp.where(isb, x, 0.0)], axis=0).astype(BF16)


def _blockdiag_wide(x):
    w = x.shape[1] // 2
    zero = jnp.zeros((x.shape[0], w), BF16)
    xb = x.astype(BF16)
    return jnp.concatenate([jnp.concatenate([xb[:, :w], zero], axis=1),
                            jnp.concatenate([zero, xb[:, w:]], axis=1)], axis=0)


def _tri_inverse_dual(a_list, eye, xor, isb):
    mm = lambda x, y: jnp.dot(x.astype(BF16), _blockdiag(y, isb), preferred_element_type=F32)
    ad = [jnp.where((xor >> 3) == 0, a, 0.0) for a in a_list]
    a2 = [mm(x, x) for x in ad]
    a4 = [mm(x, x) for x in a2]
    t = [eye - x for x in ad]
    t = [x + mm(x, y) for x, y in zip(t, a2)]
    t = [x + mm(x, y) for x, y in zip(t, a4)]
    for s in (3, 4, 5):
        off = [jnp.where((xor >> s) == 1, a, 0.0) for a in a_list]
        to = [mm(x, y) for x, y in zip(t, off)]
        t = [x - mm(y, x) for x, y in zip(t, to)]
    return t


def _chunks_local(loaded):
    hd = DN_HEAD_DIM
    n = len(loaded)
    q = [x[0] for x in loaded]
    k = [x[1] for x in loaded]
    v = [x[2] for x in loaded]
    wide = lambda col: jnp.broadcast_to(col, (CHUNK, hd))
    gf = [wide(x[3][:, 0:1]) for x in loaded]
    bf_ = [wide(x[3][:, 1:2]) for x in loaded]
    gb = [wide(x[3][:, 2:3]) for x in loaded]
    bb = [wide(x[3][:, 3:4]) for x in loaded]
    grow = [x[4] for x in loaded]
    row = lax.broadcasted_iota(jnp.int32, (CHUNK, 2 * CHUNK), 0)
    lane = lax.broadcasted_iota(jnp.int32, (CHUNK, 2 * CHUNK), 1)
    jl = lane & (CHUNK - 1)
    isb = lane >= CHUNK
    delta = jnp.where(isb, jl - row, row - jl)
    xor = row ^ jl
    eye = jnp.where(delta == 0, 1.0, 0.0)
    nt = (((1,), (1,)), ((), ()))
    k2 = [jnp.concatenate([x, x], axis=0).astype(BF16) for x in k]
    kkd = [lax.dot_general(k[i].astype(BF16), k2[i], nt, preferred_element_type=F32) for i in range(n)]
    qkd = [lax.dot_general(q[i].astype(BF16), k2[i], nt, preferred_element_type=F32) for i in range(n)]
    dec = [jnp.where(delta >= 0, jnp.exp(jnp.where(isb, gb[i], gf[i]) - grow[i]), 0.0) for i in range(n)]
    a_mat = [jnp.where(delta > 0, kkd[i] * jnp.where(isb, bb[i], bf_[i]) * dec[i], 0.0) for i in range(n)]
    t_inv = _tri_inverse_dual(a_mat, eye, xor, isb)
    egf = [jnp.exp(x) for x in gf]
    egb = [jnp.exp(x) for x in gb]
    rhs = [jnp.concatenate([v[i] * bf_[i], k[i] * (bf_[i] * egf[i]), v[i] * bb[i], k[i] * (bb[i] * egb[i])],
                           axis=1) for i in range(n)]
    sol = [jnp.dot(t_inv[i].astype(BF16), _blockdiag_wide(rhs[i]), preferred_element_type=F32)
           for i in range(n)]
    bd_sol = [_blockdiag_wide(x) for x in sol]
    r1 = [jnp.dot((qkd[i] * dec[i]).astype(BF16), bd_sol[i], preferred_element_type=F32)
          for i in range(n)]
    glf = [x[CHUNK - 1:CHUNK, :] for x in gf]
    glb = [x[0:1, :] for x in gb]
    kdec = [jnp.concatenate([k[i] * jnp.exp(glf[i] - gf[i]), k[i] * jnp.exp(glb[i] - gb[i])], axis=0)
            for i in range(n)]
    r2 = [jnp.dot(kdec[i].T.astype(BF16), bd_sol[i], preferred_element_type=F32)
          for i in range(n)]
    out = []
    for i in range(n):
        o_loc = r1[i][:, 0:hd] + r1[i][:, 2 * hd:3 * hd]
        qt = jnp.concatenate([q[i] * egf[i] - r1[i][:, hd:2 * hd], q[i] * egb[i] - r1[i][:, 3 * hd:]],
                             axis=1).astype(BF16)
        nn = jnp.concatenate([r2[i][:, 0:hd], r2[i][:, 2 * hd:3 * hd]], axis=1)
        kw = jnp.concatenate([r2[i][:, hd:2 * hd], r2[i][:, 3 * hd:]], axis=1).astype(BF16)
        ge = jnp.concatenate([jnp.exp(glf[i]), jnp.exp(glb[i])], axis=0)
        out.append((o_loc, qt, nn, kw, ge))
    return out


def _state_step(c, d, oacc, qt_ref, kw_ref, nn_ref, ge_ref, s_ref):
    hd = DN_HEAD_DIM
    r0 = pl.multiple_of(c * CHUNK, CHUNK)
    cs = slice(d * hd, (d + 1) * hd)
    s = s_ref[d]
    lhs = jnp.concatenate([qt_ref[c, :, cs], kw_ref[c, :, cs]], axis=0)
    r = jnp.dot(lhs, s.astype(BF16), preferred_element_type=F32)
    oacc[pl.ds(r0, CHUNK), :] += r[:CHUNK]
    s_ref[d] = s * ge_ref[c, d:d + 1, :] + nn_ref[c, :, cs] - r[CHUNK:]


def _dn_kernel(seq, ctx_len,
               ql_ref, kl_ref, vl_ref, qc_ref, kc_ref, vc_ref, cwq_ref, cwk_ref, cwv_ref,
               zl_ref, zc_ref, gcl_ref, gcc_ref, grl_ref, grc_ref, nw_ref,
               ol_ref, oc_ref,
               qn, kn, vn, oacc, qt_s, kw_s, nn_s, ge_s, s_s):
    nw = nw_ref[...]
    s_s[...] = jnp.zeros_like(s_s)

    def segment(n_rows, q_ref, k_ref, v_ref, z_ref, gcol_ref, grow_ref, out_ref):
        _conv_prep(q_ref, cwq_ref[...], n_rows, qn, "q")
        _conv_prep(k_ref, cwk_ref[...], n_rows, kn, "k")
        _conv_prep(v_ref, cwv_ref[...], n_rows, vn, "v")
        n_chunks = n_rows // CHUNK

        unroll = math.gcd(DN_UNROLL, n_chunks)

        def local(i, carry):
            chunks = [i * unroll + j for j in range(unroll)]
            rows = [pl.ds(pl.multiple_of(c * CHUNK, CHUNK), CHUNK) for c in chunks]
            loaded = [(qn[r, :], kn[r, :], vn[r, :], gcol_ref[0, r, :], grow_ref[0, c])
                      for c, r in zip(chunks, rows)]
            results = _chunks_local(loaded)
            for c, r, (o_loc, qt, nn, kw, ge) in zip(chunks, rows, results):
                oacc[r, :] = o_loc
                qt_s[c] = qt
                nn_s[c] = nn
                kw_s[c] = kw
                ge_s[c] = ge
            return carry

        lax.fori_loop(0, n_chunks // unroll, local, 0)

        def step(s, carry):
            _state_step(s, 0, oacc, qt_s, kw_s, nn_s, ge_s, s_s)
            _state_step(n_chunks - 1 - s, 1, oacc, qt_s, kw_s, nn_s, ge_s, s_s)
            return carry

        lax.fori_loop(0, n_chunks, step, 0)

        blk = 256

        def fin(i, carry):
            r0 = pl.multiple_of(i * blk, blk)
            o = oacc[pl.ds(r0, blk), :]
            o = o * lax.rsqrt(jnp.mean(o * o, axis=-1, keepdims=True) + EPS) * nw
            out_ref[pl.ds(r0, blk), :] = (o * _silu(z_ref[pl.ds(r0, blk), :])).astype(out_ref.dtype)
            return carry

        lax.fori_loop(0, n_rows // blk, fin, 0)

    segment(ctx_len, qc_ref, kc_ref, vc_ref, zc_ref, gcc_ref, grc_ref, oc_ref)
    segment(seq, ql_ref, kl_ref, vl_ref, zl_ref, gcl_ref, grl_ref, ol_ref)


def _deltanet(qkv, z, gcol, grow, conv_w, dn_norm_w, n_b, seq, ctx_len):
    hd = DN_HEAD_DIM
    nc = seq // CHUNK
    cb = n_b * seq // ctx_len
    lat = lambda off: pl.BlockSpec((seq, hd), lambda b, h: (b, h + off))
    ctx = lambda off: pl.BlockSpec((ctx_len, hd), lambda b, h: (cb + b, h + off))
    cws = lambda off: pl.BlockSpec((CONV_K, hd), lambda b, h: (0, h + off))
    kern = functools.partial(_dn_kernel, seq, ctx_len)
    return pl.pallas_call(
        kern,
        out_shape=(jax.ShapeDtypeStruct((n_b * seq, DN_WIDTH), BF16),
                   jax.ShapeDtypeStruct((n_b * ctx_len, DN_WIDTH), BF16)),
        grid=(n_b, N_DN_HEADS),
        in_specs=[lat(0), lat(N_DN_HEADS), lat(2 * N_DN_HEADS),
                  ctx(0), ctx(N_DN_HEADS), ctx(2 * N_DN_HEADS),
                  cws(0), cws(N_DN_HEADS), cws(2 * N_DN_HEADS),
                  lat(0), ctx(0),
                  pl.BlockSpec((1, seq, 4), lambda b, h: (h, b, 0)),
                  pl.BlockSpec((1, ctx_len, 4), lambda b, h: (h, cb + b, 0)),
                  pl.BlockSpec((1, nc, 1, 2 * CHUNK), lambda b, h: (h, b, 0, 0)),
                  pl.BlockSpec((1, ctx_len // CHUNK, 1, 2 * CHUNK), lambda b, h: (h, cb + b, 0, 0)),
                  pl.BlockSpec((1, hd), lambda b, h: (0, 0))],
        out_specs=(pl.BlockSpec((seq, hd), lambda b, h: (b, h)),
                   pl.BlockSpec((ctx_len, hd), lambda b, h: (b, h))),
        scratch_shapes=[pltpu.VMEM((seq, hd), F32) for _ in range(4)]
        + [pltpu.VMEM((nc, CHUNK, 2 * hd), BF16), pltpu.VMEM((nc, hd, 2 * hd), BF16),
           pltpu.VMEM((nc, hd, 2 * hd), F32), pltpu.VMEM((nc, 2, hd), F32),
           pltpu.VMEM((2, hd, hd), F32)],
        compiler_params=_params(("parallel", "parallel")),
        name="deltanet",
    )(qkv, qkv, qkv, qkv, qkv, qkv, conv_w, conv_w, conv_w, z, z, gcol, gcol, grow, grow, dn_norm_w)


def _reversed_rows(ref, first_row, n_rows, blk, flip_mat):
    n_blk = n_rows // blk
    parts = [jnp.dot(flip_mat, ref[pl.ds(first_row + (n_blk - 1 - i) * blk, blk), :],
                     preferred_element_type=F32) for i in range(n_blk)]
    return jnp.concatenate(parts, axis=0)


def _ft_kernel(tm, scale, cl_ref, sl_ref, xc_ref, xs_ref, nw_ref, o_ref, xcf, xsf, hi_s):
    mt = pl.program_id(1)
    n_m = pl.num_programs(1)
    half = xcf.shape[0]
    ii = lax.broadcasted_iota(jnp.int32, (tm, tm), 0)
    jj = lax.broadcasted_iota(jnp.int32, (tm, tm), 1)
    flip_mat = jnp.where(ii + jj == tm - 1, 1.0, 0.0).astype(BF16)
    row_h = lax.broadcasted_iota(jnp.int32, (half, 1), 0)

    def shifted_reverse(ref, first_row):
        rev = _reversed_rows(ref, first_row, half, tm, flip_mat)
        return jnp.where(row_h == 0, 0.0, pltpu.roll(rev, 1, 0))

    @pl.when(mt == 0)
    def _():
        xcf[...] = (xc_ref[0:half, :].astype(F32) + shifted_reverse(xc_ref, half)).astype(BF16)
        xsf[...] = (xs_ref[0:half, :].astype(F32) - shifted_reverse(xs_ref, half)).astype(BF16)

    a = jnp.dot(cl_ref[...], xcf[...], preferred_element_type=F32)
    b = jnp.dot(sl_ref[...], xsf[...], preferred_element_type=F32)
    x_nyq = xc_ref[half:half + 2 * SUBLANES, :].astype(F32)[0:1, :]
    m_idx = mt * tm + lax.broadcasted_iota(jnp.int32, (tm, 1), 0)
    corr = jnp.where((m_idx & 1) == 0, 1.0, -1.0) * x_nyq
    nw = nw_ref[...]

    def norm(y):
        y = y * scale
        return y * lax.rsqrt(jnp.mean(y * y, axis=-1, keepdims=True) + EPS) * nw

    r0 = pl.multiple_of(mt * tm, tm)
    o_ref[pl.ds(r0, tm), :] = norm(a - b + corr).astype(o_ref.dtype)
    hi_s[pl.ds(r0, tm), :] = norm(a + b + corr).astype(hi_s.dtype)

    @pl.when(mt == n_m - 1)
    def _():
        lane = lax.broadcasted_iota(jnp.int32, (SUBLANES, half), 1)
        alt = jnp.where((lane & 1) == 0, 1.0, -1.0).astype(BF16)
        y_nyq = norm(jnp.dot(alt, xcf[...], preferred_element_type=F32)[0:1, :] + x_nyq)
        upper = jnp.where(row_h == 0, y_nyq, pltpu.roll(_reversed_rows(hi_s, 0, half, tm, flip_mat), 1, 0))
        o_ref[half:, :] = upper.astype(o_ref.dtype)


def _fourier(xc, xs, cos_h, sin_h, ft_norm_w, n_b, length, row_block0):
    half = length // 2
    tm = min(512, half)
    n_m = half // tm
    scale = 1.0 / math.sqrt(length * FT_GROUP_DIM)
    return pl.pallas_call(
        functools.partial(_ft_kernel, tm, scale),
        out_shape=jax.ShapeDtypeStruct((n_b * length, FT_WIDTH), BF16),
        grid=(n_b, n_m),
        in_specs=[pl.BlockSpec((tm, half), lambda b, m: (m, 0)),
                  pl.BlockSpec((tm, half), lambda b, m: (m, 0)),
                  pl.BlockSpec((length, FT_WIDTH), lambda b, m: (row_block0 + b, 0)),
                  pl.BlockSpec((length, FT_WIDTH), lambda b, m: (row_block0 + b, 0)),
                  pl.BlockSpec((1, FT_WIDTH), lambda b, m: (0, 0))],
        out_specs=pl.BlockSpec((length, FT_WIDTH), lambda b, m: (b, 0)),
        scratch_shapes=[pltpu.VMEM((half, FT_WIDTH), BF16), pltpu.VMEM((half, FT_WIDTH), BF16),
                        pltpu.VMEM((half, FT_WIDTH), BF16)],
        compiler_params=_params(("parallel", "arbitrary")),
        name="fourier_seq",
    )(cos_h, sin_h, xc, xs, ft_norm_w)


def _dft_tables(length, n=None):
    m = jnp.arange(length if n is None else n, dtype=jnp.int32)
    ph = (m[:, None] * m[None, :]) % length
    ang = ph.astype(F32) * (2.0 * math.pi / length)
    return jnp.cos(ang).astype(BF16), jnp.sin(ang).astype(BF16)


def _store_token_tiles(ref, val):
    rows, d = val.shape
    n_lt = d // LANES
    for j in range(n_lt):
        ref[pl.ds(j, rows, stride=n_lt), :] = val[:, j * LANES:(j + 1) * LANES]


def _load_token_tiles(ref, rows, n_lt):
    return jnp.concatenate([ref[pl.ds(j, rows, stride=n_lt), :] for j in range(n_lt)], axis=1)


def _out_proj_kernel(n_lat_tiles, n_lat_tiles_per_b, n_b, x_ref, dnl_ref, dnc_ref, ftl_ref, ftc_ref,
                     wdn_ref, wft_ref, mod_ref, nw_ref, rwt_ref, x1_ref, h2_ref, lg_ref):
    t = pl.program_id(0)
    bi = jnp.minimum(t // n_lat_tiles_per_b, n_b)
    d = x_ref.shape[1]
    is_lat = t < n_lat_tiles
    dn = jnp.where(is_lat, dnl_ref[...], dnc_ref[...])
    ft = jnp.where(is_lat, ftl_ref[...], ftc_ref[...])
    mix = (jnp.dot(dn, wdn_ref[...], preferred_element_type=F32)
           + jnp.dot(ft, wft_ref[...], preferred_element_type=F32))
    g1 = mod_ref[0, pl.ds(bi, 1), 2 * d:3 * d]
    sh2 = mod_ref[0, pl.ds(bi, 1), 3 * d:4 * d]
    sc2 = mod_ref[0, pl.ds(bi, 1), 4 * d:5 * d]
    x1 = x_ref[...] + g1 * mix
    x1_ref[...] = x1
    y = x1 * lax.rsqrt(jnp.mean(x1 * x1, axis=-1, keepdims=True) + EPS) * nw_ref[...]
    h2 = y * (1.0 + sc2) + sh2
    _store_token_tiles(h2_ref, h2)
    lg_ref[...] = lax.dot_general(rwt_ref[...], h2, (((1,), (1,)), ((), ())),
                                  preferred_element_type=F32, precision=lax.Precision.HIGHEST)


def _out_proj(xs, dn_l, dn_c, ft_l, ft_c, wdn, wft, mod_i, layer, norm_w, router_wt, n_b, seq, t_rows):
    d = xs.shape[1]
    tm = 512
    n_lt = d // LANES
    n_lat_tiles = n_b * seq // tm
    kern = functools.partial(_out_proj_kernel, n_lat_tiles, seq // tm, n_b)
    const = lambda *shape: pl.BlockSpec(shape, lambda t: tuple(0 for _ in shape))
    rows = lambda w: pl.BlockSpec((tm, w), lambda t: (t, 0))
    lat = lambda w: pl.BlockSpec((tm, w), lambda t: (jnp.minimum(t, n_lat_tiles - 1), 0))
    ctx = lambda w: pl.BlockSpec((tm, w), lambda t: (jnp.maximum(t - n_lat_tiles, 0), 0))
    return pl.pallas_call(
        kern,
        out_shape=(jax.ShapeDtypeStruct((t_rows, d), F32),
                   jax.ShapeDtypeStruct((t_rows * n_lt, LANES), F32),
                   jax.ShapeDtypeStruct((N_EXPERTS, t_rows), F32)),
        grid=(t_rows // tm,),
        in_specs=[rows(d), lat(DN_WIDTH), ctx(DN_WIDTH), lat(FT_WIDTH), ctx(FT_WIDTH),
                  const(DN_WIDTH, d), const(FT_WIDTH, d),
                  pl.BlockSpec((1,) + mod_i.shape[1:], lambda t: (layer, 0, 0)),
                  const(1, d), const(N_EXPERTS, d)],
        out_specs=(rows(d), pl.BlockSpec((tm * n_lt, LANES), lambda t: (t, 0)),
                   pl.BlockSpec((N_EXPERTS, tm), lambda t: (0, t))),
        compiler_params=_params(("parallel",)),
        name="out_proj",
    )(xs, dn_l, dn_c, ft_l, ft_c, wdn, wft, mod_i, norm_w, router_wt)


def _route_kernel(tr, lg_ref, bias_ref, idx_ref, wt_ref, cnt_ref, upper, carry):
    step = pl.program_id(0)

    @pl.when(step == 0)
    def _():
        a = lax.broadcasted_iota(jnp.int32, (tr, tr), 0)
        b = lax.broadcasted_iota(jnp.int32, (tr, tr), 1)
        upper[...] = jnp.where(a < b, 1.0, 0.0).astype(BF16)
        carry[...] = jnp.zeros_like(carry)

    scores = _sigmoid(lg_ref[...])
    biased = scores + bias_ref[...]
    rows = [biased[r:r + 1, :] for r in range(N_EXPERTS)]
    srow = [scores[r:r + 1, :] for r in range(N_EXPERTS)]
    epg = EXPERTS_PER_GROUP

    def group_score(g):
        best = None
        for i in range(epg):
            for j in range(i + 1, epg):
                pair = rows[g * epg + i] + rows[g * epg + j]
                best = pair if best is None else jnp.maximum(best, pair)
        return best

    best_g = jnp.zeros((1, tr), jnp.int32)
    best_v = group_score(0)
    for g in range(1, N_EXPERT_GROUPS):
        gs = group_score(g)
        take = gs > best_v
        best_g = jnp.where(take, g, best_g)
        best_v = jnp.where(take, gs, best_v)

    def pick(table, r):
        out = table[r]
        for g in range(1, N_EXPERT_GROUPS):
            out = jnp.where(best_g == g, table[g * epg + r], out)
        return out

    in_b = [pick(rows, r) for r in range(epg)]
    in_s = [pick(srow, r) for r in range(epg)]
    l1 = jnp.zeros((1, tr), jnp.int32)
    m1 = in_b[0]
    for r in range(1, epg):
        take = in_b[r] > m1
        l1 = jnp.where(take, r, l1)
        m1 = jnp.where(take, in_b[r], m1)
    l2 = jnp.full((1, tr), -1, jnp.int32)
    m2 = jnp.full((1, tr), -jnp.inf, F32)
    for r in range(epg):
        take = jnp.logical_and(l1 != r, jnp.logical_or(l2 < 0, in_b[r] > m2))
        l2 = jnp.where(take, r, l2)
        m2 = jnp.where(take, in_b[r], m2)
    s1 = in_s[0]
    s2 = in_s[0]
    for r in range(1, epg):
        s1 = jnp.where(l1 == r, in_s[r], s1)
        s2 = jnp.where(l2 == r, in_s[r], s2)
    e1 = best_g * epg + l1
    e2 = best_g * epg + l2
    tot = s1 + s2
    wt_ref[0:1, :] = s1 / tot
    wt_ref[1:2, :] = s2 / tot

    eid = lax.broadcasted_iota(jnp.int32, (N_EXPERTS, tr), 0)
    is1 = eid == e1
    is2 = eid == e2
    memb = jnp.where(jnp.logical_or(is1, is2), 1.0, 0.0)
    prefix = jnp.dot(memb.astype(BF16), upper[...], preferred_element_type=F32) + carry[...]
    rank1 = jnp.sum(jnp.where(is1, prefix, 0.0), axis=0, keepdims=True)
    rank2 = jnp.sum(jnp.where(is2, prefix, 0.0), axis=0, keepdims=True)
    idx_ref[0:1, :] = e1
    idx_ref[1:2, :] = e2
    idx_ref[2:3, :] = rank1.astype(jnp.int32)
    idx_ref[3:4, :] = rank2.astype(jnp.int32)
    new_carry = carry[...] + jnp.sum(memb, axis=1, keepdims=True)
    carry[...] = new_carry
    cnt_ref[...] = new_carry.astype(jnp.int32)


def _route(logits_t, router_bias):
    n_e, t_rows = logits_t.shape
    tr = 512
    return pl.pallas_call(
        functools.partial(_route_kernel, tr),
        out_shape=(jax.ShapeDtypeStruct((4, t_rows), jnp.int32),
                   jax.ShapeDtypeStruct((2, t_rows), F32),
                   jax.ShapeDtypeStruct((n_e, 1), jnp.int32)),
        grid=(t_rows // tr,),
        in_specs=[pl.BlockSpec((n_e, tr), lambda t: (0, t)),
                  pl.BlockSpec((n_e, 1), lambda t: (0, 0))],
        out_specs=(pl.BlockSpec((4, tr), lambda t: (0, t)),
                   pl.BlockSpec((2, tr), lambda t: (0, t)),
                   pl.BlockSpec((n_e, 1), lambda t: (0, 0))),
        scratch_shapes=[pltpu.VMEM((tr, tr), BF16), pltpu.VMEM((n_e, 1), F32)],
        compiler_params=_params(("arbitrary",)),
        name="route",
    )(logits_t, router_bias)


def _row_copy(src, src_row, dst, dst_row, sem):
    s0 = pl.multiple_of(src_row * SUBLANES, SUBLANES)
    d0 = pl.multiple_of(dst_row * SUBLANES, SUBLANES)
    return pltpu.make_async_copy(src.at[pl.ds(s0, SUBLANES), :], dst.at[pl.ds(d0, SUBLANES), :], sem)


def _dispatch_kernel(tm, t_rows, dest_ref, lo_ref, hi_ref, h_ref, xs_hbm, sem, pad_sem):
    i = pl.program_id(0)
    base = i * tm

    @pl.when(i == 0)
    def _():
        def per_expert(e, carry):
            lo, hi = lo_ref[e], hi_ref[e]

            def start(s, c):
                _row_copy(h_ref, 0, xs_hbm, s, pad_sem).start()
                return c

            def wait(s, c):
                _row_copy(h_ref, 0, xs_hbm, lo, pad_sem).wait()
                return c

            lax.fori_loop(lo, hi, start, 0)
            lax.fori_loop(lo, hi, wait, 0)
            return carry

        lax.fori_loop(0, N_EXPERTS, per_expert, 0)

    def body(r, carry):
        _row_copy(h_ref, r, xs_hbm, dest_ref[base + r], sem).start(priority=0)
        _row_copy(h_ref, r, xs_hbm, dest_ref[t_rows + base + r], sem).start(priority=1)
        return carry

    lax.fori_loop(0, tm, body, 0, unroll=8)
    whole = pltpu.make_async_copy(h_ref, xs_hbm.at[pl.ds(0, tm * SUBLANES), :], sem)
    whole.wait()
    whole.wait()


def _dispatch(dest, pad_lo, pad_hi, h2, t_rows, p_rows):
    tm = 512
    grid_spec = pltpu.PrefetchScalarGridSpec(
        num_scalar_prefetch=3,
        grid=(t_rows // tm,),
        in_specs=[pl.BlockSpec((tm * SUBLANES, LANES), lambda i, *_: (i, 0))],
        out_specs=pl.BlockSpec(memory_space=pl.ANY),
        scratch_shapes=[pltpu.SemaphoreType.DMA, pltpu.SemaphoreType.DMA],
    )
    return pl.pallas_call(
        functools.partial(_dispatch_kernel, tm, t_rows),
        out_shape=jax.ShapeDtypeStruct((p_rows * SUBLANES, LANES), F32),
        grid_spec=grid_spec,
        compiler_params=_params(("arbitrary",)),
        name="moe_dispatch",
    )(dest, pad_lo, pad_hi, h2)


def _combine_kernel(tm, t_rows, n_lat_tiles_per_b, n_b, final, dest_ref, x1_ref, wt_ref, mod_ref, fw_ref,
                    ys_hbm, o_ref, buf, sem):
    i = pl.program_id(0)
    n_steps = pl.num_programs(0)
    d = x1_ref.shape[1]
    n_lt = d // LANES

    def issue(step, slot):
        base = step * tm

        def body(r, carry):
            _row_copy(ys_hbm, dest_ref[base + r], buf.at[slot, 0], r, sem.at[slot]).start(priority=0)
            _row_copy(ys_hbm, dest_ref[t_rows + base + r], buf.at[slot, 1], r, sem.at[slot]).start(priority=1)
            return carry

        lax.fori_loop(0, tm, body, 0, unroll=8)

    @pl.when(i == 0)
    def _():
        issue(0, 0)

    @pl.when(i + 1 < n_steps)
    def _():
        issue(i + 1, (i + 1) % 2)

    slot = i % 2
    for k in range(2):
        pltpu.make_async_copy(ys_hbm.at[pl.ds(0, tm * n_lt), :], buf.at[slot, k], sem.at[slot]).wait()
    bi = jnp.minimum(i // n_lat_tiles_per_b, n_b)
    g2 = mod_ref[0, pl.ds(bi, 1), 5 * d:6 * d]
    wt = wt_ref[...]
    y = (_load_token_tiles(buf.at[slot, 0], tm, n_lt) * wt[:, 0:1]
         + _load_token_tiles(buf.at[slot, 1], tm, n_lt) * wt[:, 1:2])
    out = x1_ref[...] + g2 * y
    if final:
        out = out * lax.rsqrt(jnp.mean(out * out, axis=-1, keepdims=True) + EPS) * fw_ref[...]
    o_ref[...] = out


def _combine(dest, x1, wts_t, mod, layer, final_w, ys, n_b, seq, final):
    t_rows, d = x1.shape
    tm = 256
    n_lt = d // LANES
    grid_spec = pltpu.PrefetchScalarGridSpec(
        num_scalar_prefetch=1,
        grid=(t_rows // tm,),
        in_specs=[pl.BlockSpec((tm, d), lambda i, *_: (i, 0)),
                  pl.BlockSpec((tm, 2), lambda i, *_: (i, 0)),
                  pl.BlockSpec((1,) + mod.shape[1:], lambda i, *_: (layer, 0, 0)),
                  pl.BlockSpec((1, d), lambda i, *_: (0, 0)),
                  pl.BlockSpec(memory_space=pl.ANY)],
        out_specs=pl.BlockSpec((tm, d), lambda i, *_: (i, 0)),
        scratch_shapes=[pltpu.VMEM((2, 2, tm * n_lt, LANES), F32), pltpu.SemaphoreType.DMA((2,))],
    )
    return pl.pallas_call(
        functools.partial(_combine_kernel, tm, t_rows, seq // tm, n_b, final),
        out_shape=jax.ShapeDtypeStruct((t_rows, d), F32),
        grid_spec=grid_spec,
        compiler_params=_params(("arbitrary",)),
        name="moe_combine",
    )(dest, x1, wts_t, mod, final_w, ys)


def _ffn_kernel(n_lt, be_ref, x_ref, wg_ref, wu_ref, wd_ref, o_ref):
    x = _load_token_tiles(x_ref, MOE_BLOCK, n_lt).astype(BF16)
    gate = jnp.dot(x, wg_ref[0], preferred_element_type=F32)
    up = jnp.dot(x, wu_ref[0], preferred_element_type=F32)
    hid = (_silu(gate) * up).astype(BF16)
    _store_token_tiles(o_ref, jnp.dot(hid, wd_ref[0], preferred_element_type=F32))


def _expert_ffn(blk_expert, xsorted, w_gate, w_up, w_down):
    d, de = w_gate.shape[-2:]
    n_lt = d // LANES
    n_blk = xsorted.shape[0] // (MOE_BLOCK * n_lt)
    grid_spec = pltpu.PrefetchScalarGridSpec(
        num_scalar_prefetch=1,
        grid=(n_blk,),
        in_specs=[pl.BlockSpec((MOE_BLOCK * n_lt, LANES), lambda i, be: (i, 0)),
                  pl.BlockSpec((1, d, de), lambda i, be: (be[i], 0, 0)),
                  pl.BlockSpec((1, d, de), lambda i, be: (be[i], 0, 0)),
                  pl.BlockSpec((1, de, d), lambda i, be: (be[i], 0, 0))],
        out_specs=pl.BlockSpec((MOE_BLOCK * n_lt, LANES), lambda i, be: (i, 0)),
    )
    return pl.pallas_call(
        functools.partial(_ffn_kernel, n_lt),
        out_shape=jax.ShapeDtypeStruct(xsorted.shape, F32),
        grid_spec=grid_spec,
        compiler_params=_params(("arbitrary",)),
        name="expert_ffn",
    )(blk_expert, xsorted, w_gate, w_up, w_down)


def _sincos_2d(length, dim):
    rows = length // GRID_W
    quarter = dim // 4
    omega = 1.0 / (POS_BASE ** (jnp.arange(quarter, dtype=F32) / quarter))
    ang_r = jnp.arange(rows, dtype=F32)[:, None] * omega
    ang_c = jnp.arange(GRID_W, dtype=F32)[:, None] * omega
    emb_r = jnp.concatenate([jnp.sin(ang_r), jnp.cos(ang_r)], axis=-1)
    emb_c = jnp.concatenate([jnp.sin(ang_c), jnp.cos(ang_c)], axis=-1)
    half = dim // 2
    emb = jnp.concatenate([jnp.broadcast_to(emb_r[:, None, :], (rows, GRID_W, half)),
                           jnp.broadcast_to(emb_c[None, :, :], (rows, GRID_W, half))], axis=-1)
    return emb.reshape(rows * GRID_W, dim)


def _moe(h2, x1, idx, wts, counts, mod, layer, final_w, w_gate, w_up, w_down, n_b, seq, final):
    t_rows, d = x1.shape
    a = 2 * t_rows
    counts = counts[:, 0]
    padded = (counts + MOE_BLOCK - 1) // MOE_BLOCK * MOE_BLOCK
    pad_end = jnp.cumsum(padded)
    pad_start = pad_end - padded
    n_blk = (a + N_EXPERTS * (MOE_BLOCK - 1) + MOE_BLOCK - 1) // MOE_BLOCK
    p_rows = n_blk * MOE_BLOCK
    blk_start = jnp.arange(n_blk, dtype=jnp.int32) * MOE_BLOCK
    blk_expert = jnp.minimum(jnp.sum(blk_start[:, None] >= pad_end[None, :], axis=1),
                             N_EXPERTS - 1).astype(jnp.int32)
    e_ids = jnp.arange(N_EXPERTS, dtype=jnp.int32)
    seg_start = jnp.sum(jnp.where(idx[0:2, :, None] == e_ids, pad_start, 0), axis=-1)
    dest = (seg_start + idx[2:4]).reshape(-1).astype(jnp.int32)
    pad_lo = (pad_start + counts).astype(jnp.int32)
    pad_hi = pad_end.at[N_EXPERTS - 1].set(p_rows).astype(jnp.int32)
    xsorted = _dispatch(dest, pad_lo, pad_hi, h2, t_rows, p_rows)
    ys = _expert_ffn(blk_expert, xsorted, w_gate, w_up, w_down)
    return _combine(dest, x1, wts.T, mod, layer, final_w, ys, n_b, seq, final)


def kernel(x, c, ctx, c_ctx, ada_w, ada_b, norm_mix_w, norm_ffn_w, w_in, conv_w, a_log, dt_bias,
           dn_norm_w, ft_norm_w, w_out, router_w, router_bias, w_gate, w_up, w_down, final_norm_w):
    n_b, seq, d = x.shape
    ctx_len = ctx.shape[1]
    depth = ada_w.shape[0]
    n_lat = n_b * seq
    n_tok = n_lat + n_b * ctx_len
    ft_off = 4 * DN_WIDTH + N_GATE_COLS

    xs = jnp.concatenate([(x + _sincos_2d(seq, d)[None]).reshape(n_lat, d),
                          ctx.reshape(n_b * ctx_len, d)], axis=0)

    mod_rows = -(-(n_b + 1) // 8) * 8
    cvec = jnp.zeros((mod_rows, d), F32).at[:n_b].set(c).at[n_b].set(c_ctx)
    mod = _ada(cvec, ada_w, ada_b)

    cos_l, sin_l = _dft_tables(seq, seq // 2)
    cos_c, sin_c = _dft_tables(ctx_len, ctx_len // 2)
    cc, sc = _dft_tables(FT_GROUP_DIM)
    dft_cs = jnp.concatenate([cc, sc], axis=1)
    assert d == SUBLANES * LANES, "token tiles assume one (8, 128) tile per token"
    final_w = final_norm_w.reshape(1, d)
    router_wt = router_w.T
    rbias = router_bias.reshape(N_EXPERTS, 1)
    wg_b, wu_b, wd_b = w_gate.astype(BF16), w_up.astype(BF16), w_down.astype(BF16)

    col = jnp.arange(N_GATE_COLS)
    col_head, col_dir, col_ab = col // 4, (col // 2) % 2, col % 2
    gate_src = col_dir * (2 * N_DN_HEADS) + col_ab * N_DN_HEADS + col_head
    is_a = (col_ab == 0).astype(F32)
    is_bwd = (col_dir == 1).astype(F32)

    for i in range(depth):
        last = i == depth - 1
        w = w_in[i]
        wab = w[:, 4 * DN_WIDTH:ft_off][:, gate_src].astype(BF16)
        neg_a = -jnp.exp(a_log[i])[col_dir, col_head] * is_a
        dtb = dt_bias[i][col_dir, col_head] * is_a
        gpar = jnp.stack([neg_a, dtb, is_a, is_bwd])
        gpar_pad = jnp.pad(gpar, ((0, 0), (0, GATE_PAD - N_GATE_COLS)))
        wab_pad = jnp.pad(wab, ((0, 0), (0, GATE_PAD - N_GATE_COLS)))
        qkv, z, xc, xsn, gcol, grow = _in_proj(
            xs, mod, i, norm_mix_w[i].reshape(1, d),
            w[:, :3 * DN_WIDTH].astype(BF16), w[:, 3 * DN_WIDTH:4 * DN_WIDTH].astype(BF16),
            w[:, ft_off:].astype(BF16), wab_pad, wab.T, gpar_pad, gpar.T, dft_cs, n_b, seq)
        dn_l, dn_c = _deltanet(qkv, z, gcol, grow, conv_w[i], dn_norm_w[i].reshape(1, DN_HEAD_DIM),
                               n_b, seq, ctx_len)
        fnw = ft_norm_w[i].reshape(1, FT_WIDTH)
        ft_l = _fourier(xc, xsn, cos_l, sin_l, fnw, n_b, seq, 0)
        if last:
            ft_c, rows = dn_c, n_lat
        else:
            ft_c = _fourier(xc, xsn, cos_c, sin_c, fnw, n_b, ctx_len, n_lat // ctx_len)
            rows = n_tok
        wo = w_out[i].astype(BF16)
        x1, h2, logits_t = _out_proj(xs, dn_l, dn_c, ft_l, ft_c, wo[:DN_WIDTH], wo[DN_WIDTH:], mod, i,
                                     norm_ffn_w[i].reshape(1, d), router_wt, n_b, seq, rows)
        idx, wts, counts = _route(logits_t, rbias)
        xs = _moe(h2, x1, idx, wts, counts, mod, i, final_w, wg_b[i], wu_b[i], wd_b[i], n_b, seq, last)
    return xs.reshape(n_b, seq, d)
```

```python
import functools
import math

import jax
import jax.numpy as jnp
from jax import lax
from jax.experimental import pallas as pl
from jax.experimental.pallas import tpu as pltpu

F32 = jnp.float32
BF16 = jnp.bfloat16

GRID_W = 64
N_DN_HEADS = 4
DN_HEAD_DIM = 128
DN_WIDTH = N_DN_HEADS * DN_HEAD_DIM
N_FT_GROUPS = 4
FT_GROUP_DIM = 128
FT_WIDTH = N_FT_GROUPS * FT_GROUP_DIM
CONV_K = 5
CHUNK = 64
N_EXPERTS = 16
N_EXPERT_GROUPS = 4
EXPERTS_PER_GROUP = N_EXPERTS // N_EXPERT_GROUPS
D_EXPERT = 512
MOE_BLOCK = 256
POS_BASE = 10000.0
EPS = 1e-6
N_GATE_COLS = 4 * N_DN_HEADS
GATE_PAD = 128
CONV_BLOCK = 256
DN_UNROLL = 16

LANES = 128
SUBLANES = 8
VMEM_LIMIT = 56 * 1024 * 1024


def _params(sem, vmem=VMEM_LIMIT):
    return pltpu.CompilerParams(dimension_semantics=sem, vmem_limit_bytes=vmem)


def _dot(a, b):
    return jnp.dot(a.astype(BF16), b.astype(BF16), preferred_element_type=F32)


def _dot_nt(a, b):
    return lax.dot_general(a.astype(BF16), b.astype(BF16), (((1,), (1,)), ((), ())),
                           preferred_element_type=F32)


def _sigmoid(x):
    return 1.0 / (1.0 + jnp.exp(-x))


def _silu(x):
    return x * _sigmoid(x)


def _softplus(x):
    return jnp.maximum(x, 0.0) + jnp.log(1.0 + jnp.exp(-jnp.abs(x)))


def _ada_kernel(c_ref, w_ref, b_ref, o_ref):
    act = _silu(c_ref[...])
    o_ref[0] = jnp.dot(act, w_ref[0], preferred_element_type=F32,
                       precision=lax.Precision.HIGHEST) + b_ref[0]


def _ada(cvec, ada_w, ada_b):
    depth, d, n = ada_w.shape
    rows = cvec.shape[0]
    tn = 1024
    return pl.pallas_call(
        _ada_kernel,
        out_shape=jax.ShapeDtypeStruct((depth, rows, n), F32),
        grid=(depth, n // tn),
        in_specs=[pl.BlockSpec((rows, d), lambda i, j: (0, 0)),
                  pl.BlockSpec((1, d, tn), lambda i, j: (i, 0, j)),
                  pl.BlockSpec((1, 1, tn), lambda i, j: (i, 0, j))],
        out_specs=pl.BlockSpec((1, rows, tn), lambda i, j: (i, 0, j)),
        compiler_params=_params(("parallel", "parallel")),
        name="ada_mod",
    )(cvec, ada_w, ada_b.reshape(depth, 1, n))


def _seg_scan(x, pos, axis, reverse):
    n = x.shape[axis]
    s = 1
    while s < CHUNK:
        if reverse:
            shifted = pltpu.roll(x, n - s, axis)
            x = x + jnp.where(pos < CHUNK - s, shifted, 0.0)
        else:
            shifted = pltpu.roll(x, s, axis)
            x = x + jnp.where(pos >= s, shifted, 0.0)
        s *= 2
    return x


def _in_proj_kernel(tm, n_lat_tiles_per_b, n_b, x_ref, mod_ref, nw_ref, wqkv_ref, wz_ref, wft_ref,
                    wab_ref, wabt_ref, gpar_ref, gpart_ref, dft_ref,
                    qkv_ref, z_ref, xc_ref, xs_ref, gcol_ref, grow_ref):
    t = pl.program_id(0)
    bi = jnp.minimum(t // n_lat_tiles_per_b, n_b)
    d = x_ref.shape[1]
    x = x_ref[...]
    y = x * lax.rsqrt(jnp.mean(x * x, axis=-1, keepdims=True) + EPS) * nw_ref[...]
    shift = mod_ref[0, pl.ds(bi, 1), 0:d]
    scale = mod_ref[0, pl.ds(bi, 1), d:2 * d]
    h = (y * (1.0 + scale) + shift).astype(BF16)

    for j in range(3):
        cs = slice(j * DN_WIDTH, (j + 1) * DN_WIDTH)
        qkv_ref[:, cs] = jnp.dot(h, wqkv_ref[:, cs], preferred_element_type=F32)
    z_ref[...] = jnp.dot(h, wz_ref[...], preferred_element_type=F32)

    ft = jnp.dot(h, wft_ref[...], preferred_element_type=F32).astype(BF16)
    for g in range(N_FT_GROUPS):
        cs = slice(g * FT_GROUP_DIM, (g + 1) * FT_GROUP_DIM)
        cssn = jnp.dot(ft[:, cs], dft_ref[...], preferred_element_type=F32)
        xc_ref[:, cs] = cssn[:, :FT_GROUP_DIM].astype(BF16)
        xs_ref[:, cs] = cssn[:, FT_GROUP_DIM:].astype(BF16)

    ab = jnp.dot(h, wab_ref[...], preferred_element_type=F32)
    abt = lax.dot_general(wabt_ref[...], h, (((1,), (1,)), ((), ())), preferred_element_type=F32)

    def gates(v, par, axis):
        neg_a, dtb, is_a, is_bwd = par
        g = neg_a * _softplus(v + dtb)
        pos = lax.broadcasted_iota(jnp.int32, v.shape, axis) % CHUNK
        fwd = _seg_scan(g, pos, axis, reverse=False)
        bwd = _seg_scan(g, pos, axis, reverse=True)
        cum = jnp.where(is_bwd > 0.5, bwd, fwd)
        return jnp.where(is_a > 0.5, cum, _sigmoid(v))

    gp = gpar_ref[...]
    gc = gates(ab, (gp[0:1], gp[1:2], gp[2:3], gp[3:4]), 0)
    gpt = gpart_ref[...]
    gt = gates(abt, (gpt[:, 0:1], gpt[:, 1:2], gpt[:, 2:3], gpt[:, 3:4]), 1)
    for hh in range(N_DN_HEADS):
        gcol_ref[hh] = gc[:, 4 * hh:4 * hh + 4]
        for j in range(tm // CHUNK):
            cs = slice(j * CHUNK, (j + 1) * CHUNK)
            grow_ref[hh, j] = jnp.concatenate([gt[4 * hh:4 * hh + 1, cs], gt[4 * hh + 2:4 * hh + 3, cs]], axis=1)


def _in_proj(xs, mod_i, layer, norm_w, wqkv, wz, wft, wab, wabt, gpar, gpart, dft_cs, n_b, seq):
    t_rows, d = xs.shape
    tm = 512
    n_tiles = t_rows // tm
    kern = functools.partial(_in_proj_kernel, tm, seq // tm, n_b)
    const = lambda *shape: pl.BlockSpec(shape, lambda t: tuple(0 for _ in shape))
    rows = lambda w: pl.BlockSpec((tm, w), lambda t: (t, 0))
    return pl.pallas_call(
        kern,
        out_shape=(jax.ShapeDtypeStruct((t_rows, 3 * DN_WIDTH), F32),
                   jax.ShapeDtypeStruct((t_rows, DN_WIDTH), F32),
                   jax.ShapeDtypeStruct((t_rows, FT_WIDTH), BF16),
                   jax.ShapeDtypeStruct((t_rows, FT_WIDTH), BF16),
                   jax.ShapeDtypeStruct((N_DN_HEADS, t_rows, 4), F32),
                   jax.ShapeDtypeStruct((N_DN_HEADS, t_rows // CHUNK, 1, 2 * CHUNK), F32)),
        grid=(n_tiles,),
        in_specs=[rows(d),
                  pl.BlockSpec((1,) + mod_i.shape[1:], lambda t: (layer, 0, 0)),
                  const(1, d),
                  const(d, 3 * DN_WIDTH), const(d, DN_WIDTH), const(d, FT_WIDTH),
                  const(d, GATE_PAD), const(N_GATE_COLS, d),
                  const(4, GATE_PAD), const(N_GATE_COLS, 4),
                  const(FT_GROUP_DIM, 2 * FT_GROUP_DIM)],
        out_specs=(rows(3 * DN_WIDTH), rows(DN_WIDTH), rows(FT_WIDTH), rows(FT_WIDTH),
                   pl.BlockSpec((N_DN_HEADS, tm, 4), lambda t: (0, t, 0)),
                   pl.BlockSpec((N_DN_HEADS, tm // CHUNK, 1, 2 * CHUNK), lambda t: (0, t, 0, 0))),
        compiler_params=_params(("parallel",)),
        name="in_proj",
    )(xs, mod_i, norm_w, wqkv, wz, wft, wab, wabt, gpar, gpart, dft_cs)


def _conv_block(src_ref, cw, r0, n_rows, mode):
    blk = CONV_BLOCK
    halo = 8
    n_win = blk + 2 * halo
    zeros = jnp.zeros((halo, DN_HEAD_DIM), F32)
    prev = src_ref[r0 - halo:r0, :] if r0 > 0 else zeros
    nxt = src_ref[r0 + blk:r0 + blk + halo, :] if r0 + blk < n_rows else zeros
    win = jnp.concatenate([prev, src_ref[r0:r0 + blk, :], nxt], axis=0)
    acc = jnp.zeros((blk, DN_HEAD_DIM), F32)
    for j in range(CONV_K):
        shift = (CONV_K // 2 - j) % n_win
        rolled = win if shift == 0 else pltpu.roll(win, shift, 0)
        acc = acc + rolled[halo:halo + blk] * cw[j:j + 1, :]
    y = _silu(acc)
    if mode != "v":
        y = y * lax.rsqrt(jnp.sum(y * y, axis=-1, keepdims=True) + EPS)
    if mode == "q":
        y = y * (DN_HEAD_DIM ** -0.5)
    return y


def _blockdiag(x, isb):
    return jnp.concatenate([jnp.where(isb, 0.0, x), jnp.where(isb, x, 0.0)], axis=0).astype(BF16)


def _blockdiag_wide(x):
    w = x.shape[1] // 2
    zero = jnp.zeros((x.shape[0], w), BF16)
    xb = x.astype(BF16)
    return jnp.concatenate([jnp.concatenate([xb[:, :w], zero], axis=1),
                            jnp.concatenate([zero, xb[:, w:]], axis=1)], axis=0)


def _tri_inverse_dual(a_list, eye, xor, isb):
    mm = lambda x, y: jnp.dot(x.astype(BF16), _blockdiag(y, isb), preferred_element_type=F32)
    ad = [jnp.where((xor >> 3) == 0, a, 0.0) for a in a_list]
    a2 = [mm(x, x) for x in ad]
    a4 = [mm(x, x) for x in a2]
    t = [eye - x for x in ad]
    t = [x + mm(x, y) for x, y in zip(t, a2)]
    t = [x + mm(x, y) for x, y in zip(t, a4)]
    for s in (3, 4, 5):
        off = [jnp.where((xor >> s) == 1, a, 0.0) for a in a_list]
        to = [mm(x, y) for x, y in zip(t, off)]
        t = [x - mm(y, x) for x, y in zip(t, to)]
    return t


def _chunks_local(loaded):
    hd = DN_HEAD_DIM
    n = len(loaded)
    q = [x[0] for x in loaded]
    k = [x[1] for x in loaded]
    v = [x[2] for x in loaded]
    wide = lambda col: jnp.broadcast_to(col, (CHUNK, hd))
    gf = [wide(x[3][:, 0:1]) for x in loaded]
    bf_ = [wide(x[3][:, 1:2]) for x in loaded]
    gb = [wide(x[3][:, 2:3]) for x in loaded]
    bb = [wide(x[3][:, 3:4]) for x in loaded]
    grow = [x[4] for x in loaded]
    row = lax.broadcasted_iota(jnp.int32, (CHUNK, 2 * CHUNK), 0)
    lane = lax.broadcasted_iota(jnp.int32, (CHUNK, 2 * CHUNK), 1)
    jl = lane & (CHUNK - 1)
    isb = lane >= CHUNK
    delta = jnp.where(isb, jl - row, row - jl)
    xor = row ^ jl
    eye = jnp.where(delta == 0, 1.0, 0.0)
    nt = (((1,), (1,)), ((), ()))
    k2 = [jnp.concatenate([x, x], axis=0).astype(BF16) for x in k]
    kkd = [lax.dot_general(k[i].astype(BF16), k2[i], nt, preferred_element_type=F32) for i in range(n)]
    qkd = [lax.dot_general(q[i].astype(BF16), k2[i], nt, preferred_element_type=F32) for i in range(n)]
    dec = [jnp.where(delta >= 0, jnp.exp(jnp.where(isb, gb[i], gf[i]) - grow[i]), 0.0) for i in range(n)]
    a_mat = [jnp.where(delta > 0, kkd[i] * jnp.where(isb, bb[i], bf_[i]) * dec[i], 0.0) for i in range(n)]
    t_inv = _tri_inverse_dual(a_mat, eye, xor, isb)
    egf = [jnp.exp(x) for x in gf]
    egb = [jnp.exp(x) for x in gb]
    rhs = [jnp.concatenate([v[i] * bf_[i], k[i] * (bf_[i] * egf[i]), v[i] * bb[i], k[i] * (bb[i] * egb[i])],
                           axis=1) for i in range(n)]
    sol = [jnp.dot(t_inv[i].astype(BF16), _blockdiag_wide(rhs[i]), preferred_element_type=F32)
           for i in range(n)]
    bd_sol = [_blockdiag_wide(x) for x in sol]
    r1 = [jnp.dot((qkd[i] * dec[i]).astype(BF16), bd_sol[i], preferred_element_type=F32)
          for i in range(n)]
    glf = [x[CHUNK - 1:CHUNK, :] for x in gf]
    glb = [x[0:1, :] for x in gb]
    kdec = [jnp.concatenate([k[i] * jnp.exp(glf[i] - gf[i]), k[i] * jnp.exp(glb[i] - gb[i])], axis=0)
            for i in range(n)]
    r2 = [jnp.dot(kdec[i].T.astype(BF16), bd_sol[i], preferred_element_type=F32)
          for i in range(n)]
    out = []
    for i in range(n):
        o_loc = r1[i][:, 0:hd] + r1[i][:, 2 * hd:3 * hd]
        qt = jnp.concatenate([q[i] * egf[i] - r1[i][:, hd:2 * hd], q[i] * egb[i] - r1[i][:, 3 * hd:]],
                             axis=1).astype(BF16)
        nn = jnp.concatenate([r2[i][:, 0:hd], r2[i][:, 2 * hd:3 * hd]], axis=1)
        kw = jnp.concatenate([r2[i][:, hd:2 * hd], r2[i][:, 3 * hd:]], axis=1).astype(BF16)
        ge = jnp.concatenate([jnp.exp(glf[i]), jnp.exp(glb[i])], axis=0)
        out.append((o_loc, qt, nn, kw, ge))
    return out


def _state_step(c, d, oacc, qt_ref, kw_ref, nn_ref, ge_ref, s_ref):
    hd = DN_HEAD_DIM
    r0 = pl.multiple_of(c * CHUNK, CHUNK)
    cs = slice(d * hd, (d + 1) * hd)
    s = s_ref[d]
    lhs = jnp.concatenate([qt_ref[c, :, cs], kw_ref[c, :, cs]], axis=0)
    r = jnp.dot(lhs, s.astype(BF16), preferred_element_type=F32)
    oacc[pl.ds(r0, CHUNK), :] += r[:CHUNK]
    s_ref[d] = s * ge_ref[c, d:d + 1, :] + nn_ref[c, :, cs] - r[CHUNK:]


def _dn_kernel(seq, ctx_len,
               ql_ref, kl_ref, vl_ref, qc_ref, kc_ref, vc_ref, cwq_ref, cwk_ref, cwv_ref,
               zl_ref, zc_ref, gcl_ref, gcc_ref, grl_ref, grc_ref, nw_ref,
               ol_ref, oc_ref,
               qn, kn, vn, oacc, qt_s, kw_s, nn_s, ge_s, s_s):
    nw = nw_ref[...]
    s_s[...] = jnp.zeros_like(s_s)

    def segment(n_rows, q_ref, k_ref, v_ref, z_ref, gcol_ref, grow_ref, out_ref):
        n_chunks = n_rows // CHUNK
        unroll = math.gcd(DN_UNROLL, n_chunks)
        g_rows = unroll * CHUNK
        n_groups = n_chunks // unroll

        def prep(g, slot):
            for b0 in range(0, g_rows, CONV_BLOCK):
                r0 = g * g_rows + b0
                for src, cwr, dst, mode in ((q_ref, cwq_ref, qn, "q"), (k_ref, cwk_ref, kn, "k"),
                                            (v_ref, cwv_ref, vn, "v")):
                    dst[slot, b0:b0 + CONV_BLOCK, :] = _conv_block(src, cwr[...], r0, n_rows, mode)

        prep(0, 0)
        for g in range(n_groups):
            slot = g % 2
            chunks = [g * unroll + j for j in range(unroll)]
            tile = lambda ref, j: ref[slot, j * CHUNK:(j + 1) * CHUNK, :]
            loaded = [(tile(qn, j), tile(kn, j), tile(vn, j),
                       gcol_ref[0, c * CHUNK:(c + 1) * CHUNK, :], grow_ref[0, c])
                      for j, c in enumerate(chunks)]
            if g + 1 < n_groups:
                prep(g + 1, 1 - slot)
            results = _chunks_local(loaded)
            for c, (o_loc, qt, nn, kw, ge) in zip(chunks, results):
                oacc[c * CHUNK:(c + 1) * CHUNK, :] = o_loc
                qt_s[c] = qt
                nn_s[c] = nn
                kw_s[c] = kw
                ge_s[c] = ge

        def step(s, carry):
            _state_step(s, 0, oacc, qt_s, kw_s, nn_s, ge_s, s_s)
            _state_step(n_chunks - 1 - s, 1, oacc, qt_s, kw_s, nn_s, ge_s, s_s)
            return carry

        lax.fori_loop(0, n_chunks, step, 0)

        blk = 256

        def fin(i, carry):
            r0 = pl.multiple_of(i * blk, blk)
            o = oacc[pl.ds(r0, blk), :]
            o = o * lax.rsqrt(jnp.mean(o * o, axis=-1, keepdims=True) + EPS) * nw
            out_ref[pl.ds(r0, blk), :] = (o * _silu(z_ref[pl.ds(r0, blk), :])).astype(out_ref.dtype)
            return carry

        lax.fori_loop(0, n_rows // blk, fin, 0)

    segment(ctx_len, qc_ref, kc_ref, vc_ref, zc_ref, gcc_ref, grc_ref, oc_ref)
    segment(seq, ql_ref, kl_ref, vl_ref, zl_ref, gcl_ref, grl_ref, ol_ref)


def _deltanet(qkv, z, gcol, grow, conv_w, dn_norm_w, n_b, seq, ctx_len):
    hd = DN_HEAD_DIM
    nc = seq // CHUNK
    cb = n_b * seq // ctx_len
    lat = lambda off: pl.BlockSpec((seq, hd), lambda b, h: (b, h + off))
    ctx = lambda off: pl.BlockSpec((ctx_len, hd), lambda b, h: (cb + b, h + off))
    cws = lambda off: pl.BlockSpec((CONV_K, hd), lambda b, h: (0, h + off))
    kern = functools.partial(_dn_kernel, seq, ctx_len)
    return pl.pallas_call(
        kern,
        out_shape=(jax.ShapeDtypeStruct((n_b * seq, DN_WIDTH), BF16),
                   jax.ShapeDtypeStruct((n_b * ctx_len, DN_WIDTH), BF16)),
        grid=(n_b, N_DN_HEADS),
        in_specs=[lat(0), lat(N_DN_HEADS), lat(2 * N_DN_HEADS),
                  ctx(0), ctx(N_DN_HEADS), ctx(2 * N_DN_HEADS),
                  cws(0), cws(N_DN_HEADS), cws(2 * N_DN_HEADS),
                  lat(0), ctx(0),
                  pl.BlockSpec((1, seq, 4), lambda b, h: (h, b, 0)),
                  pl.BlockSpec((1, ctx_len, 4), lambda b, h: (h, cb + b, 0)),
                  pl.BlockSpec((1, nc, 1, 2 * CHUNK), lambda b, h: (h, b, 0, 0)),
                  pl.BlockSpec((1, ctx_len // CHUNK, 1, 2 * CHUNK), lambda b, h: (h, cb + b, 0, 0)),
                  pl.BlockSpec((1, hd), lambda b, h: (0, 0))],
        out_specs=(pl.BlockSpec((seq, hd), lambda b, h: (b, h)),
                   pl.BlockSpec((ctx_len, hd), lambda b, h: (b, h))),
        scratch_shapes=[pltpu.VMEM((2, min(DN_UNROLL, nc) * CHUNK, hd), F32) for _ in range(3)]
        + [pltpu.VMEM((seq, hd), F32),
           pltpu.VMEM((nc, CHUNK, 2 * hd), BF16), pltpu.VMEM((nc, hd, 2 * hd), BF16),
           pltpu.VMEM((nc, hd, 2 * hd), F32), pltpu.VMEM((nc, 2, hd), F32),
           pltpu.VMEM((2, hd, hd), F32)],
        compiler_params=_params(("parallel", "parallel")),
        name="deltanet",
    )(qkv, qkv, qkv, qkv, qkv, qkv, conv_w, conv_w, conv_w, z, z, gcol, gcol, grow, grow, dn_norm_w)


def _reversed_rows(ref, first_row, n_rows, blk, flip_mat):
    n_blk = n_rows // blk
    parts = [jnp.dot(flip_mat, ref[pl.ds(first_row + (n_blk - 1 - i) * blk, blk), :],
                     preferred_element_type=F32) for i in range(n_blk)]
    return jnp.concatenate(parts, axis=0)


def _ft_kernel(tm, scale, cl_ref, sl_ref, xc_ref, xs_ref, nw_ref, o_ref, xcf, xsf, hi_s):
    mt = pl.program_id(1)
    n_m = pl.num_programs(1)
    half = xcf.shape[0]
    ii = lax.broadcasted_iota(jnp.int32, (tm, tm), 0)
    jj = lax.broadcasted_iota(jnp.int32, (tm, tm), 1)
    flip_mat = jnp.where(ii + jj == tm - 1, 1.0, 0.0).astype(BF16)
    row_h = lax.broadcasted_iota(jnp.int32, (half, 1), 0)

    def shifted_reverse(ref, first_row):
        rev = _reversed_rows(ref, first_row, half, tm, flip_mat)
        return jnp.where(row_h == 0, 0.0, pltpu.roll(rev, 1, 0))

    @pl.when(mt == 0)
    def _():
        xcf[...] = (xc_ref[0:half, :].astype(F32) + shifted_reverse(xc_ref, half)).astype(BF16)
        xsf[...] = (xs_ref[0:half, :].astype(F32) - shifted_reverse(xs_ref, half)).astype(BF16)

    a = jnp.dot(cl_ref[...], xcf[...], preferred_element_type=F32)
    b = jnp.dot(sl_ref[...], xsf[...], preferred_element_type=F32)
    x_nyq = xc_ref[half:half + 2 * SUBLANES, :].astype(F32)[0:1, :]
    m_idx = mt * tm + lax.broadcasted_iota(jnp.int32, (tm, 1), 0)
    corr = jnp.where((m_idx & 1) == 0, 1.0, -1.0) * x_nyq
    nw = nw_ref[...]

    def norm(y):
        y = y * scale
        return y * lax.rsqrt(jnp.mean(y * y, axis=-1, keepdims=True) + EPS) * nw

    r0 = pl.multiple_of(mt * tm, tm)
    o_ref[pl.ds(r0, tm), :] = norm(a - b + corr).astype(o_ref.dtype)
    hi_s[pl.ds(r0, tm), :] = norm(a + b + corr).astype(hi_s.dtype)

    @pl.when(mt == n_m - 1)
    def _():
        lane = lax.broadcasted_iota(jnp.int32, (SUBLANES, half), 1)
        alt = jnp.where((lane & 1) == 0, 1.0, -1.0).astype(BF16)
        y_nyq = norm(jnp.dot(alt, xcf[...], preferred_element_type=F32)[0:1, :] + x_nyq)
        upper = jnp.where(row_h == 0, y_nyq, pltpu.roll(_reversed_rows(hi_s, 0, half, tm, flip_mat), 1, 0))
        o_ref[half:, :] = upper.astype(o_ref.dtype)


def _fourier(xc, xs, cos_h, sin_h, ft_norm_w, n_b, length, row_block0):
    half = length // 2
    tm = min(512, half)
    n_m = half // tm
    scale = 1.0 / math.sqrt(length * FT_GROUP_DIM)
    return pl.pallas_call(
        functools.partial(_ft_kernel, tm, scale),
        out_shape=jax.ShapeDtypeStruct((n_b * length, FT_WIDTH), BF16),
        grid=(n_b, n_m),
        in_specs=[pl.BlockSpec((tm, half), lambda b, m: (m, 0)),
                  pl.BlockSpec((tm, half), lambda b, m: (m, 0)),
                  pl.BlockSpec((length, FT_WIDTH), lambda b, m: (row_block0 + b, 0)),
                  pl.BlockSpec((length, FT_WIDTH), lambda b, m: (row_block0 + b, 0)),
                  pl.BlockSpec((1, FT_WIDTH), lambda b, m: (0, 0))],
        out_specs=pl.BlockSpec((length, FT_WIDTH), lambda b, m: (b, 0)),
        scratch_shapes=[pltpu.VMEM((half, FT_WIDTH), BF16), pltpu.VMEM((half, FT_WIDTH), BF16),
                        pltpu.VMEM((half, FT_WIDTH), BF16)],
        compiler_params=_params(("parallel", "arbitrary")),
        name="fourier_seq",
    )(cos_h, sin_h, xc, xs, ft_norm_w)


def _dft_tables(length, n=None):
    m = jnp.arange(length if n is None else n, dtype=jnp.int32)
    ph = (m[:, None] * m[None, :]) % length
    ang = ph.astype(F32) * (2.0 * math.pi / length)
    return jnp.cos(ang).astype(BF16), jnp.sin(ang).astype(BF16)


def _store_token_tiles(ref, val):
    rows, d = val.shape
    n_lt = d // LANES
    for j in range(n_lt):
        ref[pl.ds(j, rows, stride=n_lt), :] = val[:, j * LANES:(j + 1) * LANES]


def _load_token_tiles(ref, rows, n_lt):
    return jnp.concatenate([ref[pl.ds(j, rows, stride=n_lt), :] for j in range(n_lt)], axis=1)


def _out_proj_kernel(n_lat_tiles, n_lat_tiles_per_b, n_b, x_ref, dnl_ref, dnc_ref, ftl_ref, ftc_ref,
                     wdn_ref, wft_ref, mod_ref, nw_ref, rwt_ref, x1_ref, h2_ref, lg_ref):
    t = pl.program_id(0)
    bi = jnp.minimum(t // n_lat_tiles_per_b, n_b)
    d = x_ref.shape[1]
    is_lat = t < n_lat_tiles
    dn = jnp.where(is_lat, dnl_ref[...], dnc_ref[...])
    ft = jnp.where(is_lat, ftl_ref[...], ftc_ref[...])
    mix = (jnp.dot(dn, wdn_ref[...], preferred_element_type=F32)
           + jnp.dot(ft, wft_ref[...], preferred_element_type=F32))
    g1 = mod_ref[0, pl.ds(bi, 1), 2 * d:3 * d]
    sh2 = mod_ref[0, pl.ds(bi, 1), 3 * d:4 * d]
    sc2 = mod_ref[0, pl.ds(bi, 1), 4 * d:5 * d]
    x1 = x_ref[...] + g1 * mix
    x1_ref[...] = x1
    y = x1 * lax.rsqrt(jnp.mean(x1 * x1, axis=-1, keepdims=True) + EPS) * nw_ref[...]
    h2 = y * (1.0 + sc2) + sh2
    _store_token_tiles(h2_ref, h2)
    lg_ref[...] = lax.dot_general(rwt_ref[...], h2, (((1,), (1,)), ((), ())),
                                  preferred_element_type=F32, precision=lax.Precision.HIGHEST)


def _out_proj(xs, dn_l, dn_c, ft_l, ft_c, wdn, wft, mod_i, layer, norm_w, router_wt, n_b, seq, t_rows):
    d = xs.shape[1]
    tm = 512
    n_lt = d // LANES
    n_lat_tiles = n_b * seq // tm
    kern = functools.partial(_out_proj_kernel, n_lat_tiles, seq // tm, n_b)
    const = lambda *shape: pl.BlockSpec(shape, lambda t: tuple(0 for _ in shape))
    rows = lambda w: pl.BlockSpec((tm, w), lambda t: (t, 0))
    lat = lambda w: pl.BlockSpec((tm, w), lambda t: (jnp.minimum(t, n_lat_tiles - 1), 0))
    ctx = lambda w: pl.BlockSpec((tm, w), lambda t: (jnp.maximum(t - n_lat_tiles, 0), 0))
    return pl.pallas_call(
        kern,
        out_shape=(jax.ShapeDtypeStruct((t_rows, d), F32),
                   jax.ShapeDtypeStruct((t_rows * n_lt, LANES), F32),
                   jax.ShapeDtypeStruct((N_EXPERTS, t_rows), F32)),
        grid=(t_rows // tm,),
        in_specs=[rows(d), lat(DN_WIDTH), ctx(DN_WIDTH), lat(FT_WIDTH), ctx(FT_WIDTH),
                  const(DN_WIDTH, d), const(FT_WIDTH, d),
                  pl.BlockSpec((1,) + mod_i.shape[1:], lambda t: (layer, 0, 0)),
                  const(1, d), const(N_EXPERTS, d)],
        out_specs=(rows(d), pl.BlockSpec((tm * n_lt, LANES), lambda t: (t, 0)),
                   pl.BlockSpec((N_EXPERTS, tm), lambda t: (0, t))),
        compiler_params=_params(("parallel",)),
        name="out_proj",
    )(xs, dn_l, dn_c, ft_l, ft_c, wdn, wft, mod_i, norm_w, router_wt)


def _route_kernel(tr, lg_ref, bias_ref, idx_ref, wt_ref, cnt_ref, upper, carry):
    step = pl.program_id(0)

    @pl.when(step == 0)
    def _():
        a = lax.broadcasted_iota(jnp.int32, (tr, tr), 0)
        b = lax.broadcasted_iota(jnp.int32, (tr, tr), 1)
        upper[...] = jnp.where(a < b, 1.0, 0.0).astype(BF16)
        carry[...] = jnp.zeros_like(carry)

    scores = _sigmoid(lg_ref[...])
    biased = scores + bias_ref[...]
    rows = [biased[r:r + 1, :] for r in range(N_EXPERTS)]
    srow = [scores[r:r + 1, :] for r in range(N_EXPERTS)]
    epg = EXPERTS_PER_GROUP

    def group_score(g):
        best = None
        for i in range(epg):
            for j in range(i + 1, epg):
                pair = rows[g * epg + i] + rows[g * epg + j]
                best = pair if best is None else jnp.maximum(best, pair)
        return best

    best_g = jnp.zeros((1, tr), jnp.int32)
    best_v = group_score(0)
    for g in range(1, N_EXPERT_GROUPS):
        gs = group_score(g)
        take = gs > best_v
        best_g = jnp.where(take, g, best_g)
        best_v = jnp.where(take, gs, best_v)

    def pick(table, r):
        out = table[r]
        for g in range(1, N_EXPERT_GROUPS):
            out = jnp.where(best_g == g, table[g * epg + r], out)
        return out

    in_b = [pick(rows, r) for r in range(epg)]
    in_s = [pick(srow, r) for r in range(epg)]
    l1 = jnp.zeros((1, tr), jnp.int32)
    m1 = in_b[0]
    for r in range(1, epg):
        take = in_b[r] > m1
        l1 = jnp.where(take, r, l1)
        m1 = jnp.where(take, in_b[r], m1)
    l2 = jnp.full((1, tr), -1, jnp.int32)
    m2 = jnp.full((1, tr), -jnp.inf, F32)
    for r in range(epg):
        take = jnp.logical_and(l1 != r, jnp.logical_or(l2 < 0, in_b[r] > m2))
        l2 = jnp.where(take, r, l2)
        m2 = jnp.where(take, in_b[r], m2)
    s1 = in_s[0]
    s2 = in_s[0]
    for r in range(1, epg):
        s1 = jnp.where(l1 == r, in_s[r], s1)
        s2 = jnp.where(l2 == r, in_s[r], s2)
    e1 = best_g * epg + l1
    e2 = best_g * epg + l2
    tot = s1 + s2
    wt_ref[0:1, :] = s1 / tot
    wt_ref[1:2, :] = s2 / tot

    eid = lax.broadcasted_iota(jnp.int32, (N_EXPERTS, tr), 0)
    is1 = eid == e1
    is2 = eid == e2
    memb = jnp.where(jnp.logical_or(is1, is2), 1.0, 0.0)
    prefix = jnp.dot(memb.astype(BF16), upper[...], preferred_element_type=F32) + carry[...]
    rank1 = jnp.sum(jnp.where(is1, prefix, 0.0), axis=0, keepdims=True)
    rank2 = jnp.sum(jnp.where(is2, prefix, 0.0), axis=0, keepdims=True)
    idx_ref[0:1, :] = e1
    idx_ref[1:2, :] = e2
    idx_ref[2:3, :] = rank1.astype(jnp.int32)
    idx_ref[3:4, :] = rank2.astype(jnp.int32)
    new_carry = carry[...] + jnp.sum(memb, axis=1, keepdims=True)
    carry[...] = new_carry
    cnt_ref[...] = new_carry.astype(jnp.int32)


def _route(logits_t, router_bias):
    n_e, t_rows = logits_t.shape
    tr = 512
    return pl.pallas_call(
        functools.partial(_route_kernel, tr),
        out_shape=(jax.ShapeDtypeStruct((4, t_rows), jnp.int32),
                   jax.ShapeDtypeStruct((2, t_rows), F32),
                   jax.ShapeDtypeStruct((n_e, 1), jnp.int32)),
        grid=(t_rows // tr,),
        in_specs=[pl.BlockSpec((n_e, tr), lambda t: (0, t)),
                  pl.BlockSpec((n_e, 1), lambda t: (0, 0))],
        out_specs=(pl.BlockSpec((4, tr), lambda t: (0, t)),
                   pl.BlockSpec((2, tr), lambda t: (0, t)),
                   pl.BlockSpec((n_e, 1), lambda t: (0, 0))),
        scratch_shapes=[pltpu.VMEM((tr, tr), BF16), pltpu.VMEM((n_e, 1), F32)],
        compiler_params=_params(("arbitrary",)),
        name="route",
    )(logits_t, router_bias)


def _row_copy(src, src_row, dst, dst_row, sem):
    s0 = pl.multiple_of(src_row * SUBLANES, SUBLANES)
    d0 = pl.multiple_of(dst_row * SUBLANES, SUBLANES)
    return pltpu.make_async_copy(src.at[pl.ds(s0, SUBLANES), :], dst.at[pl.ds(d0, SUBLANES), :], sem)


def _dispatch_kernel(tm, t_rows, dest_ref, lo_ref, hi_ref, h_ref, xs_hbm, sem, pad_sem):
    i = pl.program_id(0)
    base = i * tm

    @pl.when(i == 0)
    def _():
        def per_expert(e, carry):
            lo, hi = lo_ref[e], hi_ref[e]

            def start(s, c):
                _row_copy(h_ref, 0, xs_hbm, s, pad_sem).start()
                return c

            def wait(s, c):
                _row_copy(h_ref, 0, xs_hbm, lo, pad_sem).wait()
                return c

            lax.fori_loop(lo, hi, start, 0)
            lax.fori_loop(lo, hi, wait, 0)
            return carry

        lax.fori_loop(0, N_EXPERTS, per_expert, 0)

    def body(r, carry):
        _row_copy(h_ref, r, xs_hbm, dest_ref[base + r], sem).start(priority=0)
        _row_copy(h_ref, r, xs_hbm, dest_ref[t_rows + base + r], sem).start(priority=1)
        return carry

    lax.fori_loop(0, tm, body, 0, unroll=8)
    whole = pltpu.make_async_copy(h_ref, xs_hbm.at[pl.ds(0, tm * SUBLANES), :], sem)
    whole.wait()
    whole.wait()


def _dispatch(dest, pad_lo, pad_hi, h2, t_rows, p_rows):
    tm = 512
    grid_spec = pltpu.PrefetchScalarGridSpec(
        num_scalar_prefetch=3,
        grid=(t_rows // tm,),
        in_specs=[pl.BlockSpec((tm * SUBLANES, LANES), lambda i, *_: (i, 0))],
        out_specs=pl.BlockSpec(memory_space=pl.ANY),
        scratch_shapes=[pltpu.SemaphoreType.DMA, pltpu.SemaphoreType.DMA],
    )
    return pl.pallas_call(
        functools.partial(_dispatch_kernel, tm, t_rows),
        out_shape=jax.ShapeDtypeStruct((p_rows * SUBLANES, LANES), F32),
        grid_spec=grid_spec,
        compiler_params=_params(("arbitrary",)),
        name="moe_dispatch",
    )(dest, pad_lo, pad_hi, h2)


def _combine_kernel(tm, t_rows, n_lat_tiles_per_b, n_b, final, dest_ref, x1_ref, wt_ref, mod_ref, fw_ref,
                    ys_hbm, o_ref, buf, sem):
    i = pl.program_id(0)
    n_steps = pl.num_programs(0)
    d = x1_ref.shape[1]
    n_lt = d // LANES

    def issue(step, slot):
        base = step * tm

        def body(r, carry):
            _row_copy(ys_hbm, dest_ref[base + r], buf.at[slot, 0], r, sem.at[slot]).start(priority=0)
            _row_copy(ys_hbm, dest_ref[t_rows + base + r], buf.at[slot, 1], r, sem.at[slot]).start(priority=1)
            return carry

        lax.fori_loop(0, tm, body, 0, unroll=8)

    @pl.when(i == 0)
    def _():
        issue(0, 0)

    @pl.when(i + 1 < n_steps)
    def _():
        issue(i + 1, (i + 1) % 2)

    slot = i % 2
    for k in range(2):
        pltpu.make_async_copy(ys_hbm.at[pl.ds(0, tm * n_lt), :], buf.at[slot, k], sem.at[slot]).wait()
    bi = jnp.minimum(i // n_lat_tiles_per_b, n_b)
    g2 = mod_ref[0, pl.ds(bi, 1), 5 * d:6 * d]
    wt = wt_ref[...]
    y = (_load_token_tiles(buf.at[slot, 0], tm, n_lt) * wt[:, 0:1]
         + _load_token_tiles(buf.at[slot, 1], tm, n_lt) * wt[:, 1:2])
    out = x1_ref[...] + g2 * y
    if final:
        out = out * lax.rsqrt(jnp.mean(out * out, axis=-1, keepdims=True) + EPS) * fw_ref[...]
    o_ref[...] = out


def _combine(dest, x1, wts_t, mod, layer, final_w, ys, n_b, seq, final):
    t_rows, d = x1.shape
    tm = 256
    n_lt = d // LANES
    grid_spec = pltpu.PrefetchScalarGridSpec(
        num_scalar_prefetch=1,
        grid=(t_rows // tm,),
        in_specs=[pl.BlockSpec((tm, d), lambda i, *_: (i, 0)),
                  pl.BlockSpec((tm, 2), lambda i, *_: (i, 0)),
                  pl.BlockSpec((1,) + mod.shape[1:], lambda i, *_: (layer, 0, 0)),
                  pl.BlockSpec((1, d), lambda i, *_: (0, 0)),
                  pl.BlockSpec(memory_space=pl.ANY)],
        out_specs=pl.BlockSpec((tm, d), lambda i, *_: (i, 0)),
        scratch_shapes=[pltpu.VMEM((2, 2, tm * n_lt, LANES), F32), pltpu.SemaphoreType.DMA((2,))],
    )
    return pl.pallas_call(
        functools.partial(_combine_kernel, tm, t_rows, seq // tm, n_b, final),
        out_shape=jax.ShapeDtypeStruct((t_rows, d), F32),
        grid_spec=grid_spec,
        compiler_params=_params(("arbitrary",)),
        name="moe_combine",
    )(dest, x1, wts_t, mod, final_w, ys)


def _ffn_kernel(n_lt, be_ref, x_ref, wg_ref, wu_ref, wd_ref, o_ref):
    x = _load_token_tiles(x_ref, MOE_BLOCK, n_lt).astype(BF16)
    gate = jnp.dot(x, wg_ref[0], preferred_element_type=F32)
    up = jnp.dot(x, wu_ref[0], preferred_element_type=F32)
    hid = (_silu(gate) * up).astype(BF16)
    _store_token_tiles(o_ref, jnp.dot(hid, wd_ref[0], preferred_element_type=F32))


def _expert_ffn(blk_expert, xsorted, w_gate, w_up, w_down):
    d, de = w_gate.shape[-2:]
    n_lt = d // LANES
    n_blk = xsorted.shape[0] // (MOE_BLOCK * n_lt)
    grid_spec = pltpu.PrefetchScalarGridSpec(
        num_scalar_prefetch=1,
        grid=(n_blk,),
        in_specs=[pl.BlockSpec((MOE_BLOCK * n_lt, LANES), lambda i, be: (i, 0)),
                  pl.BlockSpec((1, d, de), lambda i, be: (be[i], 0, 0)),
                  pl.BlockSpec((1, d, de), lambda i, be: (be[i], 0, 0)),
                  pl.BlockSpec((1, de, d), lambda i, be: (be[i], 0, 0))],
        out_specs=pl.BlockSpec((MOE_BLOCK * n_lt, LANES), lambda i, be: (i, 0)),
    )
    return pl.pallas_call(
        functools.partial(_ffn_kernel, n_lt),
        out_shape=jax.ShapeDtypeStruct(xsorted.shape, F32),
        grid_spec=grid_spec,
        compiler_params=_params(("arbitrary",)),
        name="expert_ffn",
    )(blk_expert, xsorted, w_gate, w_up, w_down)


def _sincos_2d(length, dim):
    rows = length // GRID_W
    quarter = dim // 4
    omega = 1.0 / (POS_BASE ** (jnp.arange(quarter, dtype=F32) / quarter))
    ang_r = jnp.arange(rows, dtype=F32)[:, None] * omega
    ang_c = jnp.arange(GRID_W, dtype=F32)[:, None] * omega
    emb_r = jnp.concatenate([jnp.sin(ang_r), jnp.cos(ang_r)], axis=-1)
    emb_c = jnp.concatenate([jnp.sin(ang_c), jnp.cos(ang_c)], axis=-1)
    half = dim // 2
    emb = jnp.concatenate([jnp.broadcast_to(emb_r[:, None, :], (rows, GRID_W, half)),
                           jnp.broadcast_to(emb_c[None, :, :], (rows, GRID_W, half))], axis=-1)
    return emb.reshape(rows * GRID_W, dim)


def _moe(h2, x1, idx, wts, counts, mod, layer, final_w, w_gate, w_up, w_down, n_b, seq, final):
    t_rows, d = x1.shape
    a = 2 * t_rows
    counts = counts[:, 0]
    padded = (counts + MOE_BLOCK - 1) // MOE_BLOCK * MOE_BLOCK
    pad_end = jnp.cumsum(padded)
    pad_start = pad_end - padded
    n_blk = (a + N_EXPERTS * (MOE_BLOCK - 1) + MOE_BLOCK - 1) // MOE_BLOCK
    p_rows = n_blk * MOE_BLOCK
    blk_start = jnp.arange(n_blk, dtype=jnp.int32) * MOE_BLOCK
    blk_expert = jnp.minimum(jnp.sum(blk_start[:, None] >= pad_end[None, :], axis=1),
                             N_EXPERTS - 1).astype(jnp.int32)
    e_ids = jnp.arange(N_EXPERTS, dtype=jnp.int32)
    seg_start = jnp.sum(jnp.where(idx[0:2, :, None] == e_ids, pad_start, 0), axis=-1)
    dest = (seg_start + idx[2:4]).reshape(-1).astype(jnp.int32)
    pad_lo = (pad_start + counts).astype(jnp.int32)
    pad_hi = pad_end.at[N_EXPERTS - 1].set(p_rows).astype(jnp.int32)
    xsorted = _dispatch(dest, pad_lo, pad_hi, h2, t_rows, p_rows)
    ys = _expert_ffn(blk_expert, xsorted, w_gate, w_up, w_down)
    return _combine(dest, x1, wts.T, mod, layer, final_w, ys, n_b, seq, final)


def kernel(x, c, ctx, c_ctx, ada_w, ada_b, norm_mix_w, norm_ffn_w, w_in, conv_w, a_log, dt_bias,
           dn_norm_w, ft_norm_w, w_out, router_w, router_bias, w_gate, w_up, w_down, final_norm_w):
    n_b, seq, d = x.shape
    ctx_len = ctx.shape[1]
    depth = ada_w.shape[0]
    n_lat = n_b * seq
    n_tok = n_lat + n_b * ctx_len
    ft_off = 4 * DN_WIDTH + N_GATE_COLS

    xs = jnp.concatenate([(x + _sincos_2d(seq, d)[None]).reshape(n_lat, d),
                          ctx.reshape(n_b * ctx_len, d)], axis=0)

    mod_rows = -(-(n_b + 1) // 8) * 8
    cvec = jnp.zeros((mod_rows, d), F32).at[:n_b].set(c).at[n_b].set(c_ctx)
    mod = _ada(cvec, ada_w, ada_b)

    cos_l, sin_l = _dft_tables(seq, seq // 2)
    cos_c, sin_c = _dft_tables(ctx_len, ctx_len // 2)
    cc, sc = _dft_tables(FT_GROUP_DIM)
    dft_cs = jnp.concatenate([cc, sc], axis=1)
    assert d == SUBLANES * LANES, "token tiles assume one (8, 128) tile per token"
    final_w = final_norm_w.reshape(1, d)
    router_wt = router_w.T
    rbias = router_bias.reshape(N_EXPERTS, 1)
    wg_b, wu_b, wd_b = w_gate.astype(BF16), w_up.astype(BF16), w_down.astype(BF16)

    col = jnp.arange(N_GATE_COLS)
    col_head, col_dir, col_ab = col // 4, (col // 2) % 2, col % 2
    gate_src = col_dir * (2 * N_DN_HEADS) + col_ab * N_DN_HEADS + col_head
    is_a = (col_ab == 0).astype(F32)
    is_bwd = (col_dir == 1).astype(F32)

    for i in range(depth):
        last = i == depth - 1
        w = w_in[i]
        wab = w[:, 4 * DN_WIDTH:ft_off][:, gate_src].astype(BF16)
        neg_a = -jnp.exp(a_log[i])[col_dir, col_head] * is_a
        dtb = dt_bias[i][col_dir, col_head] * is_a
        gpar = jnp.stack([neg_a, dtb, is_a, is_bwd])
        gpar_pad = jnp.pad(gpar, ((0, 0), (0, GATE_PAD - N_GATE_COLS)))
        wab_pad = jnp.pad(wab, ((0, 0), (0, GATE_PAD - N_GATE_COLS)))
        qkv, z, xc, xsn, gcol, grow = _in_proj(
            xs, mod, i, norm_mix_w[i].reshape(1, d),
            w[:, :3 * DN_WIDTH].astype(BF16), w[:, 3 * DN_WIDTH:4 * DN_WIDTH].astype(BF16),
            w[:, ft_off:].astype(BF16), wab_pad, wab.T, gpar_pad, gpar.T, dft_cs, n_b, seq)
        dn_l, dn_c = _deltanet(qkv, z, gcol, grow, conv_w[i], dn_norm_w[i].reshape(1, DN_HEAD_DIM),
                               n_b, seq, ctx_len)
        fnw = ft_norm_w[i].reshape(1, FT_WIDTH)
        ft_l = _fourier(xc, xsn, cos_l, sin_l, fnw, n_b, seq, 0)
        if last:
            ft_c, rows = dn_c, n_lat
        else:
            ft_c = _fourier(xc, xsn, cos_c, sin_c, fnw, n_b, ctx_len, n_lat // ctx_len)
            rows = n_tok
        wo = w_out[i].astype(BF16)
        x1, h2, logits_t = _out_proj(xs, dn_l, dn_c, ft_l, ft_c, wo[:DN_WIDTH], wo[DN_WIDTH:], mod, i,
                                     norm_ffn_w[i].reshape(1, d), router_wt, n_b, seq, rows)
        idx, wts, counts = _route(logits_t, rbias)
        xs = _moe(h2, x1, idx, wts, counts, mod, i, final_w, wg_b[i], wu_b[i], wd_b[i], n_b, seq, last)
    return xs.reshape(n_b, seq, d)
```

```python
import functools
import math

import jax
import jax.numpy as jnp
from jax import lax
from jax.experimental import pallas as pl
from jax.experimental.pallas import tpu as pltpu

F32 = jnp.float32
BF16 = jnp.bfloat16

GRID_W = 64
N_DN_HEADS = 4
DN_HEAD_DIM = 128
DN_WIDTH = N_DN_HEADS * DN_HEAD_DIM
N_FT_GROUPS = 4
FT_GROUP_DIM = 128
FT_WIDTH = N_FT_GROUPS * FT_GROUP_DIM
CONV_K = 5
CHUNK = 64
N_EXPERTS = 16
N_EXPERT_GROUPS = 4
EXPERTS_PER_GROUP = N_EXPERTS // N_EXPERT_GROUPS
D_EXPERT = 512
MOE_BLOCK = 256
POS_BASE = 10000.0
EPS = 1e-6
N_GATE_COLS = 4 * N_DN_HEADS
GATE_PAD = 128
CONV_BLOCK = 256
DN_UNROLL = 16

LANES = 128
SUBLANES = 8
VMEM_LIMIT = 56 * 1024 * 1024


def _params(sem, vmem=VMEM_LIMIT):
    return pltpu.CompilerParams(dimension_semantics=sem, vmem_limit_bytes=vmem)


def _dot(a, b):
    return jnp.dot(a.astype(BF16), b.astype(BF16), preferred_element_type=F32)


def _dot_nt(a, b):
    return lax.dot_general(a.astype(BF16), b.astype(BF16), (((1,), (1,)), ((), ())),
                           preferred_element_type=F32)


def _sigmoid(x):
    return 1.0 / (1.0 + jnp.exp(-x))


def _silu(x):
    return x * _sigmoid(x)


def _softplus(x):
    return jnp.maximum(x, 0.0) + jnp.log(1.0 + jnp.exp(-jnp.abs(x)))


def _ada_kernel(c_ref, w_ref, b_ref, o_ref):
    act = _silu(c_ref[...])
    o_ref[0] = jnp.dot(act, w_ref[0], preferred_element_type=F32,
                       precision=lax.Precision.HIGHEST) + b_ref[0]


def _ada(cvec, ada_w, ada_b):
    depth, d, n = ada_w.shape
    rows = cvec.shape[0]
    tn = 1024
    return pl.pallas_call(
        _ada_kernel,
        out_shape=jax.ShapeDtypeStruct((depth, rows, n), F32),
        grid=(depth, n // tn),
        in_specs=[pl.BlockSpec((rows, d), lambda i, j: (0, 0)),
                  pl.BlockSpec((1, d, tn), lambda i, j: (i, 0, j)),
                  pl.BlockSpec((1, 1, tn), lambda i, j: (i, 0, j))],
        out_specs=pl.BlockSpec((1, rows, tn), lambda i, j: (i, 0, j)),
        compiler_params=_params(("parallel", "parallel")),
        name="ada_mod",
    )(cvec, ada_w, ada_b.reshape(depth, 1, n))


def _seg_scan(x, pos, axis, reverse):
    n = x.shape[axis]
    s = 1
    while s < CHUNK:
        if reverse:
            shifted = pltpu.roll(x, n - s, axis)
            x = x + jnp.where(pos < CHUNK - s, shifted, 0.0)
        else:
            shifted = pltpu.roll(x, s, axis)
            x = x + jnp.where(pos >= s, shifted, 0.0)
        s *= 2
    return x


def _in_proj_kernel(tm, n_lat_tiles_per_b, n_b, x_ref, mod_ref, nw_ref, w_ref, wabt_ref, gpar_ref, gpart_ref,
                    dft_ref, qkv_ref, z_ref, xc_ref, xs_ref, gcol_ref, grow_ref):
    t = pl.program_id(0)
    bi = jnp.minimum(t // n_lat_tiles_per_b, n_b)
    d = x_ref.shape[1]
    x = x_ref[...]
    y = x * lax.rsqrt(jnp.mean(x * x, axis=-1, keepdims=True) + EPS) * nw_ref[0]
    shift = mod_ref[0, pl.ds(bi, 1), 0:d]
    scale = mod_ref[0, pl.ds(bi, 1), d:2 * d]
    h = (y * (1.0 + scale) + shift).astype(BF16)
    z_off, ft_off, ab_off = 3 * DN_WIDTH, 4 * DN_WIDTH, 4 * DN_WIDTH + FT_WIDTH

    for j in range(3):
        cs = slice(j * DN_WIDTH, (j + 1) * DN_WIDTH)
        qkv_ref[:, cs] = jnp.dot(h, w_ref[0, :, cs], preferred_element_type=F32)
    z_ref[...] = jnp.dot(h, w_ref[0, :, z_off:ft_off], preferred_element_type=F32)

    ft = jnp.dot(h, w_ref[0, :, ft_off:ab_off], preferred_element_type=F32).astype(BF16)
    for g in range(N_FT_GROUPS):
        cs = slice(g * FT_GROUP_DIM, (g + 1) * FT_GROUP_DIM)
        cssn = jnp.dot(ft[:, cs], dft_ref[...], preferred_element_type=F32)
        xc_ref[:, cs] = cssn[:, :FT_GROUP_DIM].astype(BF16)
        xs_ref[:, cs] = cssn[:, FT_GROUP_DIM:].astype(BF16)

    ab = jnp.dot(h, w_ref[0, :, ab_off:ab_off + GATE_PAD], preferred_element_type=F32)
    abt = lax.dot_general(wabt_ref[0], h, (((1,), (1,)), ((), ())), preferred_element_type=F32)

    def gates(v, par, axis):
        neg_a, dtb, is_a, is_bwd = par
        g = neg_a * _softplus(v + dtb)
        pos = lax.broadcasted_iota(jnp.int32, v.shape, axis) % CHUNK
        fwd = _seg_scan(g, pos, axis, reverse=False)
        bwd = _seg_scan(g, pos, axis, reverse=True)
        cum = jnp.where(is_bwd > 0.5, bwd, fwd)
        return jnp.where(is_a > 0.5, cum, _sigmoid(v))

    gp = gpar_ref[0]
    gc = gates(ab, (gp[0:1], gp[1:2], gp[2:3], gp[3:4]), 0)
    gpt = gpart_ref[0]
    gt = gates(abt, (gpt[:, 0:1], gpt[:, 1:2], gpt[:, 2:3], gpt[:, 3:4]), 1)
    for hh in range(N_DN_HEADS):
        gcol_ref[hh] = gc[:, 4 * hh:4 * hh + 4]
        for j in range(tm // CHUNK):
            cs = slice(j * CHUNK, (j + 1) * CHUNK)
            grow_ref[hh, j] = jnp.concatenate([gt[4 * hh:4 * hh + 1, cs], gt[4 * hh + 2:4 * hh + 3, cs]], axis=1)


def _in_proj(xs, mod_i, layer, norm_w, w_pack, wabt, gpar, gpart, dft_cs, n_b, seq):
    t_rows, d = xs.shape
    tm = 512
    n_tiles = t_rows // tm
    kern = functools.partial(_in_proj_kernel, tm, seq // tm, n_b)
    const = lambda *shape: pl.BlockSpec(shape, lambda t: tuple(0 for _ in shape))
    per_layer = lambda a: pl.BlockSpec((1,) + a.shape[1:], lambda t: (layer,) + (0,) * (a.ndim - 1))
    rows = lambda w: pl.BlockSpec((tm, w), lambda t: (t, 0))
    return pl.pallas_call(
        kern,
        out_shape=(jax.ShapeDtypeStruct((t_rows, 3 * DN_WIDTH), F32),
                   jax.ShapeDtypeStruct((t_rows, DN_WIDTH), F32),
                   jax.ShapeDtypeStruct((t_rows, FT_WIDTH), BF16),
                   jax.ShapeDtypeStruct((t_rows, FT_WIDTH), BF16),
                   jax.ShapeDtypeStruct((N_DN_HEADS, t_rows, 4), F32),
                   jax.ShapeDtypeStruct((N_DN_HEADS, t_rows // CHUNK, 1, 2 * CHUNK), F32)),
        grid=(n_tiles,),
        in_specs=[rows(d), per_layer(mod_i), per_layer(norm_w), per_layer(w_pack), per_layer(wabt),
                  per_layer(gpar), per_layer(gpart), const(FT_GROUP_DIM, 2 * FT_GROUP_DIM)],
        out_specs=(rows(3 * DN_WIDTH), rows(DN_WIDTH), rows(FT_WIDTH), rows(FT_WIDTH),
                   pl.BlockSpec((N_DN_HEADS, tm, 4), lambda t: (0, t, 0)),
                   pl.BlockSpec((N_DN_HEADS, tm // CHUNK, 1, 2 * CHUNK), lambda t: (0, t, 0, 0))),
        compiler_params=_params(("parallel",)),
        name="in_proj",
    )(xs, mod_i, norm_w, w_pack, wabt, gpar, gpart, dft_cs)


def _conv_block(src_ref, cw, r0, n_rows, mode):
    blk = CONV_BLOCK
    half_k = CONV_K // 2
    if r0 >= half_k and r0 + blk + half_k <= n_rows:
        acc = src_ref[r0 - half_k:r0 - half_k + blk, :] * cw[0:1, :]
        for j in range(1, CONV_K):
            acc = acc + src_ref[r0 + j - half_k:r0 + j - half_k + blk, :] * cw[j:j + 1, :]
    else:
        halo = 8
        n_win = blk + 2 * halo
        zeros = jnp.zeros((halo, DN_HEAD_DIM), F32)
        prev = src_ref[r0 - halo:r0, :] if r0 > 0 else zeros
        nxt = src_ref[r0 + blk:r0 + blk + halo, :] if r0 + blk < n_rows else zeros
        win = jnp.concatenate([prev, src_ref[r0:r0 + blk, :], nxt], axis=0)
        acc = jnp.zeros((blk, DN_HEAD_DIM), F32)
        for j in range(CONV_K):
            shift = (half_k - j) % n_win
            rolled = win if shift == 0 else pltpu.roll(win, shift, 0)
            acc = acc + rolled[halo:halo + blk] * cw[j:j + 1, :]
    y = _silu(acc)
    if mode == "v":
        return y
    inv = lax.rsqrt(jnp.sum(y * y, axis=-1, keepdims=True) + EPS)
    if mode == "q":
        inv = inv * (DN_HEAD_DIM ** -0.5)
    return y * inv


def _blockdiag(x, isb):
    xb = x.astype(BF16)
    keep_b = jnp.where(isb, 1.0, 0.0).astype(BF16)
    keep_f = jnp.where(isb, 0.0, 1.0).astype(BF16)
    return jnp.concatenate([xb * keep_f, xb * keep_b], axis=0)


def _blockdiag_wide(x):
    w = x.shape[1] // 2
    zero = jnp.zeros((x.shape[0], w), BF16)
    xb = x.astype(BF16)
    return jnp.concatenate([jnp.concatenate([xb[:, :w], zero], axis=1),
                            jnp.concatenate([zero, xb[:, w:]], axis=1)], axis=0)


def _tri_inverse_dual(a_list, eye, xor, isb):
    mm = lambda x, y: jnp.dot(x.astype(BF16), _blockdiag(y, isb), preferred_element_type=F32)
    ad = [jnp.where((xor >> 3) == 0, a, 0.0) for a in a_list]
    a2 = [mm(x, x) for x in ad]
    a4 = [mm(x, x) for x in a2]
    t = [eye - x for x in ad]
    t = [x + mm(x, y) for x, y in zip(t, a2)]
    t = [x + mm(x, y) for x, y in zip(t, a4)]
    for s in (3, 4, 5):
        off = [jnp.where((xor >> s) == 1, a, 0.0) for a in a_list]
        to = [mm(x, y) for x, y in zip(t, off)]
        t = [x - mm(y, x) for x, y in zip(t, to)]
    return t


def _chunks_local(loaded):
    hd = DN_HEAD_DIM
    n = len(loaded)
    q = [x[0] for x in loaded]
    k = [x[1] for x in loaded]
    v = [x[2] for x in loaded]
    wide = lambda col: jnp.broadcast_to(col, (CHUNK, hd))
    gf = [wide(x[3][:, 0:1]) for x in loaded]
    bf_ = [wide(x[3][:, 1:2]) for x in loaded]
    gb = [wide(x[3][:, 2:3]) for x in loaded]
    bb = [wide(x[3][:, 3:4]) for x in loaded]
    grow = [x[4] for x in loaded]
    row = lax.broadcasted_iota(jnp.int32, (CHUNK, 2 * CHUNK), 0)
    lane = lax.broadcasted_iota(jnp.int32, (CHUNK, 2 * CHUNK), 1)
    jl = lane & (CHUNK - 1)
    isb = lane >= CHUNK
    delta = jnp.where(isb, jl - row, row - jl)
    xor = row ^ jl
    eye = jnp.where(delta == 0, 1.0, 0.0)
    nt = (((1,), (1,)), ((), ()))
    k2 = [jnp.concatenate([x, x], axis=0).astype(BF16) for x in k]
    kkd = [lax.dot_general(k[i].astype(BF16), k2[i], nt, preferred_element_type=F32) for i in range(n)]
    qkd = [lax.dot_general(q[i].astype(BF16), k2[i], nt, preferred_element_type=F32) for i in range(n)]
    dec = [jnp.where(delta >= 0, jnp.exp(jnp.where(isb, gb[i], gf[i]) - grow[i]), 0.0) for i in range(n)]
    a_mat = [jnp.where(delta > 0, kkd[i] * jnp.where(isb, bb[i], bf_[i]) * dec[i], 0.0) for i in range(n)]
    t_inv = _tri_inverse_dual(a_mat, eye, xor, isb)
    egf = [jnp.exp(x) for x in gf]
    egb = [jnp.exp(x) for x in gb]
    rhs = [jnp.concatenate([v[i] * bf_[i], k[i] * (bf_[i] * egf[i]), v[i] * bb[i], k[i] * (bb[i] * egb[i])],
                           axis=1) for i in range(n)]
    sol = [jnp.dot(t_inv[i].astype(BF16), _blockdiag_wide(rhs[i]), preferred_element_type=F32)
           for i in range(n)]
    bd_sol = [_blockdiag_wide(x) for x in sol]
    r1 = [jnp.dot((qkd[i] * dec[i]).astype(BF16), bd_sol[i], preferred_element_type=F32)
          for i in range(n)]
    glf = [x[CHUNK - 1:CHUNK, :] for x in gf]
    glb = [x[0:1, :] for x in gb]
    kdec = [jnp.concatenate([k[i] * jnp.exp(glf[i] - gf[i]), k[i] * jnp.exp(glb[i] - gb[i])], axis=0)
            for i in range(n)]
    r2 = [jnp.dot(kdec[i].T.astype(BF16), bd_sol[i], preferred_element_type=F32)
          for i in range(n)]
    out = []
    for i in range(n):
        o_loc = r1[i][:, 0:hd] + r1[i][:, 2 * hd:3 * hd]
        qt = jnp.concatenate([q[i] * egf[i] - r1[i][:, hd:2 * hd], q[i] * egb[i] - r1[i][:, 3 * hd:]],
                             axis=1).astype(BF16)
        nn = jnp.concatenate([r2[i][:, 0:hd], r2[i][:, 2 * hd:3 * hd]], axis=1)
        kw = jnp.concatenate([r2[i][:, hd:2 * hd], r2[i][:, 3 * hd:]], axis=1).astype(BF16)
        ge = jnp.concatenate([jnp.exp(glf[i]), jnp.exp(glb[i])], axis=0)
        out.append((o_loc, qt, nn, kw, ge))
    return out


def _state_step(c, d, oacc, qt_ref, kw_ref, nn_ref, ge_ref, s_ref):
    hd = DN_HEAD_DIM
    r0 = pl.multiple_of(c * CHUNK, CHUNK)
    cs = slice(d * hd, (d + 1) * hd)
    s = s_ref[d]
    lhs = jnp.concatenate([qt_ref[c, :, cs], kw_ref[c, :, cs]], axis=0)
    r = jnp.dot(lhs, s.astype(BF16), preferred_element_type=F32)
    oacc[pl.ds(r0, CHUNK), :] += r[:CHUNK]
    s_ref[d] = s * ge_ref[c, d:d + 1, :] + nn_ref[c, :, cs] - r[CHUNK:]


def _dn_kernel(seq, ctx_len,
               ql_ref, kl_ref, vl_ref, qc_ref, kc_ref, vc_ref, cwq_ref, cwk_ref, cwv_ref,
               zl_ref, zc_ref, gcl_ref, gcc_ref, grl_ref, grc_ref, nw_ref,
               ol_ref, oc_ref,
               qn, kn, vn, oacc, qt_s, kw_s, nn_s, ge_s, s_s):
    nw = nw_ref[...]
    s_s[...] = jnp.zeros_like(s_s)

    def segment(n_rows, q_ref, k_ref, v_ref, z_ref, gcol_ref, grow_ref, out_ref):
        n_chunks = n_rows // CHUNK
        unroll = math.gcd(DN_UNROLL, n_chunks)
        g_rows = unroll * CHUNK
        n_groups = n_chunks // unroll

        def prep(g, slot):
            for b0 in range(0, g_rows, CONV_BLOCK):
                r0 = g * g_rows + b0
                for src, cwr, dst, mode in ((q_ref, cwq_ref, qn, "q"), (k_ref, cwk_ref, kn, "k"),
                                            (v_ref, cwv_ref, vn, "v")):
                    dst[slot, b0:b0 + CONV_BLOCK, :] = _conv_block(src, cwr[...], r0, n_rows, mode)

        prep(0, 0)
        for g in range(n_groups):
            slot = g % 2
            chunks = [g * unroll + j for j in range(unroll)]
            tile = lambda ref, j: ref[slot, j * CHUNK:(j + 1) * CHUNK, :]
            loaded = [(tile(qn, j), tile(kn, j), tile(vn, j),
                       gcol_ref[0, c * CHUNK:(c + 1) * CHUNK, :], grow_ref[0, c])
                      for j, c in enumerate(chunks)]
            if g + 1 < n_groups:
                prep(g + 1, 1 - slot)
            results = _chunks_local(loaded)
            for c, (o_loc, qt, nn, kw, ge) in zip(chunks, results):
                oacc[c * CHUNK:(c + 1) * CHUNK, :] = o_loc
                qt_s[c] = qt
                nn_s[c] = nn
                kw_s[c] = kw
                ge_s[c] = ge

        def step(s, carry):
            _state_step(s, 0, oacc, qt_s, kw_s, nn_s, ge_s, s_s)
            _state_step(n_chunks - 1 - s, 1, oacc, qt_s, kw_s, nn_s, ge_s, s_s)
            return carry

        lax.fori_loop(0, n_chunks, step, 0)

        blk = 256

        def fin(i, carry):
            r0 = pl.multiple_of(i * blk, blk)
            o = oacc[pl.ds(r0, blk), :]
            o = o * lax.rsqrt(jnp.mean(o * o, axis=-1, keepdims=True) + EPS) * nw
            out_ref[pl.ds(r0, blk), :] = (o * _silu(z_ref[pl.ds(r0, blk), :])).astype(out_ref.dtype)
            return carry

        lax.fori_loop(0, n_rows // blk, fin, 0)

    segment(ctx_len, qc_ref, kc_ref, vc_ref, zc_ref, gcc_ref, grc_ref, oc_ref)
    segment(seq, ql_ref, kl_ref, vl_ref, zl_ref, gcl_ref, grl_ref, ol_ref)


def _deltanet(qkv, z, gcol, grow, conv_w, dn_norm_w, n_b, seq, ctx_len):
    hd = DN_HEAD_DIM
    nc = seq // CHUNK
    cb = n_b * seq // ctx_len
    lat = lambda off: pl.BlockSpec((seq, hd), lambda b, h: (b, h + off))
    ctx = lambda off: pl.BlockSpec((ctx_len, hd), lambda b, h: (cb + b, h + off))
    cws = lambda off: pl.BlockSpec((CONV_K, hd), lambda b, h: (0, h + off))
    kern = functools.partial(_dn_kernel, seq, ctx_len)
    return pl.pallas_call(
        kern,
        out_shape=(jax.ShapeDtypeStruct((n_b * seq, DN_WIDTH), BF16),
                   jax.ShapeDtypeStruct((n_b * ctx_len, DN_WIDTH), BF16)),
        grid=(n_b, N_DN_HEADS),
        in_specs=[lat(0), lat(N_DN_HEADS), lat(2 * N_DN_HEADS),
                  ctx(0), ctx(N_DN_HEADS), ctx(2 * N_DN_HEADS),
                  cws(0), cws(N_DN_HEADS), cws(2 * N_DN_HEADS),
                  lat(0), ctx(0),
                  pl.BlockSpec((1, seq, 4), lambda b, h: (h, b, 0)),
                  pl.BlockSpec((1, ctx_len, 4), lambda b, h: (h, cb + b, 0)),
                  pl.BlockSpec((1, nc, 1, 2 * CHUNK), lambda b, h: (h, b, 0, 0)),
                  pl.BlockSpec((1, ctx_len // CHUNK, 1, 2 * CHUNK), lambda b, h: (h, cb + b, 0, 0)),
                  pl.BlockSpec((1, hd), lambda b, h: (0, 0))],
        out_specs=(pl.BlockSpec((seq, hd), lambda b, h: (b, h)),
                   pl.BlockSpec((ctx_len, hd), lambda b, h: (b, h))),
        scratch_shapes=[pltpu.VMEM((2, min(DN_UNROLL, nc) * CHUNK, hd), F32) for _ in range(3)]
        + [pltpu.VMEM((seq, hd), F32),
           pltpu.VMEM((nc, CHUNK, 2 * hd), BF16), pltpu.VMEM((nc, hd, 2 * hd), BF16),
           pltpu.VMEM((nc, hd, 2 * hd), F32), pltpu.VMEM((nc, 2, hd), F32),
           pltpu.VMEM((2, hd, hd), F32)],
        compiler_params=_params(("parallel", "parallel")),
        name="deltanet",
    )(qkv, qkv, qkv, qkv, qkv, qkv, conv_w, conv_w, conv_w, z, z, gcol, gcol, grow, grow, dn_norm_w)


def _reversed_rows(ref, first_row, n_rows, blk, flip_mat):
    n_blk = n_rows // blk
    parts = [jnp.dot(flip_mat, ref[pl.ds(first_row + (n_blk - 1 - i) * blk, blk), :],
                     preferred_element_type=F32) for i in range(n_blk)]
    return jnp.concatenate(parts, axis=0)


def _ft_kernel(tm, scale, cl_ref, sl_ref, xc_ref, xs_ref, nw_ref, o_ref, xcf, xsf, hi_s):
    mt = pl.program_id(1)
    n_m = pl.num_programs(1)
    half = xcf.shape[0]
    ii = lax.broadcasted_iota(jnp.int32, (tm, tm), 0)
    jj = lax.broadcasted_iota(jnp.int32, (tm, tm), 1)
    flip_mat = jnp.where(ii + jj == tm - 1, 1.0, 0.0).astype(BF16)
    row_h = lax.broadcasted_iota(jnp.int32, (half, 1), 0)

    def shifted_reverse(ref, first_row):
        rev = _reversed_rows(ref, first_row, half, tm, flip_mat)
        return jnp.where(row_h == 0, 0.0, pltpu.roll(rev, 1, 0))

    @pl.when(mt == 0)
    def _():
        xcf[...] = (xc_ref[0:half, :].astype(F32) + shifted_reverse(xc_ref, half)).astype(BF16)
        xsf[...] = (xs_ref[0:half, :].astype(F32) - shifted_reverse(xs_ref, half)).astype(BF16)

    a = jnp.dot(cl_ref[...], xcf[...], preferred_element_type=F32)
    b = jnp.dot(sl_ref[...], xsf[...], preferred_element_type=F32)
    x_nyq = xc_ref[half:half + 2 * SUBLANES, :].astype(F32)[0:1, :]
    m_idx = mt * tm + lax.broadcasted_iota(jnp.int32, (tm, 1), 0)
    corr = jnp.where((m_idx & 1) == 0, 1.0, -1.0) * x_nyq
    nw = nw_ref[...]

    def norm(y):
        y = y * scale
        return y * lax.rsqrt(jnp.mean(y * y, axis=-1, keepdims=True) + EPS) * nw

    r0 = pl.multiple_of(mt * tm, tm)
    o_ref[pl.ds(r0, tm), :] = norm(a - b + corr).astype(o_ref.dtype)
    hi_s[pl.ds(r0, tm), :] = norm(a + b + corr).astype(hi_s.dtype)

    @pl.when(mt == n_m - 1)
    def _():
        lane = lax.broadcasted_iota(jnp.int32, (SUBLANES, half), 1)
        alt = jnp.where((lane & 1) == 0, 1.0, -1.0).astype(BF16)
        y_nyq = norm(jnp.dot(alt, xcf[...], preferred_element_type=F32)[0:1, :] + x_nyq)
        upper = jnp.where(row_h == 0, y_nyq, pltpu.roll(_reversed_rows(hi_s, 0, half, tm, flip_mat), 1, 0))
        o_ref[half:, :] = upper.astype(o_ref.dtype)


def _fourier(xc, xs, cos_h, sin_h, ft_norm_w, n_b, length, row_block0):
    half = length // 2
    tm = min(512, half)
    n_m = half // tm
    scale = 1.0 / math.sqrt(length * FT_GROUP_DIM)
    return pl.pallas_call(
        functools.partial(_ft_kernel, tm, scale),
        out_shape=jax.ShapeDtypeStruct((n_b * length, FT_WIDTH), BF16),
        grid=(n_b, n_m),
        in_specs=[pl.BlockSpec((tm, half), lambda b, m: (m, 0)),
                  pl.BlockSpec((tm, half), lambda b, m: (m, 0)),
                  pl.BlockSpec((length, FT_WIDTH), lambda b, m: (row_block0 + b, 0)),
                  pl.BlockSpec((length, FT_WIDTH), lambda b, m: (row_block0 + b, 0)),
                  pl.BlockSpec((1, FT_WIDTH), lambda b, m: (0, 0))],
        out_specs=pl.BlockSpec((length, FT_WIDTH), lambda b, m: (b, 0)),
        scratch_shapes=[pltpu.VMEM((half, FT_WIDTH), BF16), pltpu.VMEM((half, FT_WIDTH), BF16),
                        pltpu.VMEM((half, FT_WIDTH), BF16)],
        compiler_params=_params(("parallel", "arbitrary")),
        name="fourier_seq",
    )(cos_h, sin_h, xc, xs, ft_norm_w)


def _dft_tables(length, n=None):
    m = jnp.arange(length if n is None else n, dtype=jnp.int32)
    ph = (m[:, None] * m[None, :]) % length
    ang = ph.astype(F32) * (2.0 * math.pi / length)
    return jnp.cos(ang).astype(BF16), jnp.sin(ang).astype(BF16)


def _store_token_tiles(ref, val):
    rows, d = val.shape
    n_lt = d // LANES
    for j in range(n_lt):
        ref[pl.ds(j, rows, stride=n_lt), :] = val[:, j * LANES:(j + 1) * LANES]


def _load_token_tiles(ref, rows, n_lt):
    return jnp.concatenate([ref[pl.ds(j, rows, stride=n_lt), :] for j in range(n_lt)], axis=1)


def _out_proj_kernel(n_lat_tiles, n_lat_tiles_per_b, n_b, x_ref, dnl_ref, dnc_ref, ftl_ref, ftc_ref,
                     wdn_ref, wft_ref, mod_ref, nw_ref, rwt_ref, x1_ref, h2_ref, lg_ref):
    t = pl.program_id(0)
    bi = jnp.minimum(t // n_lat_tiles_per_b, n_b)
    d = x_ref.shape[1]
    is_lat = t < n_lat_tiles
    dn = jnp.where(is_lat, dnl_ref[...], dnc_ref[...])
    ft = jnp.where(is_lat, ftl_ref[...], ftc_ref[...])
    mix = (jnp.dot(dn, wdn_ref[0], preferred_element_type=F32)
           + jnp.dot(ft, wft_ref[0], preferred_element_type=F32))
    g1 = mod_ref[0, pl.ds(bi, 1), 2 * d:3 * d]
    sh2 = mod_ref[0, pl.ds(bi, 1), 3 * d:4 * d]
    sc2 = mod_ref[0, pl.ds(bi, 1), 4 * d:5 * d]
    x1 = x_ref[...] + g1 * mix
    x1_ref[...] = x1
    y = x1 * lax.rsqrt(jnp.mean(x1 * x1, axis=-1, keepdims=True) + EPS) * nw_ref[...]
    h2 = y * (1.0 + sc2) + sh2
    _store_token_tiles(h2_ref, h2)
    lg_ref[...] = lax.dot_general(rwt_ref[...], h2, (((1,), (1,)), ((), ())),
                                  preferred_element_type=F32, precision=lax.Precision.HIGHEST)


def _out_proj(xs, dn_l, dn_c, ft_l, ft_c, w_out, mod_i, layer, norm_w, router_wt, n_b, seq, t_rows):
    d = xs.shape[1]
    tm = 512
    n_lt = d // LANES
    n_lat_tiles = n_b * seq // tm
    kern = functools.partial(_out_proj_kernel, n_lat_tiles, seq // tm, n_b)
    const = lambda *shape: pl.BlockSpec(shape, lambda t: tuple(0 for _ in shape))
    rows = lambda w: pl.BlockSpec((tm, w), lambda t: (t, 0))
    lat = lambda w: pl.BlockSpec((tm, w), lambda t: (jnp.minimum(t, n_lat_tiles - 1), 0))
    ctx = lambda w: pl.BlockSpec((tm, w), lambda t: (jnp.maximum(t - n_lat_tiles, 0), 0))
    return pl.pallas_call(
        kern,
        out_shape=(jax.ShapeDtypeStruct((t_rows, d), F32),
                   jax.ShapeDtypeStruct((t_rows * n_lt, LANES), F32),
                   jax.ShapeDtypeStruct((N_EXPERTS, t_rows), F32)),
        grid=(t_rows // tm,),
        in_specs=[rows(d), lat(DN_WIDTH), ctx(DN_WIDTH), lat(FT_WIDTH), ctx(FT_WIDTH),
                  pl.BlockSpec((1, DN_WIDTH, d), lambda t: (layer, 0, 0)),
                  pl.BlockSpec((1, FT_WIDTH, d), lambda t: (layer, DN_WIDTH // FT_WIDTH, 0)),
                  pl.BlockSpec((1,) + mod_i.shape[1:], lambda t: (layer, 0, 0)),
                  const(1, d), const(N_EXPERTS, d)],
        out_specs=(rows(d), pl.BlockSpec((tm * n_lt, LANES), lambda t: (t, 0)),
                   pl.BlockSpec((N_EXPERTS, tm), lambda t: (0, t))),
        compiler_params=_params(("parallel",)),
        name="out_proj",
    )(xs, dn_l, dn_c, ft_l, ft_c, w_out, w_out, mod_i, norm_w, router_wt)


def _route_kernel(tr, lg_ref, bias_ref, idx_ref, wt_ref, cnt_ref, upper, carry):
    step = pl.program_id(0)

    @pl.when(step == 0)
    def _():
        a = lax.broadcasted_iota(jnp.int32, (tr, tr), 0)
        b = lax.broadcasted_iota(jnp.int32, (tr, tr), 1)
        upper[...] = jnp.where(a < b, 1.0, 0.0).astype(BF16)
        carry[...] = jnp.zeros_like(carry)

    scores = _sigmoid(lg_ref[...])
    biased = scores + bias_ref[...]
    rows = [biased[r:r + 1, :] for r in range(N_EXPERTS)]
    srow = [scores[r:r + 1, :] for r in range(N_EXPERTS)]
    epg = EXPERTS_PER_GROUP

    def group_score(g):
        best = None
        for i in range(epg):
            for j in range(i + 1, epg):
                pair = rows[g * epg + i] + rows[g * epg + j]
                best = pair if best is None else jnp.maximum(best, pair)
        return best

    best_g = jnp.zeros((1, tr), jnp.int32)
    best_v = group_score(0)
    for g in range(1, N_EXPERT_GROUPS):
        gs = group_score(g)
        take = gs > best_v
        best_g = jnp.where(take, g, best_g)
        best_v = jnp.where(take, gs, best_v)

    def pick(table, r):
        out = table[r]
        for g in range(1, N_EXPERT_GROUPS):
            out = jnp.where(best_g == g, table[g * epg + r], out)
        return out

    in_b = [pick(rows, r) for r in range(epg)]
    in_s = [pick(srow, r) for r in range(epg)]
    l1 = jnp.zeros((1, tr), jnp.int32)
    m1 = in_b[0]
    for r in range(1, epg):
        take = in_b[r] > m1
        l1 = jnp.where(take, r, l1)
        m1 = jnp.where(take, in_b[r], m1)
    l2 = jnp.full((1, tr), -1, jnp.int32)
    m2 = jnp.full((1, tr), -jnp.inf, F32)
    for r in range(epg):
        take = jnp.logical_and(l1 != r, jnp.logical_or(l2 < 0, in_b[r] > m2))
        l2 = jnp.where(take, r, l2)
        m2 = jnp.where(take, in_b[r], m2)
    s1 = in_s[0]
    s2 = in_s[0]
    for r in range(1, epg):
        s1 = jnp.where(l1 == r, in_s[r], s1)
        s2 = jnp.where(l2 == r, in_s[r], s2)
    e1 = best_g * epg + l1
    e2 = best_g * epg + l2
    tot = s1 + s2
    wt_ref[0:1, :] = s1 / tot
    wt_ref[1:2, :] = s2 / tot

    eid = lax.broadcasted_iota(jnp.int32, (N_EXPERTS, tr), 0)
    is1 = eid == e1
    is2 = eid == e2
    memb = jnp.where(jnp.logical_or(is1, is2), 1.0, 0.0)
    prefix = jnp.dot(memb.astype(BF16), upper[...], preferred_element_type=F32) + carry[...]
    rank1 = jnp.sum(jnp.where(is1, prefix, 0.0), axis=0, keepdims=True)
    rank2 = jnp.sum(jnp.where(is2, prefix, 0.0), axis=0, keepdims=True)
    idx_ref[0:1, :] = e1
    idx_ref[1:2, :] = e2
    idx_ref[2:3, :] = rank1.astype(jnp.int32)
    idx_ref[3:4, :] = rank2.astype(jnp.int32)
    new_carry = carry[...] + jnp.sum(memb, axis=1, keepdims=True)
    carry[...] = new_carry
    cnt_ref[...] = new_carry.astype(jnp.int32)


def _route(logits_t, router_bias):
    n_e, t_rows = logits_t.shape
    tr = 512
    return pl.pallas_call(
        functools.partial(_route_kernel, tr),
        out_shape=(jax.ShapeDtypeStruct((4, t_rows), jnp.int32),
                   jax.ShapeDtypeStruct((2, t_rows), F32),
                   jax.ShapeDtypeStruct((n_e, 1), jnp.int32)),
        grid=(t_rows // tr,),
        in_specs=[pl.BlockSpec((n_e, tr), lambda t: (0, t)),
                  pl.BlockSpec((n_e, 1), lambda t: (0, 0))],
        out_specs=(pl.BlockSpec((4, tr), lambda t: (0, t)),
                   pl.BlockSpec((2, tr), lambda t: (0, t)),
                   pl.BlockSpec((n_e, 1), lambda t: (0, 0))),
        scratch_shapes=[pltpu.VMEM((tr, tr), BF16), pltpu.VMEM((n_e, 1), F32)],
        compiler_params=_params(("arbitrary",)),
        name="route",
    )(logits_t, router_bias)


def _row_copy(src, src_row, dst, dst_row, sem):
    s0 = pl.multiple_of(src_row * SUBLANES, SUBLANES)
    d0 = pl.multiple_of(dst_row * SUBLANES, SUBLANES)
    return pltpu.make_async_copy(src.at[pl.ds(s0, SUBLANES), :], dst.at[pl.ds(d0, SUBLANES), :], sem)


def _dispatch_kernel(tm, t_rows, dest_ref, lo_ref, hi_ref, h_ref, xs_hbm, sem, pad_sem):
    i = pl.program_id(0)
    base = i * tm

    @pl.when(i == 0)
    def _():
        def per_expert(e, carry):
            lo, hi = lo_ref[e], hi_ref[e]

            def start(s, c):
                _row_copy(h_ref, 0, xs_hbm, s, pad_sem).start()
                return c

            def wait(s, c):
                _row_copy(h_ref, 0, xs_hbm, lo, pad_sem).wait()
                return c

            lax.fori_loop(lo, hi, start, 0)
            lax.fori_loop(lo, hi, wait, 0)
            return carry

        lax.fori_loop(0, N_EXPERTS, per_expert, 0)

    def body(r, carry):
        _row_copy(h_ref, r, xs_hbm, dest_ref[base + r], sem).start(priority=0)
        _row_copy(h_ref, r, xs_hbm, dest_ref[t_rows + base + r], sem).start(priority=1)
        return carry

    lax.fori_loop(0, tm, body, 0, unroll=8)
    whole = pltpu.make_async_copy(h_ref, xs_hbm.at[pl.ds(0, tm * SUBLANES), :], sem)
    whole.wait()
    whole.wait()


def _dispatch(dest, pad_lo, pad_hi, h2, t_rows, p_rows):
    tm = 512
    grid_spec = pltpu.PrefetchScalarGridSpec(
        num_scalar_prefetch=3,
        grid=(t_rows // tm,),
        in_specs=[pl.BlockSpec((tm * SUBLANES, LANES), lambda i, *_: (i, 0))],
        out_specs=pl.BlockSpec(memory_space=pl.ANY),
        scratch_shapes=[pltpu.SemaphoreType.DMA, pltpu.SemaphoreType.DMA],
    )
    return pl.pallas_call(
        functools.partial(_dispatch_kernel, tm, t_rows),
        out_shape=jax.ShapeDtypeStruct((p_rows * SUBLANES, LANES), F32),
        grid_spec=grid_spec,
        compiler_params=_params(("arbitrary",)),
        name="moe_dispatch",
    )(dest, pad_lo, pad_hi, h2)


def _combine_kernel(tm, t_rows, n_lat_tiles_per_b, n_b, final, dest_ref, x1_ref, wt_ref, mod_ref, fw_ref,
                    ys_hbm, o_ref, buf, sem):
    i = pl.program_id(0)
    n_steps = pl.num_programs(0)
    d = x1_ref.shape[1]
    n_lt = d // LANES

    def issue(step, slot):
        base = step * tm

        def body(r, carry):
            _row_copy(ys_hbm, dest_ref[base + r], buf.at[slot, 0], r, sem.at[slot]).start(priority=0)
            _row_copy(ys_hbm, dest_ref[t_rows + base + r], buf.at[slot, 1], r, sem.at[slot]).start(priority=1)
            return carry

        lax.fori_loop(0, tm, body, 0, unroll=8)

    @pl.when(i == 0)
    def _():
        issue(0, 0)

    @pl.when(i + 1 < n_steps)
    def _():
        issue(i + 1, (i + 1) % 2)

    slot = i % 2
    for k in range(2):
        pltpu.make_async_copy(ys_hbm.at[pl.ds(0, tm * n_lt), :], buf.at[slot, k], sem.at[slot]).wait()
    bi = jnp.minimum(i // n_lat_tiles_per_b, n_b)
    g2 = mod_ref[0, pl.ds(bi, 1), 5 * d:6 * d]
    wt = wt_ref[...]
    y = (_load_token_tiles(buf.at[slot, 0], tm, n_lt) * wt[:, 0:1]
         + _load_token_tiles(buf.at[slot, 1], tm, n_lt) * wt[:, 1:2])
    out = x1_ref[...] + g2 * y
    if final:
        out = out * lax.rsqrt(jnp.mean(out * out, axis=-1, keepdims=True) + EPS) * fw_ref[...]
    o_ref[...] = out


def _combine(dest, x1, wts_t, mod, layer, final_w, ys, n_b, seq, final):
    t_rows, d = x1.shape
    tm = 256
    n_lt = d // LANES
    grid_spec = pltpu.PrefetchScalarGridSpec(
        num_scalar_prefetch=1,
        grid=(t_rows // tm,),
        in_specs=[pl.BlockSpec((tm, d), lambda i, *_: (i, 0)),
                  pl.BlockSpec((tm, 2), lambda i, *_: (i, 0)),
                  pl.BlockSpec((1,) + mod.shape[1:], lambda i, *_: (layer, 0, 0)),
                  pl.BlockSpec((1, d), lambda i, *_: (0, 0)),
                  pl.BlockSpec(memory_space=pl.ANY)],
        out_specs=pl.BlockSpec((tm, d), lambda i, *_: (i, 0)),
        scratch_shapes=[pltpu.VMEM((2, 2, tm * n_lt, LANES), F32), pltpu.SemaphoreType.DMA((2,))],
    )
    return pl.pallas_call(
        functools.partial(_combine_kernel, tm, t_rows, seq // tm, n_b, final),
        out_shape=jax.ShapeDtypeStruct((t_rows, d), F32),
        grid_spec=grid_spec,
        compiler_params=_params(("arbitrary",)),
        name="moe_combine",
    )(dest, x1, wts_t, mod, final_w, ys)


def _ffn_kernel(n_lt, be_ref, x_ref, wg_ref, wu_ref, wd_ref, o_ref, wg_s, wu_s, wd_s):
    i = pl.program_id(0)
    prev_expert = be_ref[jnp.maximum(i - 1, 0)]

    @pl.when(jnp.logical_or(i == 0, be_ref[i] != prev_expert))
    def _():
        wg_s[...] = wg_ref[0].astype(BF16)
        wu_s[...] = wu_ref[0].astype(BF16)
        wd_s[...] = wd_ref[0].astype(BF16)

    x = _load_token_tiles(x_ref, MOE_BLOCK, n_lt).astype(BF16)
    gate = jnp.dot(x, wg_s[...], preferred_element_type=F32)
    up = jnp.dot(x, wu_s[...], preferred_element_type=F32)
    hid = (_silu(gate) * up).astype(BF16)
    _store_token_tiles(o_ref, jnp.dot(hid, wd_s[...], preferred_element_type=F32))


def _expert_ffn(blk_expert, xsorted, w_gate, w_up, w_down):
    d, de = w_gate.shape[-2:]
    n_lt = d // LANES
    n_blk = xsorted.shape[0] // (MOE_BLOCK * n_lt)
    grid_spec = pltpu.PrefetchScalarGridSpec(
        num_scalar_prefetch=1,
        grid=(n_blk,),
        in_specs=[pl.BlockSpec((MOE_BLOCK * n_lt, LANES), lambda i, be: (i, 0)),
                  pl.BlockSpec((1, d, de), lambda i, be: (be[i], 0, 0)),
                  pl.BlockSpec((1, d, de), lambda i, be: (be[i], 0, 0)),
                  pl.BlockSpec((1, de, d), lambda i, be: (be[i], 0, 0))],
        out_specs=pl.BlockSpec((MOE_BLOCK * n_lt, LANES), lambda i, be: (i, 0)),
        scratch_shapes=[pltpu.VMEM((d, de), BF16), pltpu.VMEM((d, de), BF16), pltpu.VMEM((de, d), BF16)],
    )
    return pl.pallas_call(
        functools.partial(_ffn_kernel, n_lt),
        out_shape=jax.ShapeDtypeStruct(xsorted.shape, F32),
        grid_spec=grid_spec,
        compiler_params=_params(("arbitrary",)),
        name="expert_ffn",
    )(blk_expert, xsorted, w_gate, w_up, w_down)


def _sincos_2d(length, dim):
    rows = length // GRID_W
    quarter = dim // 4
    omega = 1.0 / (POS_BASE ** (jnp.arange(quarter, dtype=F32) / quarter))
    ang_r = jnp.arange(rows, dtype=F32)[:, None] * omega
    ang_c = jnp.arange(GRID_W, dtype=F32)[:, None] * omega
    emb_r = jnp.concatenate([jnp.sin(ang_r), jnp.cos(ang_r)], axis=-1)
    emb_c = jnp.concatenate([jnp.sin(ang_c), jnp.cos(ang_c)], axis=-1)
    half = dim // 2
    emb = jnp.concatenate([jnp.broadcast_to(emb_r[:, None, :], (rows, GRID_W, half)),
                           jnp.broadcast_to(emb_c[None, :, :], (rows, GRID_W, half))], axis=-1)
    return emb.reshape(rows * GRID_W, dim)


def _moe(h2, x1, idx, wts, counts, mod, layer, final_w, w_gate, w_up, w_down, n_b, seq, final):
    t_rows, d = x1.shape
    a = 2 * t_rows
    counts = counts[:, 0]
    padded = (counts + MOE_BLOCK - 1) // MOE_BLOCK * MOE_BLOCK
    pad_end = jnp.cumsum(padded)
    pad_start = pad_end - padded
    n_blk = (a + N_EXPERTS * (MOE_BLOCK - 1) + MOE_BLOCK - 1) // MOE_BLOCK
    p_rows = n_blk * MOE_BLOCK
    blk_start = jnp.arange(n_blk, dtype=jnp.int32) * MOE_BLOCK
    blk_expert = jnp.minimum(jnp.sum(blk_start[:, None] >= pad_end[None, :], axis=1),
                             N_EXPERTS - 1).astype(jnp.int32)
    e_ids = jnp.arange(N_EXPERTS, dtype=jnp.int32)
    seg_start = jnp.sum(jnp.where(idx[0:2, :, None] == e_ids, pad_start, 0), axis=-1)
    dest = (seg_start + idx[2:4]).reshape(-1).astype(jnp.int32)
    pad_lo = (pad_start + counts).astype(jnp.int32)
    pad_hi = pad_end.at[N_EXPERTS - 1].set(p_rows).astype(jnp.int32)
    xsorted = _dispatch(dest, pad_lo, pad_hi, h2, t_rows, p_rows)
    ys = _expert_ffn(blk_expert, xsorted, w_gate, w_up, w_down)
    return _combine(dest, x1, wts.T, mod, layer, final_w, ys, n_b, seq, final)


def kernel(x, c, ctx, c_ctx, ada_w, ada_b, norm_mix_w, norm_ffn_w, w_in, conv_w, a_log, dt_bias,
           dn_norm_w, ft_norm_w, w_out, router_w, router_bias, w_gate, w_up, w_down, final_norm_w):
    n_b, seq, d = x.shape
    ctx_len = ctx.shape[1]
    depth = ada_w.shape[0]
    n_lat = n_b * seq
    n_tok = n_lat + n_b * ctx_len
    ft_off = 4 * DN_WIDTH + N_GATE_COLS

    xs = jnp.concatenate([(x + _sincos_2d(seq, d)[None]).reshape(n_lat, d),
                          ctx.reshape(n_b * ctx_len, d)], axis=0)

    mod_rows = -(-(n_b + 1) // 8) * 8
    cvec = jnp.zeros((mod_rows, d), F32).at[:n_b].set(c).at[n_b].set(c_ctx)
    mod = _ada(cvec, ada_w, ada_b)

    cos_l, sin_l = _dft_tables(seq, seq // 2)
    cos_c, sin_c = _dft_tables(ctx_len, ctx_len // 2)
    cc, sc = _dft_tables(FT_GROUP_DIM)
    dft_cs = jnp.concatenate([cc, sc], axis=1)
    assert d == SUBLANES * LANES, "token tiles assume one (8, 128) tile per token"
    final_w = final_norm_w.reshape(1, d)
    router_wt = router_w.T
    rbias = router_bias.reshape(N_EXPERTS, 1)

    col = jnp.arange(N_GATE_COLS)
    col_head, col_dir, col_ab = col // 4, (col // 2) % 2, col % 2
    gate_src = col_dir * (2 * N_DN_HEADS) + col_ab * N_DN_HEADS + col_head
    is_a = (col_ab == 0).astype(F32)
    is_bwd = (col_dir == 1).astype(F32)

    gate_w = w_in[:, :, 4 * DN_WIDTH:ft_off][:, :, gate_src]
    lane_pad = ((0, 0), (0, 0), (0, GATE_PAD - N_GATE_COLS))
    w_pack = jnp.concatenate([w_in[:, :, :4 * DN_WIDTH], w_in[:, :, ft_off:], jnp.pad(gate_w, lane_pad)],
                             axis=-1).astype(BF16)
    wabt = jnp.swapaxes(gate_w, 1, 2).astype(BF16)
    neg_a = -jnp.exp(a_log)[:, col_dir, col_head] * is_a
    dtb = dt_bias[:, col_dir, col_head] * is_a
    gpar = jnp.stack([neg_a, dtb, jnp.broadcast_to(is_a, neg_a.shape),
                      jnp.broadcast_to(is_bwd, neg_a.shape)], axis=1)
    gpar_pad = jnp.pad(gpar, lane_pad)
    gpart = jnp.swapaxes(gpar, 1, 2)
    w_out_b = w_out.astype(BF16)
    norm_mix = norm_mix_w.reshape(depth, 1, d)

    for i in range(depth):
        last = i == depth - 1
        qkv, z, xc, xsn, gcol, grow = _in_proj(xs, mod, i, norm_mix, w_pack, wabt, gpar_pad, gpart, dft_cs,
                                               n_b, seq)
        dn_l, dn_c = _deltanet(qkv, z, gcol, grow, conv_w[i], dn_norm_w[i].reshape(1, DN_HEAD_DIM),
                               n_b, seq, ctx_len)
        fnw = ft_norm_w[i].reshape(1, FT_WIDTH)
        ft_l = _fourier(xc, xsn, cos_l, sin_l, fnw, n_b, seq, 0)
        if last:
            ft_c, rows = dn_c, n_lat
        else:
            ft_c = _fourier(xc, xsn, cos_c, sin_c, fnw, n_b, ctx_len, n_lat // ctx_len)
            rows = n_tok
        x1, h2, logits_t = _out_proj(xs, dn_l, dn_c, ft_l, ft_c, w_out_b, mod, i,
                                     norm_ffn_w[i].reshape(1, d), router_wt, n_b, seq, rows)
        idx, wts, counts = _route(logits_t, rbias)
        xs = _moe(h2, x1, idx, wts, counts, mod, i, final_w, w_gate[i], w_up[i], w_down[i], n_b, seq, last)
    return xs.reshape(n_b, seq, d)
```

```python
import functools
import math

import jax
import jax.numpy as jnp
from jax import lax
from jax.experimental import pallas as pl
from jax.experimental.pallas import tpu as pltpu

F32 = jnp.float32
BF16 = jnp.bfloat16

GRID_W = 64
N_DN_HEADS = 4
DN_HEAD_DIM = 128
DN_WIDTH = N_DN_HEADS * DN_HEAD_DIM
N_FT_GROUPS = 4
FT_GROUP_DIM = 128
FT_WIDTH = N_FT_GROUPS * FT_GROUP_DIM
CONV_K = 5
CHUNK = 64
N_EXPERTS = 16
N_EXPERT_GROUPS = 4
EXPERTS_PER_GROUP = N_EXPERTS // N_EXPERT_GROUPS
D_EXPERT = 512
MOE_BLOCK = 512
POS_BASE = 10000.0
EPS = 1e-6
N_GATE_COLS = 4 * N_DN_HEADS
CONV_BLOCK = 256
DN_UNROLL = 16

LANES = 128
SUBLANES = 8
VMEM_LIMIT = 56 * 1024 * 1024


def _params(sem, vmem=VMEM_LIMIT):
    return pltpu.CompilerParams(dimension_semantics=sem, vmem_limit_bytes=vmem)


def _dot(a, b):
    return jnp.dot(a.astype(BF16), b.astype(BF16), preferred_element_type=F32)


def _dot_nt(a, b):
    return lax.dot_general(a.astype(BF16), b.astype(BF16), (((1,), (1,)), ((), ())),
                           preferred_element_type=F32)


def _sigmoid(x):
    return 1.0 / (1.0 + jnp.exp(-x))


def _silu(x):
    return x * _sigmoid(x)


def _softplus(x):
    return jnp.maximum(x, 0.0) + jnp.log(1.0 + jnp.exp(-jnp.abs(x)))


def _ada_kernel(c_ref, w_ref, b_ref, o_ref):
    act = _silu(c_ref[...])
    o_ref[0] = jnp.dot(act, w_ref[0], preferred_element_type=F32,
                       precision=lax.Precision.HIGHEST) + b_ref[0]


def _ada(cvec, ada_w, ada_b):
    depth, d, n = ada_w.shape
    rows = cvec.shape[0]
    tn = 1024
    return pl.pallas_call(
        _ada_kernel,
        out_shape=jax.ShapeDtypeStruct((depth, rows, n), F32),
        grid=(depth, n // tn),
        in_specs=[pl.BlockSpec((rows, d), lambda i, j: (0, 0)),
                  pl.BlockSpec((1, d, tn), lambda i, j: (i, 0, j)),
                  pl.BlockSpec((1, 1, tn), lambda i, j: (i, 0, j))],
        out_specs=pl.BlockSpec((1, rows, tn), lambda i, j: (i, 0, j)),
        compiler_params=_params(("parallel", "parallel")),
        name="ada_mod",
    )(cvec, ada_w, ada_b.reshape(depth, 1, n))


def _seg_scan(x, pos, axis, reverse):
    n = x.shape[axis]
    s = 1
    while s < CHUNK:
        if reverse:
            shifted = pltpu.roll(x, n - s, axis)
            x = x + jnp.where(pos < CHUNK - s, shifted, 0.0)
        else:
            shifted = pltpu.roll(x, s, axis)
            x = x + jnp.where(pos >= s, shifted, 0.0)
        s *= 2
    return x


def _in_proj_kernel(tm, n_lat_tiles_per_b, n_b, x_ref, mod_ref, nw_ref, w_ref, wabt_ref, gpart_ref,
                    dft_ref, qkv_ref, z_ref, xc_ref, xs_ref, gcol_ref, grow_ref):
    t = pl.program_id(0)
    bi = jnp.minimum(t // n_lat_tiles_per_b, n_b)
    d = x_ref.shape[1]
    x = x_ref[...]
    y = x * lax.rsqrt(jnp.mean(x * x, axis=-1, keepdims=True) + EPS) * nw_ref[0]
    shift = mod_ref[0, pl.ds(bi, 1), 0:d]
    scale = mod_ref[0, pl.ds(bi, 1), d:2 * d]
    h = (y * (1.0 + scale) + shift).astype(BF16)
    z_off, ft_off = 3 * DN_WIDTH, 4 * DN_WIDTH

    for j in range(3):
        cs = slice(j * DN_WIDTH, (j + 1) * DN_WIDTH)
        qkv_ref[:, cs] = jnp.dot(h, w_ref[0, :, cs], preferred_element_type=F32)
    z_ref[...] = jnp.dot(h, w_ref[0, :, z_off:ft_off], preferred_element_type=F32)

    ft = jnp.dot(h, w_ref[0, :, ft_off:], preferred_element_type=F32).astype(BF16)
    for g in range(N_FT_GROUPS):
        cs = slice(g * FT_GROUP_DIM, (g + 1) * FT_GROUP_DIM)
        cssn = jnp.dot(ft[:, cs], dft_ref[...], preferred_element_type=F32)
        xc_ref[:, cs] = cssn[:, :FT_GROUP_DIM].astype(BF16)
        xs_ref[:, cs] = cssn[:, FT_GROUP_DIM:].astype(BF16)

    abt = lax.dot_general(wabt_ref[0], h, (((1,), (1,)), ((), ())), preferred_element_type=F32)

    def gates(v, par, axis):
        neg_a, dtb, is_a, is_bwd = par
        g = neg_a * _softplus(v + dtb)
        pos = lax.broadcasted_iota(jnp.int32, v.shape, axis) % CHUNK
        fwd = _seg_scan(g, pos, axis, reverse=False)
        bwd = _seg_scan(g, pos, axis, reverse=True)
        cum = jnp.where(is_bwd > 0.5, bwd, fwd)
        return jnp.where(is_a > 0.5, cum, _sigmoid(v))

    gpt = gpart_ref[0]
    gt = gates(abt, (gpt[:, 0:1], gpt[:, 1:2], gpt[:, 2:3], gpt[:, 3:4]), 1)
    gc = gt.T
    for hh in range(N_DN_HEADS):
        gcol_ref[hh] = gc[:, 4 * hh:4 * hh + 4]
        for j in range(tm // CHUNK):
            cs = slice(j * CHUNK, (j + 1) * CHUNK)
            grow_ref[hh, j] = jnp.concatenate([gt[4 * hh:4 * hh + 1, cs], gt[4 * hh + 2:4 * hh + 3, cs]], axis=1)


def _in_proj(xs, mod_i, layer, norm_w, w_pack, wabt, gpart, dft_cs, n_b, seq):
    t_rows, d = xs.shape
    tm = 512
    n_tiles = t_rows // tm
    kern = functools.partial(_in_proj_kernel, tm, seq // tm, n_b)
    const = lambda *shape: pl.BlockSpec(shape, lambda t: tuple(0 for _ in shape))
    per_layer = lambda a: pl.BlockSpec((1,) + a.shape[1:], lambda t: (layer,) + (0,) * (a.ndim - 1))
    rows = lambda w: pl.BlockSpec((tm, w), lambda t: (t, 0))
    return pl.pallas_call(
        kern,
        out_shape=(jax.ShapeDtypeStruct((t_rows, 3 * DN_WIDTH), F32),
                   jax.ShapeDtypeStruct((t_rows, DN_WIDTH), F32),
                   jax.ShapeDtypeStruct((t_rows, FT_WIDTH), BF16),
                   jax.ShapeDtypeStruct((t_rows, FT_WIDTH), BF16),
                   jax.ShapeDtypeStruct((N_DN_HEADS, t_rows, 4), F32),
                   jax.ShapeDtypeStruct((N_DN_HEADS, t_rows // CHUNK, 1, 2 * CHUNK), F32)),
        grid=(n_tiles,),
        in_specs=[rows(d), per_layer(mod_i), per_layer(norm_w), per_layer(w_pack), per_layer(wabt),
                  per_layer(gpart), const(FT_GROUP_DIM, 2 * FT_GROUP_DIM)],
        out_specs=(rows(3 * DN_WIDTH), rows(DN_WIDTH), rows(FT_WIDTH), rows(FT_WIDTH),
                   pl.BlockSpec((N_DN_HEADS, tm, 4), lambda t: (0, t, 0)),
                   pl.BlockSpec((N_DN_HEADS, tm // CHUNK, 1, 2 * CHUNK), lambda t: (0, t, 0, 0))),
        compiler_params=_params(("parallel",)),
        name="in_proj",
    )(xs, mod_i, norm_w, w_pack, wabt, gpart, dft_cs)


def _conv_block(src_ref, cw, r0, n_rows, mode):
    blk = CONV_BLOCK
    half_k = CONV_K // 2
    if r0 >= half_k and r0 + blk + half_k <= n_rows:
        acc = src_ref[r0 - half_k:r0 - half_k + blk, :] * cw[0:1, :]
        for j in range(1, CONV_K):
            acc = acc + src_ref[r0 + j - half_k:r0 + j - half_k + blk, :] * cw[j:j + 1, :]
    else:
        halo = 8
        n_win = blk + 2 * halo
        zeros = jnp.zeros((halo, DN_HEAD_DIM), F32)
        prev = src_ref[r0 - halo:r0, :] if r0 > 0 else zeros
        nxt = src_ref[r0 + blk:r0 + blk + halo, :] if r0 + blk < n_rows else zeros
        win = jnp.concatenate([prev, src_ref[r0:r0 + blk, :], nxt], axis=0)
        acc = jnp.zeros((blk, DN_HEAD_DIM), F32)
        for j in range(CONV_K):
            shift = (half_k - j) % n_win
            rolled = win if shift == 0 else pltpu.roll(win, shift, 0)
            acc = acc + rolled[halo:halo + blk] * cw[j:j + 1, :]
    y = _silu(acc)
    if mode == "v":
        return y
    inv = lax.rsqrt(jnp.sum(y * y, axis=-1, keepdims=True) + EPS)
    if mode == "q":
        inv = inv * (DN_HEAD_DIM ** -0.5)
    return y * inv


def _blockdiag(x, isb):
    xb = x.astype(BF16)
    keep_b = jnp.where(isb, 1.0, 0.0).astype(BF16)
    keep_f = jnp.where(isb, 0.0, 1.0).astype(BF16)
    return jnp.concatenate([xb * keep_f, xb * keep_b], axis=0)


def _blockdiag_wide(x):
    w = x.shape[1] // 2
    zero = jnp.zeros((x.shape[0], w), BF16)
    xb = x.astype(BF16)
    return jnp.concatenate([jnp.concatenate([xb[:, :w], zero], axis=1),
                            jnp.concatenate([zero, xb[:, w:]], axis=1)], axis=0)


def _tri_inverse_dual(a_list, eye, xor, isb):
    mm = lambda x, y: jnp.dot(x.astype(BF16), _blockdiag(y, isb), preferred_element_type=F32)
    ad = [jnp.where((xor >> 3) == 0, a, 0.0) for a in a_list]
    a2 = [mm(x, x) for x in ad]
    a4 = [mm(x, x) for x in a2]
    t = [eye - x for x in ad]
    t = [x + mm(x, y) for x, y in zip(t, a2)]
    t = [x + mm(x, y) for x, y in zip(t, a4)]
    for s in (3, 4, 5):
        off = [jnp.where((xor >> s) == 1, a, 0.0) for a in a_list]
        to = [mm(x, y) for x, y in zip(t, off)]
        t = [x - mm(y, x) for x, y in zip(t, to)]
    return t


def _chunks_local(loaded):
    hd = DN_HEAD_DIM
    n = len(loaded)
    q = [x[0] for x in loaded]
    k = [x[1] for x in loaded]
    v = [x[2] for x in loaded]
    wide = lambda col: jnp.broadcast_to(col, (CHUNK, hd))
    gf = [wide(x[3][:, 0:1]) for x in loaded]
    bf_ = [wide(x[3][:, 1:2]) for x in loaded]
    gb = [wide(x[3][:, 2:3]) for x in loaded]
    bb = [wide(x[3][:, 3:4]) for x in loaded]
    grow = [x[4] for x in loaded]
    row = lax.broadcasted_iota(jnp.int32, (CHUNK, 2 * CHUNK), 0)
    lane = lax.broadcasted_iota(jnp.int32, (CHUNK, 2 * CHUNK), 1)
    jl = lane & (CHUNK - 1)
    isb = lane >= CHUNK
    delta = jnp.where(isb, jl - row, row - jl)
    xor = row ^ jl
    eye = jnp.where(delta == 0, 1.0, 0.0)
    nt = (((1,), (1,)), ((), ()))
    k2 = [jnp.concatenate([x, x], axis=0).astype(BF16) for x in k]
    kkd = [lax.dot_general(k[i].astype(BF16), k2[i], nt, preferred_element_type=F32) for i in range(n)]
    qkd = [lax.dot_general(q[i].astype(BF16), k2[i], nt, preferred_element_type=F32) for i in range(n)]
    dec = [jnp.where(delta >= 0, jnp.exp(jnp.where(isb, gb[i], gf[i]) - grow[i]), 0.0) for i in range(n)]
    a_mat = [jnp.where(delta > 0, kkd[i] * jnp.where(isb, bb[i], bf_[i]) * dec[i], 0.0) for i in range(n)]
    t_inv = _tri_inverse_dual(a_mat, eye, xor, isb)
    egf = [jnp.exp(x) for x in gf]
    egb = [jnp.exp(x) for x in gb]
    rhs = [jnp.concatenate([v[i] * bf_[i], k[i] * (bf_[i] * egf[i]), v[i] * bb[i], k[i] * (bb[i] * egb[i])],
                           axis=1) for i in range(n)]
    sol = [jnp.dot(t_inv[i].astype(BF16), _blockdiag_wide(rhs[i]), preferred_element_type=F32)
           for i in range(n)]
    bd_sol = [_blockdiag_wide(x) for x in sol]
    r1 = [jnp.dot((qkd[i] * dec[i]).astype(BF16), bd_sol[i], preferred_element_type=F32)
          for i in range(n)]
    glf = [x[CHUNK - 1:CHUNK, :] for x in gf]
    glb = [x[0:1, :] for x in gb]
    kdec = [jnp.concatenate([k[i] * jnp.exp(glf[i] - gf[i]), k[i] * jnp.exp(glb[i] - gb[i])], axis=0)
            for i in range(n)]
    r2 = [jnp.dot(kdec[i].T.astype(BF16), bd_sol[i], preferred_element_type=F32)
          for i in range(n)]
    out = []
    for i in range(n):
        o_loc = r1[i][:, 0:hd] + r1[i][:, 2 * hd:3 * hd]
        qt = jnp.concatenate([q[i] * egf[i] - r1[i][:, hd:2 * hd], q[i] * egb[i] - r1[i][:, 3 * hd:]],
                             axis=1).astype(BF16)
        nn = jnp.concatenate([r2[i][:, 0:hd], r2[i][:, 2 * hd:3 * hd]], axis=1)
        kw = jnp.concatenate([r2[i][:, hd:2 * hd], r2[i][:, 3 * hd:]], axis=1).astype(BF16)
        ge = jnp.concatenate([jnp.exp(glf[i]), jnp.exp(glb[i])], axis=0)
        out.append((o_loc, qt, nn, kw, ge))
    return out


def _state_step(c, d, oacc, qt_ref, kw_ref, nn_ref, ge_ref, s_ref):
    hd = DN_HEAD_DIM
    r0 = pl.multiple_of(c * CHUNK, CHUNK)
    cs = slice(d * hd, (d + 1) * hd)
    s = s_ref[d]
    lhs = jnp.concatenate([qt_ref[c, :, cs], kw_ref[c, :, cs]], axis=0)
    r = jnp.dot(lhs, s.astype(BF16), preferred_element_type=F32)
    oacc[pl.ds(r0, CHUNK), :] += r[:CHUNK]
    s_ref[d] = s * ge_ref[c, d:d + 1, :] + nn_ref[c, :, cs] - r[CHUNK:]


def _dn_kernel(seq, ctx_len,
               ql_ref, kl_ref, vl_ref, qc_ref, kc_ref, vc_ref, cwq_ref, cwk_ref, cwv_ref,
               zl_ref, zc_ref, gcl_ref, gcc_ref, grl_ref, grc_ref, nw_ref,
               ol_ref, oc_ref,
               qn, kn, vn, oacc, qt_s, kw_s, nn_s, ge_s, s_s):
    nw = nw_ref[...]
    s_s[...] = jnp.zeros_like(s_s)

    def segment(n_rows, q_ref, k_ref, v_ref, z_ref, gcol_ref, grow_ref, out_ref):
        n_chunks = n_rows // CHUNK
        unroll = math.gcd(DN_UNROLL, n_chunks)
        g_rows = unroll * CHUNK
        n_groups = n_chunks // unroll

        def prep(g, slot):
            for b0 in range(0, g_rows, CONV_BLOCK):
                r0 = g * g_rows + b0
                for src, cwr, dst, mode in ((q_ref, cwq_ref, qn, "q"), (k_ref, cwk_ref, kn, "k"),
                                            (v_ref, cwv_ref, vn, "v")):
                    dst[slot, b0:b0 + CONV_BLOCK, :] = _conv_block(src, cwr[...], r0, n_rows, mode)

        prep(0, 0)
        for g in range(n_groups):
            slot = g % 2
            chunks = [g * unroll + j for j in range(unroll)]
            tile = lambda ref, j: ref[slot, j * CHUNK:(j + 1) * CHUNK, :]
            loaded = [(tile(qn, j), tile(kn, j), tile(vn, j),
                       gcol_ref[0, c * CHUNK:(c + 1) * CHUNK, :], grow_ref[0, c])
                      for j, c in enumerate(chunks)]
            if g + 1 < n_groups:
                prep(g + 1, 1 - slot)
            results = _chunks_local(loaded)
            for c, (o_loc, qt, nn, kw, ge) in zip(chunks, results):
                oacc[c * CHUNK:(c + 1) * CHUNK, :] = o_loc
                qt_s[c] = qt
                nn_s[c] = nn
                kw_s[c] = kw
                ge_s[c] = ge

        def step(s, carry):
            _state_step(s, 0, oacc, qt_s, kw_s, nn_s, ge_s, s_s)
            _state_step(n_chunks - 1 - s, 1, oacc, qt_s, kw_s, nn_s, ge_s, s_s)
            return carry

        lax.fori_loop(0, n_chunks, step, 0)

        blk = 256

        def fin(i, carry):
            r0 = pl.multiple_of(i * blk, blk)
            o = oacc[pl.ds(r0, blk), :]
            o = o * lax.rsqrt(jnp.mean(o * o, axis=-1, keepdims=True) + EPS) * nw
            out_ref[pl.ds(r0, blk), :] = (o * _silu(z_ref[pl.ds(r0, blk), :])).astype(out_ref.dtype)
            return carry

        lax.fori_loop(0, n_rows // blk, fin, 0)

    segment(ctx_len, qc_ref, kc_ref, vc_ref, zc_ref, gcc_ref, grc_ref, oc_ref)
    segment(seq, ql_ref, kl_ref, vl_ref, zl_ref, gcl_ref, grl_ref, ol_ref)


def _deltanet(qkv, z, gcol, grow, conv_w, dn_norm_w, n_b, seq, ctx_len):
    hd = DN_HEAD_DIM
    nc = seq // CHUNK
    cb = n_b * seq // ctx_len
    lat = lambda off: pl.BlockSpec((seq, hd), lambda b, h: (b, h + off))
    ctx = lambda off: pl.BlockSpec((ctx_len, hd), lambda b, h: (cb + b, h + off))
    cws = lambda off: pl.BlockSpec((CONV_K, hd), lambda b, h: (0, h + off))
    kern = functools.partial(_dn_kernel, seq, ctx_len)
    return pl.pallas_call(
        kern,
        out_shape=(jax.ShapeDtypeStruct((n_b * seq, DN_WIDTH), BF16),
                   jax.ShapeDtypeStruct((n_b * ctx_len, DN_WIDTH), BF16)),
        grid=(n_b, N_DN_HEADS),
        in_specs=[lat(0), lat(N_DN_HEADS), lat(2 * N_DN_HEADS),
                  ctx(0), ctx(N_DN_HEADS), ctx(2 * N_DN_HEADS),
                  cws(0), cws(N_DN_HEADS), cws(2 * N_DN_HEADS),
                  lat(0), ctx(0),
                  pl.BlockSpec((1, seq, 4), lambda b, h: (h, b, 0)),
                  pl.BlockSpec((1, ctx_len, 4), lambda b, h: (h, cb + b, 0)),
                  pl.BlockSpec((1, nc, 1, 2 * CHUNK), lambda b, h: (h, b, 0, 0)),
                  pl.BlockSpec((1, ctx_len // CHUNK, 1, 2 * CHUNK), lambda b, h: (h, cb + b, 0, 0)),
                  pl.BlockSpec((1, hd), lambda b, h: (0, 0))],
        out_specs=(pl.BlockSpec((seq, hd), lambda b, h: (b, h)),
                   pl.BlockSpec((ctx_len, hd), lambda b, h: (b, h))),
        scratch_shapes=[pltpu.VMEM((2, min(DN_UNROLL, nc) * CHUNK, hd), F32) for _ in range(3)]
        + [pltpu.VMEM((seq, hd), F32),
           pltpu.VMEM((nc, CHUNK, 2 * hd), BF16), pltpu.VMEM((nc, hd, 2 * hd), BF16),
           pltpu.VMEM((nc, hd, 2 * hd), F32), pltpu.VMEM((nc, 2, hd), F32),
           pltpu.VMEM((2, hd, hd), F32)],
        compiler_params=_params(("parallel", "parallel")),
        name="deltanet",
    )(qkv, qkv, qkv, qkv, qkv, qkv, conv_w, conv_w, conv_w, z, z, gcol, gcol, grow, grow, dn_norm_w)


def _reversed_rows(ref, first_row, n_rows, blk, flip_mat):
    n_blk = n_rows // blk
    parts = [jnp.dot(flip_mat, ref[pl.ds(first_row + (n_blk - 1 - i) * blk, blk), :],
                     preferred_element_type=F32) for i in range(n_blk)]
    return jnp.concatenate(parts, axis=0)


def _ft_kernel(tm, scale, cl_ref, sl_ref, xc_ref, xs_ref, nw_ref, o_ref, xcf, xsf, hi_s):
    mt = pl.program_id(1)
    n_m = pl.num_programs(1)
    half = xcf.shape[0]
    ii = lax.broadcasted_iota(jnp.int32, (tm, tm), 0)
    jj = lax.broadcasted_iota(jnp.int32, (tm, tm), 1)
    flip_mat = jnp.where(ii + jj == tm - 1, 1.0, 0.0).astype(BF16)
    row_h = lax.broadcasted_iota(jnp.int32, (half, 1), 0)

    def shifted_reverse(ref, first_row):
        rev = _reversed_rows(ref, first_row, half, tm, flip_mat)
        return jnp.where(row_h == 0, 0.0, pltpu.roll(rev, 1, 0))

    @pl.when(mt == 0)
    def _():
        xcf[...] = (xc_ref[0:half, :].astype(F32) + shifted_reverse(xc_ref, half)).astype(BF16)
        xsf[...] = (xs_ref[0:half, :].astype(F32) - shifted_reverse(xs_ref, half)).astype(BF16)

    a = jnp.dot(cl_ref[...], xcf[...], preferred_element_type=F32)
    b = jnp.dot(sl_ref[...], xsf[...], preferred_element_type=F32)
    x_nyq = xc_ref[half:half + 2 * SUBLANES, :].astype(F32)[0:1, :]
    m_idx = mt * tm + lax.broadcasted_iota(jnp.int32, (tm, 1), 0)
    corr = jnp.where((m_idx & 1) == 0, 1.0, -1.0) * x_nyq
    nw = nw_ref[...]

    def norm(y):
        y = y * scale
        return y * lax.rsqrt(jnp.mean(y * y, axis=-1, keepdims=True) + EPS) * nw

    r0 = pl.multiple_of(mt * tm, tm)
    o_ref[pl.ds(r0, tm), :] = norm(a - b + corr).astype(o_ref.dtype)
    hi_s[pl.ds(r0, tm), :] = norm(a + b + corr).astype(hi_s.dtype)

    @pl.when(mt == n_m - 1)
    def _():
        lane = lax.broadcasted_iota(jnp.int32, (SUBLANES, half), 1)
        alt = jnp.where((lane & 1) == 0, 1.0, -1.0).astype(BF16)
        y_nyq = norm(jnp.dot(alt, xcf[...], preferred_element_type=F32)[0:1, :] + x_nyq)
        upper = jnp.where(row_h == 0, y_nyq, pltpu.roll(_reversed_rows(hi_s, 0, half, tm, flip_mat), 1, 0))
        o_ref[half:, :] = upper.astype(o_ref.dtype)


def _fourier(xc, xs, cos_h, sin_h, ft_norm_w, n_b, length, row_block0):
    half = length // 2
    tm = min(512, half)
    n_m = half // tm
    scale = 1.0 / math.sqrt(length * FT_GROUP_DIM)
    return pl.pallas_call(
        functools.partial(_ft_kernel, tm, scale),
        out_shape=jax.ShapeDtypeStruct((n_b * length, FT_WIDTH), BF16),
        grid=(n_b, n_m),
        in_specs=[pl.BlockSpec((tm, half), lambda b, m: (m, 0)),
                  pl.BlockSpec((tm, half), lambda b, m: (m, 0)),
                  pl.BlockSpec((length, FT_WIDTH), lambda b, m: (row_block0 + b, 0)),
                  pl.BlockSpec((length, FT_WIDTH), lambda b, m: (row_block0 + b, 0)),
                  pl.BlockSpec((1, FT_WIDTH), lambda b, m: (0, 0))],
        out_specs=pl.BlockSpec((length, FT_WIDTH), lambda b, m: (b, 0)),
        scratch_shapes=[pltpu.VMEM((half, FT_WIDTH), BF16), pltpu.VMEM((half, FT_WIDTH), BF16),
                        pltpu.VMEM((half, FT_WIDTH), BF16)],
        compiler_params=_params(("parallel", "arbitrary")),
        name="fourier_seq",
    )(cos_h, sin_h, xc, xs, ft_norm_w)


def _dft_tables(length, n=None):
    m = jnp.arange(length if n is None else n, dtype=jnp.int32)
    ph = (m[:, None] * m[None, :]) % length
    ang = ph.astype(F32) * (2.0 * math.pi / length)
    return jnp.cos(ang).astype(BF16), jnp.sin(ang).astype(BF16)


def _store_token_tiles(ref, val):
    rows, d = val.shape
    n_lt = d // LANES
    for j in range(n_lt):
        ref[pl.ds(j, rows, stride=n_lt), :] = val[:, j * LANES:(j + 1) * LANES]


def _load_token_tiles(ref, rows, n_lt):
    return jnp.concatenate([ref[pl.ds(j, rows, stride=n_lt), :] for j in range(n_lt)], axis=1)


def _out_proj_kernel(n_lat_tiles, n_lat_tiles_per_b, n_b, x_ref, dnl_ref, dnc_ref, ftl_ref, ftc_ref,
                     wdn_ref, wft_ref, mod_ref, nw_ref, rwt_ref, x1_ref, h2_ref, lg_ref):
    t = pl.program_id(0)
    bi = jnp.minimum(t // n_lat_tiles_per_b, n_b)
    d = x_ref.shape[1]
    is_lat = t < n_lat_tiles
    dn = jnp.where(is_lat, dnl_ref[...], dnc_ref[...])
    ft = jnp.where(is_lat, ftl_ref[...], ftc_ref[...])
    mix = (jnp.dot(dn, wdn_ref[0], preferred_element_type=F32)
           + jnp.dot(ft, wft_ref[0], preferred_element_type=F32))
    g1 = mod_ref[0, pl.ds(bi, 1), 2 * d:3 * d]
    sh2 = mod_ref[0, pl.ds(bi, 1), 3 * d:4 * d]
    sc2 = mod_ref[0, pl.ds(bi, 1), 4 * d:5 * d]
    x1 = x_ref[...] + g1 * mix
    x1_ref[...] = x1
    y = x1 * lax.rsqrt(jnp.mean(x1 * x1, axis=-1, keepdims=True) + EPS) * nw_ref[...]
    h2 = y * (1.0 + sc2) + sh2
    _store_token_tiles(h2_ref, h2)
    lg_ref[...] = lax.dot_general(rwt_ref[...], h2, (((1,), (1,)), ((), ())),
                                  preferred_element_type=F32, precision=lax.Precision.HIGHEST)


def _out_proj(xs, dn_l, dn_c, ft_l, ft_c, w_out, mod_i, layer, norm_w, router_wt, n_b, seq, t_rows):
    d = xs.shape[1]
    tm = 512
    n_lt = d // LANES
    n_lat_tiles = n_b * seq // tm
    kern = functools.partial(_out_proj_kernel, n_lat_tiles, seq // tm, n_b)
    const = lambda *shape: pl.BlockSpec(shape, lambda t: tuple(0 for _ in shape))
    rows = lambda w: pl.BlockSpec((tm, w), lambda t: (t, 0))
    lat = lambda w: pl.BlockSpec((tm, w), lambda t: (jnp.minimum(t, n_lat_tiles - 1), 0))
    ctx = lambda w: pl.BlockSpec((tm, w), lambda t: (jnp.maximum(t - n_lat_tiles, 0), 0))
    return pl.pallas_call(
        kern,
        out_shape=(jax.ShapeDtypeStruct((t_rows, d), F32),
                   jax.ShapeDtypeStruct((t_rows * n_lt, LANES), F32),
                   jax.ShapeDtypeStruct((N_EXPERTS, t_rows), F32)),
        grid=(t_rows // tm,),
        in_specs=[rows(d), lat(DN_WIDTH), ctx(DN_WIDTH), lat(FT_WIDTH), ctx(FT_WIDTH),
                  pl.BlockSpec((1, DN_WIDTH, d), lambda t: (layer, 0, 0)),
                  pl.BlockSpec((1, FT_WIDTH, d), lambda t: (layer, DN_WIDTH // FT_WIDTH, 0)),
                  pl.BlockSpec((1,) + mod_i.shape[1:], lambda t: (layer, 0, 0)),
                  const(1, d), const(N_EXPERTS, d)],
        out_specs=(rows(d), pl.BlockSpec((tm * n_lt, LANES), lambda t: (t, 0)),
                   pl.BlockSpec((N_EXPERTS, tm), lambda t: (0, t))),
        compiler_params=_params(("parallel",)),
        name="out_proj",
    )(xs, dn_l, dn_c, ft_l, ft_c, w_out, w_out, mod_i, norm_w, router_wt)


def _route_kernel(tr, lg_ref, bias_ref, idx_ref, wt_ref, cnt_ref, upper, carry):
    step = pl.program_id(0)

    @pl.when(step == 0)
    def _():
        a = lax.broadcasted_iota(jnp.int32, (tr, tr), 0)
        b = lax.broadcasted_iota(jnp.int32, (tr, tr), 1)
        upper[...] = jnp.where(a < b, 1.0, 0.0).astype(BF16)
        carry[...] = jnp.zeros_like(carry)

    scores = _sigmoid(lg_ref[...])
    biased = scores + bias_ref[...]
    rows = [biased[r:r + 1, :] for r in range(N_EXPERTS)]
    srow = [scores[r:r + 1, :] for r in range(N_EXPERTS)]
    epg = EXPERTS_PER_GROUP

    def group_score(g):
        best = None
        for i in range(epg):
            for j in range(i + 1, epg):
                pair = rows[g * epg + i] + rows[g * epg + j]
                best = pair if best is None else jnp.maximum(best, pair)
        return best

    best_g = jnp.zeros((1, tr), jnp.int32)
    best_v = group_score(0)
    for g in range(1, N_EXPERT_GROUPS):
        gs = group_score(g)
        take = gs > best_v
        best_g = jnp.where(take, g, best_g)
        best_v = jnp.where(take, gs, best_v)

    def pick(table, r):
        out = table[r]
        for g in range(1, N_EXPERT_GROUPS):
            out = jnp.where(best_g == g, table[g * epg + r], out)
        return out

    in_b = [pick(rows, r) for r in range(epg)]
    in_s = [pick(srow, r) for r in range(epg)]
    l1 = jnp.zeros((1, tr), jnp.int32)
    m1 = in_b[0]
    for r in range(1, epg):
        take = in_b[r] > m1
        l1 = jnp.where(take, r, l1)
        m1 = jnp.where(take, in_b[r], m1)
    l2 = jnp.full((1, tr), -1, jnp.int32)
    m2 = jnp.full((1, tr), -jnp.inf, F32)
    for r in range(epg):
        take = jnp.logical_and(l1 != r, jnp.logical_or(l2 < 0, in_b[r] > m2))
        l2 = jnp.where(take, r, l2)
        m2 = jnp.where(take, in_b[r], m2)
    s1 = in_s[0]
    s2 = in_s[0]
    for r in range(1, epg):
        s1 = jnp.where(l1 == r, in_s[r], s1)
        s2 = jnp.where(l2 == r, in_s[r], s2)
    e1 = best_g * epg + l1
    e2 = best_g * epg + l2
    tot = s1 + s2
    wt_ref[0:1, :] = s1 / tot
    wt_ref[1:2, :] = s2 / tot

    eid = lax.broadcasted_iota(jnp.int32, (N_EXPERTS, tr), 0)
    is1 = eid == e1
    is2 = eid == e2
    memb = jnp.where(jnp.logical_or(is1, is2), 1.0, 0.0)
    prefix = jnp.dot(memb.astype(BF16), upper[...], preferred_element_type=F32) + carry[...]
    rank1 = jnp.sum(jnp.where(is1, prefix, 0.0), axis=0, keepdims=True)
    rank2 = jnp.sum(jnp.where(is2, prefix, 0.0), axis=0, keepdims=True)
    idx_ref[0:1, :] = e1
    idx_ref[1:2, :] = e2
    idx_ref[2:3, :] = rank1.astype(jnp.int32)
    idx_ref[3:4, :] = rank2.astype(jnp.int32)
    new_carry = carry[...] + jnp.sum(memb, axis=1, keepdims=True)
    carry[...] = new_carry
    cnt_ref[...] = new_carry.astype(jnp.int32)


def _route(logits_t, router_bias):
    n_e, t_rows = logits_t.shape
    tr = 512
    return pl.pallas_call(
        functools.partial(_route_kernel, tr),
        out_shape=(jax.ShapeDtypeStruct((4, t_rows), jnp.int32),
                   jax.ShapeDtypeStruct((2, t_rows), F32),
                   jax.ShapeDtypeStruct((n_e, 1), jnp.int32)),
        grid=(t_rows // tr,),
        in_specs=[pl.BlockSpec((n_e, tr), lambda t: (0, t)),
                  pl.BlockSpec((n_e, 1), lambda t: (0, 0))],
        out_specs=(pl.BlockSpec((4, tr), lambda t: (0, t)),
                   pl.BlockSpec((2, tr), lambda t: (0, t)),
                   pl.BlockSpec((n_e, 1), lambda t: (0, 0))),
        scratch_shapes=[pltpu.VMEM((tr, tr), BF16), pltpu.VMEM((n_e, 1), F32)],
        compiler_params=_params(("arbitrary",)),
        name="route",
    )(logits_t, router_bias)


def _row_copy(src, src_row, dst, dst_row, sem):
    s0 = pl.multiple_of(src_row * SUBLANES, SUBLANES)
    d0 = pl.multiple_of(dst_row * SUBLANES, SUBLANES)
    return pltpu.make_async_copy(src.at[pl.ds(s0, SUBLANES), :], dst.at[pl.ds(d0, SUBLANES), :], sem)


def _dispatch_kernel(tm, t_rows, dest_ref, lo_ref, hi_ref, h_ref, xs_hbm, sem, pad_sem):
    i = pl.program_id(0)
    base = i * tm

    @pl.when(i == 0)
    def _():
        def per_expert(e, carry):
            lo, hi = lo_ref[e], hi_ref[e]

            def start(s, c):
                _row_copy(h_ref, 0, xs_hbm, s, pad_sem).start()
                return c

            def wait(s, c):
                _row_copy(h_ref, 0, xs_hbm, lo, pad_sem).wait()
                return c

            lax.fori_loop(lo, hi, start, 0)
            lax.fori_loop(lo, hi, wait, 0)
            return carry

        lax.fori_loop(0, N_EXPERTS, per_expert, 0)

    def body(r, carry):
        _row_copy(h_ref, r, xs_hbm, dest_ref[base + r], sem).start(priority=0)
        _row_copy(h_ref, r, xs_hbm, dest_ref[t_rows + base + r], sem).start(priority=1)
        return carry

    lax.fori_loop(0, tm, body, 0, unroll=8)
    whole = pltpu.make_async_copy(h_ref, xs_hbm.at[pl.ds(0, tm * SUBLANES), :], sem)
    whole.wait()
    whole.wait()


def _dispatch(dest, pad_lo, pad_hi, h2, t_rows, p_rows):
    tm = 1024
    grid_spec = pltpu.PrefetchScalarGridSpec(
        num_scalar_prefetch=3,
        grid=(t_rows // tm,),
        in_specs=[pl.BlockSpec((tm * SUBLANES, LANES), lambda i, *_: (i, 0))],
        out_specs=pl.BlockSpec(memory_space=pl.ANY),
        scratch_shapes=[pltpu.SemaphoreType.DMA, pltpu.SemaphoreType.DMA],
    )
    return pl.pallas_call(
        functools.partial(_dispatch_kernel, tm, t_rows),
        out_shape=jax.ShapeDtypeStruct((p_rows * SUBLANES, LANES), F32),
        grid_spec=grid_spec,
        compiler_params=_params(("arbitrary",)),
        name="moe_dispatch",
    )(dest, pad_lo, pad_hi, h2)


def _combine_kernel(tm, t_rows, n_lat_tiles_per_b, n_b, final, dest_ref, x1_ref, wt_ref, mod_ref, fw_ref,
                    ys_hbm, o_ref, buf, sem):
    i = pl.program_id(0)
    n_steps = pl.num_programs(0)
    d = x1_ref.shape[1]
    n_lt = d // LANES

    def issue(step, slot):
        base = step * tm

        def body(r, carry):
            _row_copy(ys_hbm, dest_ref[base + r], buf.at[slot, 0], r, sem.at[slot]).start(priority=0)
            _row_copy(ys_hbm, dest_ref[t_rows + base + r], buf.at[slot, 1], r, sem.at[slot]).start(priority=1)
            return carry

        lax.fori_loop(0, tm, body, 0, unroll=8)

    @pl.when(i == 0)
    def _():
        issue(0, 0)

    @pl.when(i + 1 < n_steps)
    def _():
        issue(i + 1, (i + 1) % 2)

    slot = i % 2
    for k in range(2):
        pltpu.make_async_copy(ys_hbm.at[pl.ds(0, tm * n_lt), :], buf.at[slot, k], sem.at[slot]).wait()
    bi = jnp.minimum(i // n_lat_tiles_per_b, n_b)
    g2 = mod_ref[0, pl.ds(bi, 1), 5 * d:6 * d]
    wt = wt_ref[...]
    y = (_load_token_tiles(buf.at[slot, 0], tm, n_lt) * wt[:, 0:1]
         + _load_token_tiles(buf.at[slot, 1], tm, n_lt) * wt[:, 1:2])
    out = x1_ref[...] + g2 * y
    if final:
        out = out * lax.rsqrt(jnp.mean(out * out, axis=-1, keepdims=True) + EPS) * fw_ref[...]
    o_ref[...] = out


def _combine(dest, x1, wts_t, mod, layer, final_w, ys, n_b, seq, final):
    t_rows, d = x1.shape
    tm = 512
    n_lt = d // LANES
    grid_spec = pltpu.PrefetchScalarGridSpec(
        num_scalar_prefetch=1,
        grid=(t_rows // tm,),
        in_specs=[pl.BlockSpec((tm, d), lambda i, *_: (i, 0)),
                  pl.BlockSpec((tm, 2), lambda i, *_: (i, 0)),
                  pl.BlockSpec((1,) + mod.shape[1:], lambda i, *_: (layer, 0, 0)),
                  pl.BlockSpec((1, d), lambda i, *_: (0, 0)),
                  pl.BlockSpec(memory_space=pl.ANY)],
        out_specs=pl.BlockSpec((tm, d), lambda i, *_: (i, 0)),
        scratch_shapes=[pltpu.VMEM((2, 2, tm * n_lt, LANES), F32), pltpu.SemaphoreType.DMA((2,))],
    )
    return pl.pallas_call(
        functools.partial(_combine_kernel, tm, t_rows, seq // tm, n_b, final),
        out_shape=jax.ShapeDtypeStruct((t_rows, d), F32),
        grid_spec=grid_spec,
        compiler_params=_params(("arbitrary",)),
        name="moe_combine",
    )(dest, x1, wts_t, mod, final_w, ys)


def _ffn_kernel(n_lt, be_ref, x_ref, wg_ref, wu_ref, wd_ref, o_ref, wg_s, wu_s, wd_s):
    i = pl.program_id(0)
    prev_expert = be_ref[jnp.maximum(i - 1, 0)]

    @pl.when(jnp.logical_or(i == 0, be_ref[i] != prev_expert))
    def _():
        wg_s[...] = wg_ref[0].astype(BF16)
        wu_s[...] = wu_ref[0].astype(BF16)
        wd_s[...] = wd_ref[0].astype(BF16)

    x = _load_token_tiles(x_ref, MOE_BLOCK, n_lt).astype(BF16)
    gate = jnp.dot(x, wg_s[...], preferred_element_type=F32)
    up = jnp.dot(x, wu_s[...], preferred_element_type=F32)
    hid = (_silu(gate) * up).astype(BF16)
    _store_token_tiles(o_ref, jnp.dot(hid, wd_s[...], preferred_element_type=F32))


def _expert_ffn(blk_expert, xsorted, w_gate, w_up, w_down):
    d, de = w_gate.shape[-2:]
    n_lt = d // LANES
    n_blk = xsorted.shape[0] // (MOE_BLOCK * n_lt)
    grid_spec = pltpu.PrefetchScalarGridSpec(
        num_scalar_prefetch=1,
        grid=(n_blk,),
        in_specs=[pl.BlockSpec((MOE_BLOCK * n_lt, LANES), lambda i, be: (i, 0)),
                  pl.BlockSpec((1, d, de), lambda i, be: (be[i], 0, 0)),
                  pl.BlockSpec((1, d, de), lambda i, be: (be[i], 0, 0)),
                  pl.BlockSpec((1, de, d), lambda i, be: (be[i], 0, 0))],
        out_specs=pl.BlockSpec((MOE_BLOCK * n_lt, LANES), lambda i, be: (i, 0)),
        scratch_shapes=[pltpu.VMEM((d, de), BF16), pltpu.VMEM((d, de), BF16), pltpu.VMEM((de, d), BF16)],
    )
    return pl.pallas_call(
        functools.partial(_ffn_kernel, n_lt),
        out_shape=jax.ShapeDtypeStruct(xsorted.shape, F32),
        grid_spec=grid_spec,
        compiler_params=_params(("arbitrary",)),
        name="expert_ffn",
    )(blk_expert, xsorted, w_gate, w_up, w_down)


def _sincos_2d(length, dim):
    rows = length // GRID_W
    quarter = dim // 4
    omega = 1.0 / (POS_BASE ** (jnp.arange(quarter, dtype=F32) / quarter))
    ang_r = jnp.arange(rows, dtype=F32)[:, None] * omega
    ang_c = jnp.arange(GRID_W, dtype=F32)[:, None] * omega
    emb_r = jnp.concatenate([jnp.sin(ang_r), jnp.cos(ang_r)], axis=-1)
    emb_c = jnp.concatenate([jnp.sin(ang_c), jnp.cos(ang_c)], axis=-1)
    half = dim // 2
    emb = jnp.concatenate([jnp.broadcast_to(emb_r[:, None, :], (rows, GRID_W, half)),
                           jnp.broadcast_to(emb_c[None, :, :], (rows, GRID_W, half))], axis=-1)
    return emb.reshape(rows * GRID_W, dim)


def _moe(h2, x1, idx, wts, counts, mod, layer, final_w, w_gate, w_up, w_down, n_b, seq, final):
    t_rows, d = x1.shape
    a = 2 * t_rows
    counts = counts[:, 0]
    padded = (counts + MOE_BLOCK - 1) // MOE_BLOCK * MOE_BLOCK
    pad_end = jnp.cumsum(padded)
    pad_start = pad_end - padded
    n_blk = (a + N_EXPERTS * (MOE_BLOCK - 1) + MOE_BLOCK - 1) // MOE_BLOCK
    p_rows = n_blk * MOE_BLOCK
    blk_start = jnp.arange(n_blk, dtype=jnp.int32) * MOE_BLOCK
    blk_expert = jnp.minimum(jnp.sum(blk_start[:, None] >= pad_end[None, :], axis=1),
                             N_EXPERTS - 1).astype(jnp.int32)
    e_ids = jnp.arange(N_EXPERTS, dtype=jnp.int32)
    seg_start = jnp.sum(jnp.where(idx[0:2, :, None] == e_ids, pad_start, 0), axis=-1)
    dest = (seg_start + idx[2:4]).reshape(-1).astype(jnp.int32)
    pad_lo = (pad_start + counts).astype(jnp.int32)
    pad_hi = pad_end.at[N_EXPERTS - 1].set(p_rows).astype(jnp.int32)
    xsorted = _dispatch(dest, pad_lo, pad_hi, h2, t_rows, p_rows)
    ys = _expert_ffn(blk_expert, xsorted, w_gate, w_up, w_down)
    return _combine(dest, x1, wts.T, mod, layer, final_w, ys, n_b, seq, final)


def kernel(x, c, ctx, c_ctx, ada_w, ada_b, norm_mix_w, norm_ffn_w, w_in, conv_w, a_log, dt_bias,
           dn_norm_w, ft_norm_w, w_out, router_w, router_bias, w_gate, w_up, w_down, final_norm_w):
    n_b, seq, d = x.shape
    ctx_len = ctx.shape[1]
    depth = ada_w.shape[0]
    n_lat = n_b * seq
    n_tok = n_lat + n_b * ctx_len
    ft_off = 4 * DN_WIDTH + N_GATE_COLS

    xs = jnp.concatenate([(x + _sincos_2d(seq, d)[None]).reshape(n_lat, d),
                          ctx.reshape(n_b * ctx_len, d)], axis=0)

    mod_rows = -(-(n_b + 1) // 8) * 8
    cvec = jnp.zeros((mod_rows, d), F32).at[:n_b].set(c).at[n_b].set(c_ctx)
    mod = _ada(cvec, ada_w, ada_b)

    cos_l, sin_l = _dft_tables(seq, seq // 2)
    cos_c, sin_c = _dft_tables(ctx_len, ctx_len // 2)
    cc, sc = _dft_tables(FT_GROUP_DIM)
    dft_cs = jnp.concatenate([cc, sc], axis=1)
    assert d == SUBLANES * LANES, "token tiles assume one (8, 128) tile per token"
    final_w = final_norm_w.reshape(1, d)
    router_wt = router_w.T
    rbias = router_bias.reshape(N_EXPERTS, 1)

    col = jnp.arange(N_GATE_COLS)
    col_head, col_dir, col_ab = col // 4, (col // 2) % 2, col % 2
    gate_src = col_dir * (2 * N_DN_HEADS) + col_ab * N_DN_HEADS + col_head
    is_a = (col_ab == 0).astype(F32)
    is_bwd = (col_dir == 1).astype(F32)

    gate_w = w_in[:, :, 4 * DN_WIDTH:ft_off][:, :, gate_src]
    w_pack = jnp.concatenate([w_in[:, :, :4 * DN_WIDTH], w_in[:, :, ft_off:]], axis=-1).astype(BF16)
    wabt = jnp.swapaxes(gate_w, 1, 2).astype(BF16)
    neg_a = -jnp.exp(a_log)[:, col_dir, col_head] * is_a
    dtb = dt_bias[:, col_dir, col_head] * is_a
    gpart = jnp.stack([neg_a, dtb, jnp.broadcast_to(is_a, neg_a.shape),
                       jnp.broadcast_to(is_bwd, neg_a.shape)], axis=2)
    w_out_b = w_out.astype(BF16)
    norm_mix = norm_mix_w.reshape(depth, 1, d)

    for i in range(depth):
        last = i == depth - 1
        qkv, z, xc, xsn, gcol, grow = _in_proj(xs, mod, i, norm_mix, w_pack, wabt, gpart, dft_cs, n_b, seq)
        dn_l, dn_c = _deltanet(qkv, z, gcol, grow, conv_w[i], dn_norm_w[i].reshape(1, DN_HEAD_DIM),
                               n_b, seq, ctx_len)
        fnw = ft_norm_w[i].reshape(1, FT_WIDTH)
        ft_l = _fourier(xc, xsn, cos_l, sin_l, fnw, n_b, seq, 0)
        if last:
            ft_c, rows = dn_c, n_lat
        else:
            ft_c = _fourier(xc, xsn, cos_c, sin_c, fnw, n_b, ctx_len, n_lat // ctx_len)
            rows = n_tok
        x1, h2, logits_t = _out_proj(xs, dn_l, dn_c, ft_l, ft_c, w_out_b, mod, i,
                                     norm_ffn_w[i].reshape(1, d), router_wt, n_b, seq, rows)
        idx, wts, counts = _route(logits_t, rbias)
        xs = _moe(h2, x1, idx, wts, counts, mod, i, final_w, w_gate[i], w_up[i], w_down[i], n_b, seq, last)
    return xs.reshape(n_b, seq, d)
```

```python
import functools
import math

import jax
import jax.numpy as jnp
from jax import lax
from jax.experimental import pallas as pl
from jax.experimental.pallas import tpu as pltpu

F32 = jnp.float32
BF16 = jnp.bfloat16

GRID_W = 64
N_DN_HEADS = 4
DN_HEAD_DIM = 128
DN_WIDTH = N_DN_HEADS * DN_HEAD_DIM
N_FT_GROUPS = 4
FT_GROUP_DIM = 128
FT_WIDTH = N_FT_GROUPS * FT_GROUP_DIM
CONV_K = 5
CHUNK = 64
N_EXPERTS = 16
N_EXPERT_GROUPS = 4
EXPERTS_PER_GROUP = N_EXPERTS // N_EXPERT_GROUPS
D_EXPERT = 512
MOE_BLOCK = 512
POS_BASE = 10000.0
EPS = 1e-6
N_GATE_COLS = 4 * N_DN_HEADS
CONV_BLOCK = 256
DN_UNROLL = 16

LANES = 128
SUBLANES = 8
VMEM_LIMIT = 56 * 1024 * 1024


def _params(sem, vmem=VMEM_LIMIT):
    return pltpu.CompilerParams(dimension_semantics=sem, vmem_limit_bytes=vmem)


def _dot(a, b):
    return jnp.dot(a.astype(BF16), b.astype(BF16), preferred_element_type=F32)


def _dot_nt(a, b):
    return lax.dot_general(a.astype(BF16), b.astype(BF16), (((1,), (1,)), ((), ())),
                           preferred_element_type=F32)


def _sigmoid(x):
    return 1.0 / (1.0 + jnp.exp(-x))


def _silu(x):
    return x * _sigmoid(x)


def _softplus(x):
    return jnp.maximum(x, 0.0) + jnp.log(1.0 + jnp.exp(-jnp.abs(x)))


def _ada_kernel(c_ref, w_ref, b_ref, o_ref):
    act = _silu(c_ref[...])
    o_ref[0] = jnp.dot(act, w_ref[0], preferred_element_type=F32,
                       precision=lax.Precision.HIGHEST) + b_ref[0]


def _ada(cvec, ada_w, ada_b):
    depth, d, n = ada_w.shape
    rows = cvec.shape[0]
    tn = 1024
    return pl.pallas_call(
        _ada_kernel,
        out_shape=jax.ShapeDtypeStruct((depth, rows, n), F32),
        grid=(depth, n // tn),
        in_specs=[pl.BlockSpec((rows, d), lambda i, j: (0, 0)),
                  pl.BlockSpec((1, d, tn), lambda i, j: (i, 0, j)),
                  pl.BlockSpec((1, 1, tn), lambda i, j: (i, 0, j))],
        out_specs=pl.BlockSpec((1, rows, tn), lambda i, j: (i, 0, j)),
        compiler_params=_params(("parallel", "parallel")),
        name="ada_mod",
    )(cvec, ada_w, ada_b.reshape(depth, 1, n))


def _pack_w_kernel(w_ref, o_ref):
    n_main = 4 * DN_WIDTH
    o_ref[0, :, :n_main] = w_ref[0, :, :n_main].astype(BF16)
    o_ref[0, :, n_main:] = w_ref[0, :, n_main + N_GATE_COLS:].astype(BF16)


def _pack_in_weights(w_in):
    depth, d, n_in = w_in.shape
    n_out = n_in - N_GATE_COLS
    tr = 256
    return pl.pallas_call(
        _pack_w_kernel,
        out_shape=jax.ShapeDtypeStruct((depth, d, n_out), BF16),
        grid=(depth, d // tr),
        in_specs=[pl.BlockSpec((1, tr, n_in), lambda i, r: (i, r, 0))],
        out_specs=pl.BlockSpec((1, tr, n_out), lambda i, r: (i, r, 0)),
        compiler_params=_params(("parallel", "parallel")),
        name="pack_w_in",
    )(w_in)


def _seg_scan(x, pos, axis, reverse):
    n = x.shape[axis]
    s = 1
    while s < CHUNK:
        if reverse:
            shifted = pltpu.roll(x, n - s, axis)
            x = x + jnp.where(pos < CHUNK - s, shifted, 0.0)
        else:
            shifted = pltpu.roll(x, s, axis)
            x = x + jnp.where(pos >= s, shifted, 0.0)
        s *= 2
    return x


def _in_proj_kernel(tm, n_lat_tiles_per_b, n_b, x_ref, mod_ref, nw_ref, w_ref, wabt_ref, gpart_ref,
                    dft_ref, qkv_ref, z_ref, xc_ref, xs_ref, gcol_ref, grow_ref):
    t = pl.program_id(0)
    bi = jnp.minimum(t // n_lat_tiles_per_b, n_b)
    d = x_ref.shape[1]
    x = x_ref[...]
    y = x * lax.rsqrt(jnp.mean(x * x, axis=-1, keepdims=True) + EPS) * nw_ref[0]
    shift = mod_ref[0, pl.ds(bi, 1), 0:d]
    scale = mod_ref[0, pl.ds(bi, 1), d:2 * d]
    h = (y * (1.0 + scale) + shift).astype(BF16)
    z_off, ft_off = 3 * DN_WIDTH, 4 * DN_WIDTH

    for j in range(3):
        cs = slice(j * DN_WIDTH, (j + 1) * DN_WIDTH)
        qkv_ref[:, cs] = jnp.dot(h, w_ref[0, :, cs], preferred_element_type=F32)
    z_ref[...] = jnp.dot(h, w_ref[0, :, z_off:ft_off], preferred_element_type=F32)

    ft = jnp.dot(h, w_ref[0, :, ft_off:], preferred_element_type=F32).astype(BF16)
    for g in range(N_FT_GROUPS):
        cs = slice(g * FT_GROUP_DIM, (g + 1) * FT_GROUP_DIM)
        cssn = jnp.dot(ft[:, cs], dft_ref[...], preferred_element_type=F32)
        xc_ref[:, cs] = cssn[:, :FT_GROUP_DIM].astype(BF16)
        xs_ref[:, cs] = cssn[:, FT_GROUP_DIM:].astype(BF16)

    abt = lax.dot_general(wabt_ref[0], h, (((1,), (1,)), ((), ())), preferred_element_type=F32)

    def gates(v, par, axis):
        neg_a, dtb, is_a, is_bwd = par
        g = neg_a * _softplus(v + dtb)
        pos = lax.broadcasted_iota(jnp.int32, v.shape, axis) % CHUNK
        fwd = _seg_scan(g, pos, axis, reverse=False)
        bwd = _seg_scan(g, pos, axis, reverse=True)
        cum = jnp.where(is_bwd > 0.5, bwd, fwd)
        return jnp.where(is_a > 0.5, cum, _sigmoid(v))

    gpt = gpart_ref[0]
    gt = gates(abt, (gpt[:, 0:1], gpt[:, 1:2], gpt[:, 2:3], gpt[:, 3:4]), 1)
    gc = gt.T
    for hh in range(N_DN_HEADS):
        gcol_ref[hh] = gc[:, 4 * hh:4 * hh + 4]
        for j in range(tm // CHUNK):
            cs = slice(j * CHUNK, (j + 1) * CHUNK)
            grow_ref[hh, j] = jnp.concatenate([gt[4 * hh:4 * hh + 1, cs], gt[4 * hh + 2:4 * hh + 3, cs]], axis=1)


def _in_proj(xs, mod_i, layer, norm_w, w_pack, wabt, gpart, dft_cs, n_b, seq):
    t_rows, d = xs.shape
    tm = 512
    n_tiles = t_rows // tm
    kern = functools.partial(_in_proj_kernel, tm, seq // tm, n_b)
    const = lambda *shape: pl.BlockSpec(shape, lambda t: tuple(0 for _ in shape))
    per_layer = lambda a: pl.BlockSpec((1,) + a.shape[1:], lambda t: (layer,) + (0,) * (a.ndim - 1))
    rows = lambda w: pl.BlockSpec((tm, w), lambda t: (t, 0))
    return pl.pallas_call(
        kern,
        out_shape=(jax.ShapeDtypeStruct((t_rows, 3 * DN_WIDTH), F32),
                   jax.ShapeDtypeStruct((t_rows, DN_WIDTH), F32),
                   jax.ShapeDtypeStruct((t_rows, FT_WIDTH), BF16),
                   jax.ShapeDtypeStruct((t_rows, FT_WIDTH), BF16),
                   jax.ShapeDtypeStruct((N_DN_HEADS, t_rows, 4), F32),
                   jax.ShapeDtypeStruct((N_DN_HEADS, t_rows // CHUNK, 1, 2 * CHUNK), F32)),
        grid=(n_tiles,),
        in_specs=[rows(d), per_layer(mod_i), per_layer(norm_w), per_layer(w_pack), per_layer(wabt),
                  per_layer(gpart), const(FT_GROUP_DIM, 2 * FT_GROUP_DIM)],
        out_specs=(rows(3 * DN_WIDTH), rows(DN_WIDTH), rows(FT_WIDTH), rows(FT_WIDTH),
                   pl.BlockSpec((N_DN_HEADS, tm, 4), lambda t: (0, t, 0)),
                   pl.BlockSpec((N_DN_HEADS, tm // CHUNK, 1, 2 * CHUNK), lambda t: (0, t, 0, 0))),
        compiler_params=_params(("parallel",)),
        name="in_proj",
    )(xs, mod_i, norm_w, w_pack, wabt, gpart, dft_cs)


def _conv_block(src_ref, cw, r0, n_rows, mode):
    blk = CONV_BLOCK
    half_k = CONV_K // 2
    if r0 >= half_k and r0 + blk + half_k <= n_rows:
        acc = src_ref[r0 - half_k:r0 - half_k + blk, :] * cw[0:1, :]
        for j in range(1, CONV_K):
            acc = acc + src_ref[r0 + j - half_k:r0 + j - half_k + blk, :] * cw[j:j + 1, :]
    else:
        halo = 8
        n_win = blk + 2 * halo
        zeros = jnp.zeros((halo, DN_HEAD_DIM), F32)
        prev = src_ref[r0 - halo:r0, :] if r0 > 0 else zeros
        nxt = src_ref[r0 + blk:r0 + blk + halo, :] if r0 + blk < n_rows else zeros
        win = jnp.concatenate([prev, src_ref[r0:r0 + blk, :], nxt], axis=0)
        acc = jnp.zeros((blk, DN_HEAD_DIM), F32)
        for j in range(CONV_K):
            shift = (half_k - j) % n_win
            rolled = win if shift == 0 else pltpu.roll(win, shift, 0)
            acc = acc + rolled[halo:halo + blk] * cw[j:j + 1, :]
    y = _silu(acc)
    if mode == "v":
        return y
    inv = lax.rsqrt(jnp.sum(y * y, axis=-1, keepdims=True) + EPS)
    if mode == "q":
        inv = inv * (DN_HEAD_DIM ** -0.5)
    return y * inv


def _blockdiag(x, isb):
    xb = x.astype(BF16)
    keep_b = jnp.where(isb, 1.0, 0.0).astype(BF16)
    keep_f = jnp.where(isb, 0.0, 1.0).astype(BF16)
    return jnp.concatenate([xb * keep_f, xb * keep_b], axis=0)


def _blockdiag_wide(x):
    w = x.shape[1] // 2
    zero = jnp.zeros((x.shape[0], w), BF16)
    xb = x.astype(BF16)
    return jnp.concatenate([jnp.concatenate([xb[:, :w], zero], axis=1),
                            jnp.concatenate([zero, xb[:, w:]], axis=1)], axis=0)


def _tri_inverse_dual(a_list, eye, xor, isb):
    mm = lambda x, y: jnp.dot(x.astype(BF16), _blockdiag(y, isb), preferred_element_type=F32)
    ad = [jnp.where((xor >> 3) == 0, a, 0.0) for a in a_list]
    a2 = [mm(x, x) for x in ad]
    a4 = [mm(x, x) for x in a2]
    t = [eye - x for x in ad]
    t = [x + mm(x, y) for x, y in zip(t, a2)]
    t = [x + mm(x, y) for x, y in zip(t, a4)]
    for s in (3, 4, 5):
        off = [jnp.where((xor >> s) == 1, a, 0.0) for a in a_list]
        to = [mm(x, y) for x, y in zip(t, off)]
        t = [x - mm(y, x) for x, y in zip(t, to)]
    return t


def _chunks_local(loaded):
    hd = DN_HEAD_DIM
    n = len(loaded)
    q = [x[0] for x in loaded]
    k = [x[1] for x in loaded]
    v = [x[2] for x in loaded]
    wide = lambda col: jnp.broadcast_to(col, (CHUNK, hd))
    gf = [wide(x[3][:, 0:1]) for x in loaded]
    bf_ = [wide(x[3][:, 1:2]) for x in loaded]
    gb = [wide(x[3][:, 2:3]) for x in loaded]
    bb = [wide(x[3][:, 3:4]) for x in loaded]
    grow = [x[4] for x in loaded]
    row = lax.broadcasted_iota(jnp.int32, (CHUNK, 2 * CHUNK), 0)
    lane = lax.broadcasted_iota(jnp.int32, (CHUNK, 2 * CHUNK), 1)
    jl = lane & (CHUNK - 1)
    isb = lane >= CHUNK
    delta = jnp.where(isb, jl - row, row - jl)
    xor = row ^ jl
    eye = jnp.where(delta == 0, 1.0, 0.0)
    nt = (((1,), (1,)), ((), ()))
    k2 = [jnp.concatenate([x, x], axis=0).astype(BF16) for x in k]
    kkd = [lax.dot_general(k[i].astype(BF16), k2[i], nt, preferred_element_type=F32) for i in range(n)]
    qkd = [lax.dot_general(q[i].astype(BF16), k2[i], nt, preferred_element_type=F32) for i in range(n)]
    dec = [jnp.where(delta >= 0, jnp.exp(jnp.where(isb, gb[i], gf[i]) - grow[i]), 0.0) for i in range(n)]
    a_mat = [jnp.where(delta > 0, kkd[i] * jnp.where(isb, bb[i], bf_[i]) * dec[i], 0.0) for i in range(n)]
    t_inv = _tri_inverse_dual(a_mat, eye, xor, isb)
    egf = [jnp.exp(x) for x in gf]
    egb = [jnp.exp(x) for x in gb]
    rhs = [jnp.concatenate([v[i] * bf_[i], k[i] * (bf_[i] * egf[i]), v[i] * bb[i], k[i] * (bb[i] * egb[i])],
                           axis=1) for i in range(n)]
    sol = [jnp.dot(t_inv[i].astype(BF16), _blockdiag_wide(rhs[i]), preferred_element_type=F32)
           for i in range(n)]
    bd_sol = [_blockdiag_wide(x) for x in sol]
    r1 = [jnp.dot((qkd[i] * dec[i]).astype(BF16), bd_sol[i], preferred_element_type=F32)
          for i in range(n)]
    glf = [x[CHUNK - 1:CHUNK, :] for x in gf]
    glb = [x[0:1, :] for x in gb]
    kdec = [jnp.concatenate([k[i] * jnp.exp(glf[i] - gf[i]), k[i] * jnp.exp(glb[i] - gb[i])], axis=0)
            for i in range(n)]
    r2 = [jnp.dot(kdec[i].T.astype(BF16), bd_sol[i], preferred_element_type=F32)
          for i in range(n)]
    out = []
    for i in range(n):
        o_loc = r1[i][:, 0:hd] + r1[i][:, 2 * hd:3 * hd]
        qt = jnp.concatenate([q[i] * egf[i] - r1[i][:, hd:2 * hd], q[i] * egb[i] - r1[i][:, 3 * hd:]],
                             axis=1).astype(BF16)
        nn = jnp.concatenate([r2[i][:, 0:hd], r2[i][:, 2 * hd:3 * hd]], axis=1)
        kw = jnp.concatenate([r2[i][:, hd:2 * hd], r2[i][:, 3 * hd:]], axis=1).astype(BF16)
        ge = jnp.concatenate([jnp.exp(glf[i]), jnp.exp(glb[i])], axis=0)
        out.append((o_loc, qt, nn, kw, ge))
    return out


def _state_step(c, d, oacc, qt_ref, kw_ref, nn_ref, ge_ref, s_ref):
    hd = DN_HEAD_DIM
    r0 = pl.multiple_of(c * CHUNK, CHUNK)
    cs = slice(d * hd, (d + 1) * hd)
    s = s_ref[d]
    lhs = jnp.concatenate([qt_ref[c, :, cs], kw_ref[c, :, cs]], axis=0)
    r = jnp.dot(lhs, s.astype(BF16), preferred_element_type=F32)
    oacc[pl.ds(r0, CHUNK), :] += r[:CHUNK]
    s_ref[d] = s * ge_ref[c, d:d + 1, :] + nn_ref[c, :, cs] - r[CHUNK:]


def _dn_kernel(seq, ctx_len,
               ql_ref, kl_ref, vl_ref, qc_ref, kc_ref, vc_ref, cwq_ref, cwk_ref, cwv_ref,
               zl_ref, zc_ref, gcl_ref, gcc_ref, grl_ref, grc_ref, nw_ref,
               ol_ref, oc_ref,
               qn, kn, vn, oacc, qt_s, kw_s, nn_s, ge_s, s_s):
    nw = nw_ref[...]
    s_s[...] = jnp.zeros_like(s_s)

    def segment(n_rows, q_ref, k_ref, v_ref, z_ref, gcol_ref, grow_ref, out_ref):
        n_chunks = n_rows // CHUNK
        unroll = math.gcd(DN_UNROLL, n_chunks)
        g_rows = unroll * CHUNK
        n_groups = n_chunks // unroll

        def prep(g, slot):
            for b0 in range(0, g_rows, CONV_BLOCK):
                r0 = g * g_rows + b0
                for src, cwr, dst, mode in ((q_ref, cwq_ref, qn, "q"), (k_ref, cwk_ref, kn, "k"),
                                            (v_ref, cwv_ref, vn, "v")):
                    dst[slot, b0:b0 + CONV_BLOCK, :] = _conv_block(src, cwr[...], r0, n_rows, mode)

        prep(0, 0)
        for g in range(n_groups):
            slot = g % 2
            chunks = [g * unroll + j for j in range(unroll)]
            tile = lambda ref, j: ref[slot, j * CHUNK:(j + 1) * CHUNK, :]
            loaded = [(tile(qn, j), tile(kn, j), tile(vn, j),
                       gcol_ref[0, c * CHUNK:(c + 1) * CHUNK, :], grow_ref[0, c])
                      for j, c in enumerate(chunks)]
            if g + 1 < n_groups:
                prep(g + 1, 1 - slot)
            results = _chunks_local(loaded)
            for c, (o_loc, qt, nn, kw, ge) in zip(chunks, results):
                oacc[c * CHUNK:(c + 1) * CHUNK, :] = o_loc
                qt_s[c] = qt
                nn_s[c] = nn
                kw_s[c] = kw
                ge_s[c] = ge

        def step(s, carry):
            _state_step(s, 0, oacc, qt_s, kw_s, nn_s, ge_s, s_s)
            _state_step(n_chunks - 1 - s, 1, oacc, qt_s, kw_s, nn_s, ge_s, s_s)
            return carry

        lax.fori_loop(0, n_chunks, step, 0)

        blk = 256

        def fin(i, carry):
            r0 = pl.multiple_of(i * blk, blk)
            o = oacc[pl.ds(r0, blk), :]
            o = o * lax.rsqrt(jnp.mean(o * o, axis=-1, keepdims=True) + EPS) * nw
            out_ref[pl.ds(r0, blk), :] = (o * _silu(z_ref[pl.ds(r0, blk), :])).astype(out_ref.dtype)
            return carry

        lax.fori_loop(0, n_rows // blk, fin, 0)

    segment(ctx_len, qc_ref, kc_ref, vc_ref, zc_ref, gcc_ref, grc_ref, oc_ref)
    segment(seq, ql_ref, kl_ref, vl_ref, zl_ref, gcl_ref, grl_ref, ol_ref)


def _deltanet(qkv, z, gcol, grow, conv_w, dn_norm_w, n_b, seq, ctx_len):
    hd = DN_HEAD_DIM
    nc = seq // CHUNK
    cb = n_b * seq // ctx_len
    lat = lambda off: pl.BlockSpec((seq, hd), lambda b, h: (b, h + off))
    ctx = lambda off: pl.BlockSpec((ctx_len, hd), lambda b, h: (cb + b, h + off))
    cws = lambda off: pl.BlockSpec((CONV_K, hd), lambda b, h: (0, h + off))
    kern = functools.partial(_dn_kernel, seq, ctx_len)
    return pl.pallas_call(
        kern,
        out_shape=(jax.ShapeDtypeStruct((n_b * seq, DN_WIDTH), BF16),
                   jax.ShapeDtypeStruct((n_b * ctx_len, DN_WIDTH), BF16)),
        grid=(n_b, N_DN_HEADS),
        in_specs=[lat(0), lat(N_DN_HEADS), lat(2 * N_DN_HEADS),
                  ctx(0), ctx(N_DN_HEADS), ctx(2 * N_DN_HEADS),
                  cws(0), cws(N_DN_HEADS), cws(2 * N_DN_HEADS),
                  lat(0), ctx(0),
                  pl.BlockSpec((1, seq, 4), lambda b, h: (h, b, 0)),
                  pl.BlockSpec((1, ctx_len, 4), lambda b, h: (h, cb + b, 0)),
                  pl.BlockSpec((1, nc, 1, 2 * CHUNK), lambda b, h: (h, b, 0, 0)),
                  pl.BlockSpec((1, ctx_len // CHUNK, 1, 2 * CHUNK), lambda b, h: (h, cb + b, 0, 0)),
                  pl.BlockSpec((1, hd), lambda b, h: (0, 0))],
        out_specs=(pl.BlockSpec((seq, hd), lambda b, h: (b, h)),
                   pl.BlockSpec((ctx_len, hd), lambda b, h: (b, h))),
        scratch_shapes=[pltpu.VMEM((2, min(DN_UNROLL, nc) * CHUNK, hd), F32) for _ in range(3)]
        + [pltpu.VMEM((seq, hd), F32),
           pltpu.VMEM((nc, CHUNK, 2 * hd), BF16), pltpu.VMEM((nc, hd, 2 * hd), BF16),
           pltpu.VMEM((nc, hd, 2 * hd), F32), pltpu.VMEM((nc, 2, hd), F32),
           pltpu.VMEM((2, hd, hd), F32)],
        compiler_params=_params(("parallel", "parallel")),
        name="deltanet",
    )(qkv, qkv, qkv, qkv, qkv, qkv, conv_w, conv_w, conv_w, z, z, gcol, gcol, grow, grow, dn_norm_w)


def _reversed_rows(ref, first_row, n_rows, blk, flip_mat):
    n_blk = n_rows // blk
    parts = [jnp.dot(flip_mat, ref[pl.ds(first_row + (n_blk - 1 - i) * blk, blk), :],
                     preferred_element_type=F32) for i in range(n_blk)]
    return jnp.concatenate(parts, axis=0)


def _ft_kernel(tm, scale, cl_ref, sl_ref, xc_ref, xs_ref, nw_ref, o_ref, xcf, xsf, hi_s):
    mt = pl.program_id(1)
    n_m = pl.num_programs(1)
    half = xcf.shape[0]
    ii = lax.broadcasted_iota(jnp.int32, (tm, tm), 0)
    jj = lax.broadcasted_iota(jnp.int32, (tm, tm), 1)
    flip_mat = jnp.where(ii + jj == tm - 1, 1.0, 0.0).astype(BF16)
    row_h = lax.broadcasted_iota(jnp.int32, (half, 1), 0)

    def shifted_reverse(ref, first_row):
        rev = _reversed_rows(ref, first_row, half, tm, flip_mat)
        return jnp.where(row_h == 0, 0.0, pltpu.roll(rev, 1, 0))

    @pl.when(mt == 0)
    def _():
        xcf[...] = (xc_ref[0:half, :].astype(F32) + shifted_reverse(xc_ref, half)).astype(BF16)
        xsf[...] = (xs_ref[0:half, :].astype(F32) - shifted_reverse(xs_ref, half)).astype(BF16)

    a = jnp.dot(cl_ref[...], xcf[...], preferred_element_type=F32)
    b = jnp.dot(sl_ref[...], xsf[...], preferred_element_type=F32)
    x_nyq = xc_ref[half:half + 2 * SUBLANES, :].astype(F32)[0:1, :]
    m_idx = mt * tm + lax.broadcasted_iota(jnp.int32, (tm, 1), 0)
    corr = jnp.where((m_idx & 1) == 0, 1.0, -1.0) * x_nyq
    nw = nw_ref[...]

    def norm(y):
        y = y * scale
        return y * lax.rsqrt(jnp.mean(y * y, axis=-1, keepdims=True) + EPS) * nw

    r0 = pl.multiple_of(mt * tm, tm)
    o_ref[pl.ds(r0, tm), :] = norm(a - b + corr).astype(o_ref.dtype)
    hi_s[pl.ds(r0, tm), :] = norm(a + b + corr).astype(hi_s.dtype)

    @pl.when(mt == n_m - 1)
    def _():
        lane = lax.broadcasted_iota(jnp.int32, (SUBLANES, half), 1)
        alt = jnp.where((lane & 1) == 0, 1.0, -1.0).astype(BF16)
        y_nyq = norm(jnp.dot(alt, xcf[...], preferred_element_type=F32)[0:1, :] + x_nyq)
        upper = jnp.where(row_h == 0, y_nyq, pltpu.roll(_reversed_rows(hi_s, 0, half, tm, flip_mat), 1, 0))
        o_ref[half:, :] = upper.astype(o_ref.dtype)


def _fourier(xc, xs, cos_h, sin_h, ft_norm_w, n_b, length, row_block0):
    half = length // 2
    tm = min(512, half)
    n_m = half // tm
    scale = 1.0 / math.sqrt(length * FT_GROUP_DIM)
    return pl.pallas_call(
        functools.partial(_ft_kernel, tm, scale),
        out_shape=jax.ShapeDtypeStruct((n_b * length, FT_WIDTH), BF16),
        grid=(n_b, n_m),
        in_specs=[pl.BlockSpec((tm, half), lambda b, m: (m, 0)),
                  pl.BlockSpec((tm, half), lambda b, m: (m, 0)),
                  pl.BlockSpec((length, FT_WIDTH), lambda b, m: (row_block0 + b, 0)),
                  pl.BlockSpec((length, FT_WIDTH), lambda b, m: (row_block0 + b, 0)),
                  pl.BlockSpec((1, FT_WIDTH), lambda b, m: (0, 0))],
        out_specs=pl.BlockSpec((length, FT_WIDTH), lambda b, m: (b, 0)),
        scratch_shapes=[pltpu.VMEM((half, FT_WIDTH), BF16), pltpu.VMEM((half, FT_WIDTH), BF16),
                        pltpu.VMEM((half, FT_WIDTH), BF16)],
        compiler_params=_params(("parallel", "arbitrary")),
        name="fourier_seq",
    )(cos_h, sin_h, xc, xs, ft_norm_w)


def _dft_tables(length, n=None):
    m = jnp.arange(length if n is None else n, dtype=jnp.int32)
    ph = (m[:, None] * m[None, :]) % length
    ang = ph.astype(F32) * (2.0 * math.pi / length)
    return jnp.cos(ang).astype(BF16), jnp.sin(ang).astype(BF16)


def _store_token_tiles(ref, val):
    rows, d = val.shape
    n_lt = d // LANES
    for j in range(n_lt):
        ref[pl.ds(j, rows, stride=n_lt), :] = val[:, j * LANES:(j + 1) * LANES]


def _load_token_tiles(ref, rows, n_lt):
    return jnp.concatenate([ref[pl.ds(j, rows, stride=n_lt), :] for j in range(n_lt)], axis=1)


def _out_proj_kernel(n_lat_tiles, n_lat_tiles_per_b, n_b, x_ref, dnl_ref, dnc_ref, ftl_ref, ftc_ref,
                     wdn_ref, wft_ref, mod_ref, nw_ref, rwt_ref, x1_ref, h2_ref, lg_ref):
    t = pl.program_id(0)
    bi = jnp.minimum(t // n_lat_tiles_per_b, n_b)
    d = x_ref.shape[1]
    is_lat = t < n_lat_tiles
    dn = jnp.where(is_lat, dnl_ref[...], dnc_ref[...])
    ft = jnp.where(is_lat, ftl_ref[...], ftc_ref[...])
    mix = (jnp.dot(dn, wdn_ref[0], preferred_element_type=F32)
           + jnp.dot(ft, wft_ref[0], preferred_element_type=F32))
    g1 = mod_ref[0, pl.ds(bi, 1), 2 * d:3 * d]
    sh2 = mod_ref[0, pl.ds(bi, 1), 3 * d:4 * d]
    sc2 = mod_ref[0, pl.ds(bi, 1), 4 * d:5 * d]
    x1 = x_ref[...] + g1 * mix
    x1_ref[...] = x1
    y = x1 * lax.rsqrt(jnp.mean(x1 * x1, axis=-1, keepdims=True) + EPS) * nw_ref[...]
    h2 = y * (1.0 + sc2) + sh2
    _store_token_tiles(h2_ref, h2)
    lg_ref[...] = lax.dot_general(rwt_ref[...], h2, (((1,), (1,)), ((), ())),
                                  preferred_element_type=F32, precision=lax.Precision.HIGHEST)


def _out_proj(xs, dn_l, dn_c, ft_l, ft_c, w_out, mod_i, layer, norm_w, router_wt, n_b, seq, t_rows):
    d = xs.shape[1]
    tm = 512
    n_lt = d // LANES
    n_lat_tiles = n_b * seq // tm
    kern = functools.partial(_out_proj_kernel, n_lat_tiles, seq // tm, n_b)
    const = lambda *shape: pl.BlockSpec(shape, lambda t: tuple(0 for _ in shape))
    rows = lambda w: pl.BlockSpec((tm, w), lambda t: (t, 0))
    lat = lambda w: pl.BlockSpec((tm, w), lambda t: (jnp.minimum(t, n_lat_tiles - 1), 0))
    ctx = lambda w: pl.BlockSpec((tm, w), lambda t: (jnp.maximum(t - n_lat_tiles, 0), 0))
    return pl.pallas_call(
        kern,
        out_shape=(jax.ShapeDtypeStruct((t_rows, d), F32),
                   jax.ShapeDtypeStruct((t_rows * n_lt, LANES), F32),
                   jax.ShapeDtypeStruct((N_EXPERTS, t_rows), F32)),
        grid=(t_rows // tm,),
        in_specs=[rows(d), lat(DN_WIDTH), ctx(DN_WIDTH), lat(FT_WIDTH), ctx(FT_WIDTH),
                  pl.BlockSpec((1, DN_WIDTH, d), lambda t: (layer, 0, 0)),
                  pl.BlockSpec((1, FT_WIDTH, d), lambda t: (layer, DN_WIDTH // FT_WIDTH, 0)),
                  pl.BlockSpec((1,) + mod_i.shape[1:], lambda t: (layer, 0, 0)),
                  const(1, d), const(N_EXPERTS, d)],
        out_specs=(rows(d), pl.BlockSpec((tm * n_lt, LANES), lambda t: (t, 0)),
                   pl.BlockSpec((N_EXPERTS, tm), lambda t: (0, t))),
        compiler_params=_params(("parallel",)),
        name="out_proj",
    )(xs, dn_l, dn_c, ft_l, ft_c, w_out, w_out, mod_i, norm_w, router_wt)


def _route_kernel(tr, lg_ref, bias_ref, idx_ref, wt_ref, cnt_ref, upper, carry):
    step = pl.program_id(0)

    @pl.when(step == 0)
    def _():
        a = lax.broadcasted_iota(jnp.int32, (tr, tr), 0)
        b = lax.broadcasted_iota(jnp.int32, (tr, tr), 1)
        upper[...] = jnp.where(a < b, 1.0, 0.0).astype(BF16)
        carry[...] = jnp.zeros_like(carry)

    scores = _sigmoid(lg_ref[...])
    biased = scores + bias_ref[...]
    rows = [biased[r:r + 1, :] for r in range(N_EXPERTS)]
    srow = [scores[r:r + 1, :] for r in range(N_EXPERTS)]
    epg = EXPERTS_PER_GROUP

    def group_score(g):
        best = None
        for i in range(epg):
            for j in range(i + 1, epg):
                pair = rows[g * epg + i] + rows[g * epg + j]
                best = pair if best is None else jnp.maximum(best, pair)
        return best

    best_g = jnp.zeros((1, tr), jnp.int32)
    best_v = group_score(0)
    for g in range(1, N_EXPERT_GROUPS):
        gs = group_score(g)
        take = gs > best_v
        best_g = jnp.where(take, g, best_g)
        best_v = jnp.where(take, gs, best_v)

    def pick(table, r):
        out = table[r]
        for g in range(1, N_EXPERT_GROUPS):
            out = jnp.where(best_g == g, table[g * epg + r], out)
        return out

    in_b = [pick(rows, r) for r in range(epg)]
    in_s = [pick(srow, r) for r in range(epg)]
    l1 = jnp.zeros((1, tr), jnp.int32)
    m1 = in_b[0]
    for r in range(1, epg):
        take = in_b[r] > m1
        l1 = jnp.where(take, r, l1)
        m1 = jnp.where(take, in_b[r], m1)
    l2 = jnp.full((1, tr), -1, jnp.int32)
    m2 = jnp.full((1, tr), -jnp.inf, F32)
    for r in range(epg):
        take = jnp.logical_and(l1 != r, jnp.logical_or(l2 < 0, in_b[r] > m2))
        l2 = jnp.where(take, r, l2)
        m2 = jnp.where(take, in_b[r], m2)
    s1 = in_s[0]
    s2 = in_s[0]
    for r in range(1, epg):
        s1 = jnp.where(l1 == r, in_s[r], s1)
        s2 = jnp.where(l2 == r, in_s[r], s2)
    e1 = best_g * epg + l1
    e2 = best_g * epg + l2
    tot = s1 + s2
    wt_ref[0:1, :] = s1 / tot
    wt_ref[1:2, :] = s2 / tot

    eid = lax.broadcasted_iota(jnp.int32, (N_EXPERTS, tr), 0)
    is1 = eid == e1
    is2 = eid == e2
    memb = jnp.where(jnp.logical_or(is1, is2), 1.0, 0.0)
    prefix = jnp.dot(memb.astype(BF16), upper[...], preferred_element_type=F32) + carry[...]
    rank1 = jnp.sum(jnp.where(is1, prefix, 0.0), axis=0, keepdims=True)
    rank2 = jnp.sum(jnp.where(is2, prefix, 0.0), axis=0, keepdims=True)
    idx_ref[0:1, :] = e1
    idx_ref[1:2, :] = e2
    idx_ref[2:3, :] = rank1.astype(jnp.int32)
    idx_ref[3:4, :] = rank2.astype(jnp.int32)
    new_carry = carry[...] + jnp.sum(memb, axis=1, keepdims=True)
    carry[...] = new_carry
    cnt_ref[...] = new_carry.astype(jnp.int32)


def _route(logits_t, router_bias):
    n_e, t_rows = logits_t.shape
    tr = 512
    return pl.pallas_call(
        functools.partial(_route_kernel, tr),
        out_shape=(jax.ShapeDtypeStruct((4, t_rows), jnp.int32),
                   jax.ShapeDtypeStruct((2, t_rows), F32),
                   jax.ShapeDtypeStruct((n_e, 1), jnp.int32)),
        grid=(t_rows // tr,),
        in_specs=[pl.BlockSpec((n_e, tr), lambda t: (0, t)),
                  pl.BlockSpec((n_e, 1), lambda t: (0, 0))],
        out_specs=(pl.BlockSpec((4, tr), lambda t: (0, t)),
                   pl.BlockSpec((2, tr), lambda t: (0, t)),
                   pl.BlockSpec((n_e, 1), lambda t: (0, 0))),
        scratch_shapes=[pltpu.VMEM((tr, tr), BF16), pltpu.VMEM((n_e, 1), F32)],
        compiler_params=_params(("arbitrary",)),
        name="route",
    )(logits_t, router_bias)


def _row_copy(src, src_row, dst, dst_row, sem):
    s0 = pl.multiple_of(src_row * SUBLANES, SUBLANES)
    d0 = pl.multiple_of(dst_row * SUBLANES, SUBLANES)
    return pltpu.make_async_copy(src.at[pl.ds(s0, SUBLANES), :], dst.at[pl.ds(d0, SUBLANES), :], sem)


def _dispatch_kernel(tm, t_rows, dest_ref, lo_ref, hi_ref, h_ref, xs_hbm, sem, pad_sem):
    i = pl.program_id(0)
    base = i * tm

    @pl.when(i == 0)
    def _():
        n_bits = tm.bit_length() - 1

        def run(copy_op):
            def per_expert(e, carry):
                lo = lo_ref[e]
                n = hi_ref[e] - lo

                def big(j, c):
                    d0 = pl.multiple_of((lo + j * tm) * SUBLANES, SUBLANES)
                    copy_op(pltpu.make_async_copy(h_ref, xs_hbm.at[pl.ds(d0, tm * SUBLANES), :], pad_sem))
                    return c

                lax.fori_loop(0, n >> n_bits, big, 0)
                for bit in range(n_bits - 1, -1, -1):
                    size = (1 << bit) * SUBLANES
                    done = (n >> (bit + 1)) << (bit + 1)

                    @pl.when(((n >> bit) & 1) == 1)
                    def _():
                        d0 = pl.multiple_of((lo + done) * SUBLANES, SUBLANES)
                        copy_op(pltpu.make_async_copy(h_ref.at[pl.ds(0, size), :],
                                                      xs_hbm.at[pl.ds(d0, size), :], pad_sem))
                return carry

            lax.fori_loop(0, N_EXPERTS, per_expert, 0)

        run(lambda cp: cp.start())
        run(lambda cp: cp.wait())

    def body(r, carry):
        _row_copy(h_ref, r, xs_hbm, dest_ref[base + r], sem).start(priority=0)
        _row_copy(h_ref, r, xs_hbm, dest_ref[t_rows + base + r], sem).start(priority=1)
        return carry

    lax.fori_loop(0, tm, body, 0, unroll=8)
    whole = pltpu.make_async_copy(h_ref, xs_hbm.at[pl.ds(0, tm * SUBLANES), :], sem)
    whole.wait()
    whole.wait()


def _dispatch(dest, pad_lo, pad_hi, h2, t_rows, p_rows):
    tm = 1024
    grid_spec = pltpu.PrefetchScalarGridSpec(
        num_scalar_prefetch=3,
        grid=(t_rows // tm,),
        in_specs=[pl.BlockSpec((tm * SUBLANES, LANES), lambda i, *_: (i, 0))],
        out_specs=pl.BlockSpec(memory_space=pl.ANY),
        scratch_shapes=[pltpu.SemaphoreType.DMA, pltpu.SemaphoreType.DMA],
    )
    return pl.pallas_call(
        functools.partial(_dispatch_kernel, tm, t_rows),
        out_shape=jax.ShapeDtypeStruct((p_rows * SUBLANES, LANES), F32),
        grid_spec=grid_spec,
        compiler_params=_params(("arbitrary",)),
        name="moe_dispatch",
    )(dest, pad_lo, pad_hi, h2)


def _combine_kernel(tm, t_rows, n_lat_tiles_per_b, n_b, final, dest_ref, x1_ref, wt_ref, mod_ref, fw_ref,
                    ys_hbm, o_ref, buf, sem):
    i = pl.program_id(0)
    n_steps = pl.num_programs(0)
    d = x1_ref.shape[1]
    n_lt = d // LANES

    def issue(step, slot):
        base = step * tm

        def body(r, carry):
            _row_copy(ys_hbm, dest_ref[base + r], buf.at[slot, 0], r, sem.at[slot]).start(priority=0)
            _row_copy(ys_hbm, dest_ref[t_rows + base + r], buf.at[slot, 1], r, sem.at[slot]).start(priority=1)
            return carry

        lax.fori_loop(0, tm, body, 0, unroll=8)

    @pl.when(i == 0)
    def _():
        issue(0, 0)

    @pl.when(i + 1 < n_steps)
    def _():
        issue(i + 1, (i + 1) % 2)

    slot = i % 2
    for k in range(2):
        pltpu.make_async_copy(ys_hbm.at[pl.ds(0, tm * n_lt), :], buf.at[slot, k], sem.at[slot]).wait()
    bi = jnp.minimum(i // n_lat_tiles_per_b, n_b)
    g2 = mod_ref[0, pl.ds(bi, 1), 5 * d:6 * d]
    wt = wt_ref[...]
    y = (_load_token_tiles(buf.at[slot, 0], tm, n_lt) * wt[:, 0:1]
         + _load_token_tiles(buf.at[slot, 1], tm, n_lt) * wt[:, 1:2])
    out = x1_ref[...] + g2 * y
    if final:
        out = out * lax.rsqrt(jnp.mean(out * out, axis=-1, keepdims=True) + EPS) * fw_ref[...]
    o_ref[...] = out


def _combine(dest, x1, wts_t, mod, layer, final_w, ys, n_b, seq, final):
    t_rows, d = x1.shape
    tm = 512
    n_lt = d // LANES
    grid_spec = pltpu.PrefetchScalarGridSpec(
        num_scalar_prefetch=1,
        grid=(t_rows // tm,),
        in_specs=[pl.BlockSpec((tm, d), lambda i, *_: (i, 0)),
                  pl.BlockSpec((tm, 2), lambda i, *_: (i, 0)),
                  pl.BlockSpec((1,) + mod.shape[1:], lambda i, *_: (layer, 0, 0)),
                  pl.BlockSpec((1, d), lambda i, *_: (0, 0)),
                  pl.BlockSpec(memory_space=pl.ANY)],
        out_specs=pl.BlockSpec((tm, d), lambda i, *_: (i, 0)),
        scratch_shapes=[pltpu.VMEM((2, 2, tm * n_lt, LANES), F32), pltpu.SemaphoreType.DMA((2,))],
    )
    return pl.pallas_call(
        functools.partial(_combine_kernel, tm, t_rows, seq // tm, n_b, final),
        out_shape=jax.ShapeDtypeStruct((t_rows, d), F32),
        grid_spec=grid_spec,
        compiler_params=_params(("arbitrary",)),
        name="moe_combine",
    )(dest, x1, wts_t, mod, final_w, ys)


def _ffn_kernel(n_lt, be_ref, x_ref, wg_ref, wu_ref, wd_ref, o_ref, wg_s, wu_s, wd_s):
    i = pl.program_id(0)
    prev_expert = be_ref[jnp.maximum(i - 1, 0)]

    @pl.when(jnp.logical_or(i == 0, be_ref[i] != prev_expert))
    def _():
        wg_s[...] = wg_ref[0].astype(BF16)
        wu_s[...] = wu_ref[0].astype(BF16)
        wd_s[...] = wd_ref[0].astype(BF16)

    x = _load_token_tiles(x_ref, MOE_BLOCK, n_lt).astype(BF16)
    gate = jnp.dot(x, wg_s[...], preferred_element_type=F32)
    up = jnp.dot(x, wu_s[...], preferred_element_type=F32)
    hid = (_silu(gate) * up).astype(BF16)
    _store_token_tiles(o_ref, jnp.dot(hid, wd_s[...], preferred_element_type=F32))


def _expert_ffn(blk_expert, xsorted, w_gate, w_up, w_down):
    d, de = w_gate.shape[-2:]
    n_lt = d // LANES
    n_blk = xsorted.shape[0] // (MOE_BLOCK * n_lt)
    grid_spec = pltpu.PrefetchScalarGridSpec(
        num_scalar_prefetch=1,
        grid=(n_blk,),
        in_specs=[pl.BlockSpec((MOE_BLOCK * n_lt, LANES), lambda i, be: (i, 0)),
                  pl.BlockSpec((1, d, de), lambda i, be: (be[i], 0, 0)),
                  pl.BlockSpec((1, d, de), lambda i, be: (be[i], 0, 0)),
                  pl.BlockSpec((1, de, d), lambda i, be: (be[i], 0, 0))],
        out_specs=pl.BlockSpec((MOE_BLOCK * n_lt, LANES), lambda i, be: (i, 0)),
        scratch_shapes=[pltpu.VMEM((d, de), BF16), pltpu.VMEM((d, de), BF16), pltpu.VMEM((de, d), BF16)],
    )
    return pl.pallas_call(
        functools.partial(_ffn_kernel, n_lt),
        out_shape=jax.ShapeDtypeStruct(xsorted.shape, F32),
        grid_spec=grid_spec,
        compiler_params=_params(("arbitrary",)),
        name="expert_ffn",
    )(blk_expert, xsorted, w_gate, w_up, w_down)


def _sincos_2d(length, dim):
    rows = length // GRID_W
    quarter = dim // 4
    omega = 1.0 / (POS_BASE ** (jnp.arange(quarter, dtype=F32) / quarter))
    ang_r = jnp.arange(rows, dtype=F32)[:, None] * omega
    ang_c = jnp.arange(GRID_W, dtype=F32)[:, None] * omega
    emb_r = jnp.concatenate([jnp.sin(ang_r), jnp.cos(ang_r)], axis=-1)
    emb_c = jnp.concatenate([jnp.sin(ang_c), jnp.cos(ang_c)], axis=-1)
    half = dim // 2
    emb = jnp.concatenate([jnp.broadcast_to(emb_r[:, None, :], (rows, GRID_W, half)),
                           jnp.broadcast_to(emb_c[None, :, :], (rows, GRID_W, half))], axis=-1)
    return emb.reshape(rows * GRID_W, dim)


def _moe(h2, x1, idx, wts, counts, mod, layer, final_w, w_gate, w_up, w_down, n_b, seq, final):
    t_rows, d = x1.shape
    a = 2 * t_rows
    counts = counts[:, 0]
    padded = (counts + MOE_BLOCK - 1) // MOE_BLOCK * MOE_BLOCK
    pad_end = jnp.cumsum(padded)
    pad_start = pad_end - padded
    n_blk = (a + N_EXPERTS * (MOE_BLOCK - 1) + MOE_BLOCK - 1) // MOE_BLOCK
    p_rows = n_blk * MOE_BLOCK
    blk_start = jnp.arange(n_blk, dtype=jnp.int32) * MOE_BLOCK
    blk_expert = jnp.minimum(jnp.sum(blk_start[:, None] >= pad_end[None, :], axis=1),
                             N_EXPERTS - 1).astype(jnp.int32)
    e_ids = jnp.arange(N_EXPERTS, dtype=jnp.int32)
    seg_start = jnp.sum(jnp.where(idx[0:2, :, None] == e_ids, pad_start, 0), axis=-1)
    dest = (seg_start + idx[2:4]).reshape(-1).astype(jnp.int32)
    pad_lo = (pad_start + counts).astype(jnp.int32)
    pad_hi = pad_end.at[N_EXPERTS - 1].set(p_rows).astype(jnp.int32)
    xsorted = _dispatch(dest, pad_lo, pad_hi, h2, t_rows, p_rows)
    ys = _expert_ffn(blk_expert, xsorted, w_gate, w_up, w_down)
    return _combine(dest, x1, wts.T, mod, layer, final_w, ys, n_b, seq, final)


def kernel(x, c, ctx, c_ctx, ada_w, ada_b, norm_mix_w, norm_ffn_w, w_in, conv_w, a_log, dt_bias,
           dn_norm_w, ft_norm_w, w_out, router_w, router_bias, w_gate, w_up, w_down, final_norm_w):
    n_b, seq, d = x.shape
    ctx_len = ctx.shape[1]
    depth = ada_w.shape[0]
    n_lat = n_b * seq
    n_tok = n_lat + n_b * ctx_len
    ft_off = 4 * DN_WIDTH + N_GATE_COLS

    xs = jnp.concatenate([(x + _sincos_2d(seq, d)[None]).reshape(n_lat, d),
                          ctx.reshape(n_b * ctx_len, d)], axis=0)

    mod_rows = -(-(n_b + 1) // 8) * 8
    cvec = jnp.zeros((mod_rows, d), F32).at[:n_b].set(c).at[n_b].set(c_ctx)
    mod = _ada(cvec, ada_w, ada_b)

    cos_l, sin_l = _dft_tables(seq, seq // 2)
    cos_c, sin_c = _dft_tables(ctx_len, ctx_len // 2)
    cc, sc = _dft_tables(FT_GROUP_DIM)
    dft_cs = jnp.concatenate([cc, sc], axis=1)
    assert d == SUBLANES * LANES, "token tiles assume one (8, 128) tile per token"
    final_w = final_norm_w.reshape(1, d)
    router_wt = router_w.T
    rbias = router_bias.reshape(N_EXPERTS, 1)

    col = jnp.arange(N_GATE_COLS)
    col_head, col_dir, col_ab = col // 4, (col // 2) % 2, col % 2
    gate_src = col_dir * (2 * N_DN_HEADS) + col_ab * N_DN_HEADS + col_head
    is_a = (col_ab == 0).astype(F32)
    is_bwd = (col_dir == 1).astype(F32)

    gate_w = w_in[:, :, 4 * DN_WIDTH:ft_off][:, :, gate_src]
    w_pack = _pack_in_weights(w_in)
    wabt = jnp.swapaxes(gate_w, 1, 2).astype(BF16)
    neg_a = -jnp.exp(a_log)[:, col_dir, col_head] * is_a
    dtb = dt_bias[:, col_dir, col_head] * is_a
    gpart = jnp.stack([neg_a, dtb, jnp.broadcast_to(is_a, neg_a.shape),
                       jnp.broadcast_to(is_bwd, neg_a.shape)], axis=2)
    w_out_b = w_out.astype(BF16)
    norm_mix = norm_mix_w.reshape(depth, 1, d)

    for i in range(depth):
        last = i == depth - 1
        qkv, z, xc, xsn, gcol, grow = _in_proj(xs, mod, i, norm_mix, w_pack, wabt, gpart, dft_cs, n_b, seq)
        dn_l, dn_c = _deltanet(qkv, z, gcol, grow, conv_w[i], dn_norm_w[i].reshape(1, DN_HEAD_DIM),
                               n_b, seq, ctx_len)
        fnw = ft_norm_w[i].reshape(1, FT_WIDTH)
        ft_l = _fourier(xc, xsn, cos_l, sin_l, fnw, n_b, seq, 0)
        if last:
            ft_c, rows = dn_c, n_lat
        else:
            ft_c = _fourier(xc, xsn, cos_c, sin_c, fnw, n_b, ctx_len, n_lat // ctx_len)
            rows = n_tok
        x1, h2, logits_t = _out_proj(xs, dn_l, dn_c, ft_l, ft_c, w_out_b, mod, i,
                                     norm_ffn_w[i].reshape(1, d), router_wt, n_b, seq, rows)
        idx, wts, counts = _route(logits_t, rbias)
        xs = _moe(h2, x1, idx, wts, counts, mod, i, final_w, w_gate[i], w_up[i], w_down[i], n_b, seq, last)
    return xs.reshape(n_b, seq, d)
```

```python
import functools
import math

import jax
import jax.numpy as jnp
from jax import lax
from jax.experimental import pallas as pl
from jax.experimental.pallas import tpu as pltpu

F32 = jnp.float32
BF16 = jnp.bfloat16

GRID_W = 64
N_DN_HEADS = 4
DN_HEAD_DIM = 128
DN_WIDTH = N_DN_HEADS * DN_HEAD_DIM
N_FT_GROUPS = 4
FT_GROUP_DIM = 128
FT_WIDTH = N_FT_GROUPS * FT_GROUP_DIM
CONV_K = 5
CHUNK = 64
N_EXPERTS = 16
N_EXPERT_GROUPS = 4
EXPERTS_PER_GROUP = N_EXPERTS // N_EXPERT_GROUPS
D_EXPERT = 512
MOE_BLOCK = 512
POS_BASE = 10000.0
EPS = 1e-6
N_GATE_COLS = 4 * N_DN_HEADS
CONV_BLOCK = 256
DN_UNROLL = 16

LANES = 128
SUBLANES = 8
VMEM_LIMIT = 56 * 1024 * 1024


def _params(sem, vmem=VMEM_LIMIT):
    return pltpu.CompilerParams(dimension_semantics=sem, vmem_limit_bytes=vmem)


def _dot(a, b):
    return jnp.dot(a.astype(BF16), b.astype(BF16), preferred_element_type=F32)


def _dot_nt(a, b):
    return lax.dot_general(a.astype(BF16), b.astype(BF16), (((1,), (1,)), ((), ())),
                           preferred_element_type=F32)


def _sigmoid(x):
    return 1.0 / (1.0 + jnp.exp(-x))


def _silu(x):
    return x * _sigmoid(x)


def _softplus(x):
    return jnp.maximum(x, 0.0) + jnp.log(1.0 + jnp.exp(-jnp.abs(x)))


def _ada_kernel(c_ref, w_ref, b_ref, o_ref):
    act = _silu(c_ref[...])
    o_ref[0] = jnp.dot(act, w_ref[0], preferred_element_type=F32,
                       precision=lax.Precision.HIGHEST) + b_ref[0]


def _ada(cvec, ada_w, ada_b):
    depth, d, n = ada_w.shape
    rows = cvec.shape[0]
    tn = 1024
    return pl.pallas_call(
        _ada_kernel,
        out_shape=jax.ShapeDtypeStruct((depth, rows, n), F32),
        grid=(depth, n // tn),
        in_specs=[pl.BlockSpec((rows, d), lambda i, j: (0, 0)),
                  pl.BlockSpec((1, d, tn), lambda i, j: (i, 0, j)),
                  pl.BlockSpec((1, 1, tn), lambda i, j: (i, 0, j))],
        out_specs=pl.BlockSpec((1, rows, tn), lambda i, j: (i, 0, j)),
        compiler_params=_params(("parallel", "parallel")),
        name="ada_mod",
    )(cvec, ada_w, ada_b.reshape(depth, 1, n))


def _embed_kernel(n_lat_tiles, x_ref, pos_ref, ctx_ref, o_ref):
    t = pl.program_id(0)
    o_ref[...] = jnp.where(t < n_lat_tiles, x_ref[...] + pos_ref[...], ctx_ref[...])


def _embed(x2d, pos, ctx2d):
    n_lat, d = x2d.shape
    seq = pos.shape[0]
    tm = 512
    n_lat_tiles = n_lat // tm
    n_tok = n_lat + ctx2d.shape[0]
    return pl.pallas_call(
        functools.partial(_embed_kernel, n_lat_tiles),
        out_shape=jax.ShapeDtypeStruct((n_tok, d), F32),
        grid=(n_tok // tm,),
        in_specs=[pl.BlockSpec((tm, d), lambda t: (jnp.minimum(t, n_lat_tiles - 1), 0)),
                  pl.BlockSpec((tm, d), lambda t: (t % (seq // tm), 0)),
                  pl.BlockSpec((tm, d), lambda t: (jnp.maximum(t - n_lat_tiles, 0), 0))],
        out_specs=pl.BlockSpec((tm, d), lambda t: (t, 0)),
        compiler_params=_params(("parallel",)),
        name="embed",
    )(x2d, pos, ctx2d)


def _pack_w_kernel(w_ref, o_ref):
    n_main = 4 * DN_WIDTH
    o_ref[0, :, :n_main] = w_ref[0, :, :n_main].astype(BF16)
    o_ref[0, :, n_main:] = w_ref[0, :, n_main + N_GATE_COLS:].astype(BF16)


def _pack_in_weights(w_in):
    depth, d, n_in = w_in.shape
    n_out = n_in - N_GATE_COLS
    tr = 256
    return pl.pallas_call(
        _pack_w_kernel,
        out_shape=jax.ShapeDtypeStruct((depth, d, n_out), BF16),
        grid=(depth, d // tr),
        in_specs=[pl.BlockSpec((1, tr, n_in), lambda i, r: (i, r, 0))],
        out_specs=pl.BlockSpec((1, tr, n_out), lambda i, r: (i, r, 0)),
        compiler_params=_params(("parallel", "parallel")),
        name="pack_w_in",
    )(w_in)


def _seg_scan(x, pos, axis, reverse):
    n = x.shape[axis]
    s = 1
    while s < CHUNK:
        if reverse:
            shifted = pltpu.roll(x, n - s, axis)
            x = x + jnp.where(pos < CHUNK - s, shifted, 0.0)
        else:
            shifted = pltpu.roll(x, s, axis)
            x = x + jnp.where(pos >= s, shifted, 0.0)
        s *= 2
    return x


def _in_proj_kernel(tm, n_lat_tiles_per_b, n_b, x_ref, mod_ref, nw_ref, w_ref, wabt_ref, gpart_ref,
                    dft_ref, qkv_ref, z_ref, xc_ref, xs_ref, gcol_ref, grow_ref):
    t = pl.program_id(0)
    bi = jnp.minimum(t // n_lat_tiles_per_b, n_b)
    d = x_ref.shape[1]
    x = x_ref[...]
    y = x * lax.rsqrt(jnp.mean(x * x, axis=-1, keepdims=True) + EPS) * nw_ref[0]
    shift = mod_ref[0, pl.ds(bi, 1), 0:d]
    scale = mod_ref[0, pl.ds(bi, 1), d:2 * d]
    h = (y * (1.0 + scale) + shift).astype(BF16)
    z_off, ft_off = 3 * DN_WIDTH, 4 * DN_WIDTH

    for j in range(3):
        cs = slice(j * DN_WIDTH, (j + 1) * DN_WIDTH)
        qkv_ref[:, cs] = jnp.dot(h, w_ref[0, :, cs], preferred_element_type=F32)
    z_ref[...] = jnp.dot(h, w_ref[0, :, z_off:ft_off], preferred_element_type=F32)

    ft = jnp.dot(h, w_ref[0, :, ft_off:], preferred_element_type=F32).astype(BF16)
    for g in range(N_FT_GROUPS):
        cs = slice(g * FT_GROUP_DIM, (g + 1) * FT_GROUP_DIM)
        cssn = jnp.dot(ft[:, cs], dft_ref[...], preferred_element_type=F32)
        xc_ref[:, cs] = cssn[:, :FT_GROUP_DIM].astype(BF16)
        xs_ref[:, cs] = cssn[:, FT_GROUP_DIM:].astype(BF16)

    abt = lax.dot_general(wabt_ref[0], h, (((1,), (1,)), ((), ())), preferred_element_type=F32)

    def gates(v, par, axis):
        neg_a, dtb, is_a, is_bwd = par
        g = neg_a * _softplus(v + dtb)
        pos = lax.broadcasted_iota(jnp.int32, v.shape, axis) % CHUNK
        fwd = _seg_scan(g, pos, axis, reverse=False)
        bwd = _seg_scan(g, pos, axis, reverse=True)
        cum = jnp.where(is_bwd > 0.5, bwd, fwd)
        return jnp.where(is_a > 0.5, cum, _sigmoid(v))

    gpt = gpart_ref[0]
    gt = gates(abt, (gpt[:, 0:1], gpt[:, 1:2], gpt[:, 2:3], gpt[:, 3:4]), 1)
    gc = gt.T
    for hh in range(N_DN_HEADS):
        gcol_ref[hh] = gc[:, 4 * hh:4 * hh + 4]
        for j in range(tm // CHUNK):
            cs = slice(j * CHUNK, (j + 1) * CHUNK)
            grow_ref[hh, j] = jnp.concatenate([gt[4 * hh:4 * hh + 1, cs], gt[4 * hh + 2:4 * hh + 3, cs]], axis=1)


def _in_proj(xs, mod_i, layer, norm_w, w_pack, wabt, gpart, dft_cs, n_b, seq):
    t_rows, d = xs.shape
    tm = 512
    n_tiles = t_rows // tm
    kern = functools.partial(_in_proj_kernel, tm, seq // tm, n_b)
    const = lambda *shape: pl.BlockSpec(shape, lambda t: tuple(0 for _ in shape))
    per_layer = lambda a: pl.BlockSpec((1,) + a.shape[1:], lambda t: (layer,) + (0,) * (a.ndim - 1))
    rows = lambda w: pl.BlockSpec((tm, w), lambda t: (t, 0))
    return pl.pallas_call(
        kern,
        out_shape=(jax.ShapeDtypeStruct((t_rows, 3 * DN_WIDTH), F32),
                   jax.ShapeDtypeStruct((t_rows, DN_WIDTH), F32),
                   jax.ShapeDtypeStruct((t_rows, FT_WIDTH), BF16),
                   jax.ShapeDtypeStruct((t_rows, FT_WIDTH), BF16),
                   jax.ShapeDtypeStruct((N_DN_HEADS, t_rows, 4), F32),
                   jax.ShapeDtypeStruct((N_DN_HEADS, t_rows // CHUNK, 1, 2 * CHUNK), F32)),
        grid=(n_tiles,),
        in_specs=[rows(d), per_layer(mod_i), per_layer(norm_w), per_layer(w_pack), per_layer(wabt),
                  per_layer(gpart), const(FT_GROUP_DIM, 2 * FT_GROUP_DIM)],
        out_specs=(rows(3 * DN_WIDTH), rows(DN_WIDTH), rows(FT_WIDTH), rows(FT_WIDTH),
                   pl.BlockSpec((N_DN_HEADS, tm, 4), lambda t: (0, t, 0)),
                   pl.BlockSpec((N_DN_HEADS, tm // CHUNK, 1, 2 * CHUNK), lambda t: (0, t, 0, 0))),
        compiler_params=_params(("parallel",)),
        name="in_proj",
    )(xs, mod_i, norm_w, w_pack, wabt, gpart, dft_cs)


def _conv_block(src_ref, cw, r0, n_rows, mode):
    blk = CONV_BLOCK
    half_k = CONV_K // 2
    if r0 >= half_k and r0 + blk + half_k <= n_rows:
        acc = src_ref[r0 - half_k:r0 - half_k + blk, :] * cw[0:1, :]
        for j in range(1, CONV_K):
            acc = acc + src_ref[r0 + j - half_k:r0 + j - half_k + blk, :] * cw[j:j + 1, :]
    else:
        halo = 8
        n_win = blk + 2 * halo
        zeros = jnp.zeros((halo, DN_HEAD_DIM), F32)
        prev = src_ref[r0 - halo:r0, :] if r0 > 0 else zeros
        nxt = src_ref[r0 + blk:r0 + blk + halo, :] if r0 + blk < n_rows else zeros
        win = jnp.concatenate([prev, src_ref[r0:r0 + blk, :], nxt], axis=0)
        acc = jnp.zeros((blk, DN_HEAD_DIM), F32)
        for j in range(CONV_K):
            shift = (half_k - j) % n_win
            rolled = win if shift == 0 else pltpu.roll(win, shift, 0)
            acc = acc + rolled[halo:halo + blk] * cw[j:j + 1, :]
    y = _silu(acc)
    if mode == "v":
        return y
    inv = lax.rsqrt(jnp.sum(y * y, axis=-1, keepdims=True) + EPS)
    if mode == "q":
        inv = inv * (DN_HEAD_DIM ** -0.5)
    return y * inv


def _blockdiag(x, isb):
    xb = x.astype(BF16)
    keep_b = jnp.where(isb, 1.0, 0.0).astype(BF16)
    keep_f = jnp.where(isb, 0.0, 1.0).astype(BF16)
    return jnp.concatenate([xb * keep_f, xb * keep_b], axis=0)


def _blockdiag_wide(x):
    w = x.shape[1] // 2
    zero = jnp.zeros((x.shape[0], w), BF16)
    xb = x.astype(BF16)
    return jnp.concatenate([jnp.concatenate([xb[:, :w], zero], axis=1),
                            jnp.concatenate([zero, xb[:, w:]], axis=1)], axis=0)


def _tri_inverse_dual(a_list, eye, xor, isb):
    mm = lambda x, y: jnp.dot(x.astype(BF16), _blockdiag(y, isb), preferred_element_type=F32)
    ad = [jnp.where((xor >> 3) == 0, a, 0.0) for a in a_list]
    a2 = [mm(x, x) for x in ad]
    a4 = [mm(x, x) for x in a2]
    t = [eye - x for x in ad]
    t = [x + mm(x, y) for x, y in zip(t, a2)]
    t = [x + mm(x, y) for x, y in zip(t, a4)]
    for s in (3, 4, 5):
        off = [jnp.where((xor >> s) == 1, a, 0.0) for a in a_list]
        to = [mm(x, y) for x, y in zip(t, off)]
        t = [x - mm(y, x) for x, y in zip(t, to)]
    return t


def _chunks_local(loaded):
    hd = DN_HEAD_DIM
    n = len(loaded)
    q = [x[0] for x in loaded]
    k = [x[1] for x in loaded]
    v = [x[2] for x in loaded]
    wide = lambda col: jnp.broadcast_to(col, (CHUNK, hd))
    gf = [wide(x[3][:, 0:1]) for x in loaded]
    bf_ = [wide(x[3][:, 1:2]) for x in loaded]
    gb = [wide(x[3][:, 2:3]) for x in loaded]
    bb = [wide(x[3][:, 3:4]) for x in loaded]
    grow = [x[4] for x in loaded]
    row = lax.broadcasted_iota(jnp.int32, (CHUNK, 2 * CHUNK), 0)
    lane = lax.broadcasted_iota(jnp.int32, (CHUNK, 2 * CHUNK), 1)
    jl = lane & (CHUNK - 1)
    isb = lane >= CHUNK
    delta = jnp.where(isb, jl - row, row - jl)
    xor = row ^ jl
    eye = jnp.where(delta == 0, 1.0, 0.0)
    nt = (((1,), (1,)), ((), ()))
    k2 = [jnp.concatenate([x, x], axis=0).astype(BF16) for x in k]
    kkd = [lax.dot_general(k[i].astype(BF16), k2[i], nt, preferred_element_type=F32) for i in range(n)]
    qkd = [lax.dot_general(q[i].astype(BF16), k2[i], nt, preferred_element_type=F32) for i in range(n)]
    dec = [jnp.where(delta >= 0, jnp.exp(jnp.where(isb, gb[i], gf[i]) - grow[i]), 0.0) for i in range(n)]
    a_mat = [jnp.where(delta > 0, kkd[i] * jnp.where(isb, bb[i], bf_[i]) * dec[i], 0.0) for i in range(n)]
    t_inv = _tri_inverse_dual(a_mat, eye, xor, isb)
    egf = [jnp.exp(x) for x in gf]
    egb = [jnp.exp(x) for x in gb]
    rhs = [jnp.concatenate([v[i] * bf_[i], k[i] * (bf_[i] * egf[i]), v[i] * bb[i], k[i] * (bb[i] * egb[i])],
                           axis=1) for i in range(n)]
    sol = [jnp.dot(t_inv[i].astype(BF16), _blockdiag_wide(rhs[i]), preferred_element_type=F32)
           for i in range(n)]
    bd_sol = [_blockdiag_wide(x) for x in sol]
    r1 = [jnp.dot((qkd[i] * dec[i]).astype(BF16), bd_sol[i], preferred_element_type=F32)
          for i in range(n)]
    glf = [x[CHUNK - 1:CHUNK, :] for x in gf]
    glb = [x[0:1, :] for x in gb]
    kdec = [jnp.concatenate([k[i] * jnp.exp(glf[i] - gf[i]), k[i] * jnp.exp(glb[i] - gb[i])], axis=0)
            for i in range(n)]
    r2 = [jnp.dot(kdec[i].T.astype(BF16), bd_sol[i], preferred_element_type=F32)
          for i in range(n)]
    out = []
    for i in range(n):
        o_loc = r1[i][:, 0:hd] + r1[i][:, 2 * hd:3 * hd]
        qt = jnp.concatenate([q[i] * egf[i] - r1[i][:, hd:2 * hd], q[i] * egb[i] - r1[i][:, 3 * hd:]],
                             axis=1).astype(BF16)
        nn = jnp.concatenate([r2[i][:, 0:hd], r2[i][:, 2 * hd:3 * hd]], axis=1)
        kw = jnp.concatenate([r2[i][:, hd:2 * hd], r2[i][:, 3 * hd:]], axis=1).astype(BF16)
        ge = jnp.concatenate([jnp.exp(glf[i]), jnp.exp(glb[i])], axis=0)
        out.append((o_loc, qt, nn, kw, ge))
    return out


def _state_step(c, d, oacc, qt_ref, kw_ref, nn_ref, ge_ref, s_ref):
    hd = DN_HEAD_DIM
    r0 = pl.multiple_of(c * CHUNK, CHUNK)
    cs = slice(d * hd, (d + 1) * hd)
    s = s_ref[d]
    lhs = jnp.concatenate([qt_ref[c, :, cs], kw_ref[c, :, cs]], axis=0)
    r = jnp.dot(lhs, s.astype(BF16), preferred_element_type=F32)
    oacc[pl.ds(r0, CHUNK), :] += r[:CHUNK]
    s_ref[d] = s * ge_ref[c, d:d + 1, :] + nn_ref[c, :, cs] - r[CHUNK:]


def _dn_kernel(seq, ctx_len,
               ql_ref, kl_ref, vl_ref, qc_ref, kc_ref, vc_ref, cwq_ref, cwk_ref, cwv_ref,
               zl_ref, zc_ref, gcl_ref, gcc_ref, grl_ref, grc_ref, nw_ref,
               ol_ref, oc_ref,
               qn, kn, vn, oacc, qt_s, kw_s, nn_s, ge_s, s_s):
    nw = nw_ref[...]
    s_s[...] = jnp.zeros_like(s_s)

    def segment(n_rows, q_ref, k_ref, v_ref, z_ref, gcol_ref, grow_ref, out_ref):
        n_chunks = n_rows // CHUNK
        unroll = math.gcd(DN_UNROLL, n_chunks)
        g_rows = unroll * CHUNK
        n_groups = n_chunks // unroll

        def prep(g, slot):
            for b0 in range(0, g_rows, CONV_BLOCK):
                r0 = g * g_rows + b0
                for src, cwr, dst, mode in ((q_ref, cwq_ref, qn, "q"), (k_ref, cwk_ref, kn, "k"),
                                            (v_ref, cwv_ref, vn, "v")):
                    dst[slot, b0:b0 + CONV_BLOCK, :] = _conv_block(src, cwr[...], r0, n_rows, mode)

        prep(0, 0)
        for g in range(n_groups):
            slot = g % 2
            chunks = [g * unroll + j for j in range(unroll)]
            tile = lambda ref, j: ref[slot, j * CHUNK:(j + 1) * CHUNK, :]
            loaded = [(tile(qn, j), tile(kn, j), tile(vn, j),
                       gcol_ref[0, c * CHUNK:(c + 1) * CHUNK, :], grow_ref[0, c])
                      for j, c in enumerate(chunks)]
            if g + 1 < n_groups:
                prep(g + 1, 1 - slot)
            results = _chunks_local(loaded)
            for c, (o_loc, qt, nn, kw, ge) in zip(chunks, results):
                oacc[c * CHUNK:(c + 1) * CHUNK, :] = o_loc
                qt_s[c] = qt
                nn_s[c] = nn
                kw_s[c] = kw
                ge_s[c] = ge

        def step(s, carry):
            _state_step(s, 0, oacc, qt_s, kw_s, nn_s, ge_s, s_s)
            _state_step(n_chunks - 1 - s, 1, oacc, qt_s, kw_s, nn_s, ge_s, s_s)
            return carry

        lax.fori_loop(0, n_chunks, step, 0)

        blk = 256

        def fin(i, carry):
            r0 = pl.multiple_of(i * blk, blk)
            o = oacc[pl.ds(r0, blk), :]
            o = o * lax.rsqrt(jnp.mean(o * o, axis=-1, keepdims=True) + EPS) * nw
            out_ref[pl.ds(r0, blk), :] = (o * _silu(z_ref[pl.ds(r0, blk), :])).astype(out_ref.dtype)
            return carry

        lax.fori_loop(0, n_rows // blk, fin, 0)

    segment(ctx_len, qc_ref, kc_ref, vc_ref, zc_ref, gcc_ref, grc_ref, oc_ref)
    segment(seq, ql_ref, kl_ref, vl_ref, zl_ref, gcl_ref, grl_ref, ol_ref)


def _deltanet(qkv, z, gcol, grow, conv_w, dn_norm_w, n_b, seq, ctx_len):
    hd = DN_HEAD_DIM
    nc = seq // CHUNK
    cb = n_b * seq // ctx_len
    lat = lambda off: pl.BlockSpec((seq, hd), lambda b, h: (b, h + off))
    ctx = lambda off: pl.BlockSpec((ctx_len, hd), lambda b, h: (cb + b, h + off))
    cws = lambda off: pl.BlockSpec((CONV_K, hd), lambda b, h: (0, h + off))
    kern = functools.partial(_dn_kernel, seq, ctx_len)
    return pl.pallas_call(
        kern,
        out_shape=(jax.ShapeDtypeStruct((n_b * seq, DN_WIDTH), BF16),
                   jax.ShapeDtypeStruct((n_b * ctx_len, DN_WIDTH), BF16)),
        grid=(n_b, N_DN_HEADS),
        in_specs=[lat(0), lat(N_DN_HEADS), lat(2 * N_DN_HEADS),
                  ctx(0), ctx(N_DN_HEADS), ctx(2 * N_DN_HEADS),
                  cws(0), cws(N_DN_HEADS), cws(2 * N_DN_HEADS),
                  lat(0), ctx(0),
                  pl.BlockSpec((1, seq, 4), lambda b, h: (h, b, 0)),
                  pl.BlockSpec((1, ctx_len, 4), lambda b, h: (h, cb + b, 0)),
                  pl.BlockSpec((1, nc, 1, 2 * CHUNK), lambda b, h: (h, b, 0, 0)),
                  pl.BlockSpec((1, ctx_len // CHUNK, 1, 2 * CHUNK), lambda b, h: (h, cb + b, 0, 0)),
                  pl.BlockSpec((1, hd), lambda b, h: (0, 0))],
        out_specs=(pl.BlockSpec((seq, hd), lambda b, h: (b, h)),
                   pl.BlockSpec((ctx_len, hd), lambda b, h: (b, h))),
        scratch_shapes=[pltpu.VMEM((2, min(DN_UNROLL, nc) * CHUNK, hd), F32) for _ in range(3)]
        + [pltpu.VMEM((seq, hd), F32),
           pltpu.VMEM((nc, CHUNK, 2 * hd), BF16), pltpu.VMEM((nc, hd, 2 * hd), BF16),
           pltpu.VMEM((nc, hd, 2 * hd), F32), pltpu.VMEM((nc, 2, hd), F32),
           pltpu.VMEM((2, hd, hd), F32)],
        compiler_params=_params(("parallel", "parallel")),
        name="deltanet",
    )(qkv, qkv, qkv, qkv, qkv, qkv, conv_w, conv_w, conv_w, z, z, gcol, gcol, grow, grow, dn_norm_w)


def _reversed_rows(ref, first_row, n_rows, blk, flip_mat):
    n_blk = n_rows // blk
    parts = [jnp.dot(flip_mat, ref[pl.ds(first_row + (n_blk - 1 - i) * blk, blk), :],
                     preferred_element_type=F32) for i in range(n_blk)]
    return jnp.concatenate(parts, axis=0)


def _ft_kernel(tm, scale, cl_ref, sl_ref, xc_ref, xs_ref, nw_ref, o_ref, xcf, xsf, hi_s):
    mt = pl.program_id(1)
    n_m = pl.num_programs(1)
    half = xcf.shape[0]
    ii = lax.broadcasted_iota(jnp.int32, (tm, tm), 0)
    jj = lax.broadcasted_iota(jnp.int32, (tm, tm), 1)
    flip_mat = jnp.where(ii + jj == tm - 1, 1.0, 0.0).astype(BF16)
    row_h = lax.broadcasted_iota(jnp.int32, (half, 1), 0)

    def shifted_reverse(ref, first_row):
        rev = _reversed_rows(ref, first_row, half, tm, flip_mat)
        return jnp.where(row_h == 0, 0.0, pltpu.roll(rev, 1, 0))

    @pl.when(mt == 0)
    def _():
        xcf[...] = (xc_ref[0:half, :].astype(F32) + shifted_reverse(xc_ref, half)).astype(BF16)
        xsf[...] = (xs_ref[0:half, :].astype(F32) - shifted_reverse(xs_ref, half)).astype(BF16)

    a = jnp.dot(cl_ref[...], xcf[...], preferred_element_type=F32)
    b = jnp.dot(sl_ref[...], xsf[...], preferred_element_type=F32)
    x_nyq = xc_ref[half:half + 2 * SUBLANES, :].astype(F32)[0:1, :]
    m_idx = mt * tm + lax.broadcasted_iota(jnp.int32, (tm, 1), 0)
    corr = jnp.where((m_idx & 1) == 0, 1.0, -1.0) * x_nyq
    nw = nw_ref[...]

    def norm(y):
        y = y * scale
        return y * lax.rsqrt(jnp.mean(y * y, axis=-1, keepdims=True) + EPS) * nw

    r0 = pl.multiple_of(mt * tm, tm)
    o_ref[pl.ds(r0, tm), :] = norm(a - b + corr).astype(o_ref.dtype)
    hi_s[pl.ds(r0, tm), :] = norm(a + b + corr).astype(hi_s.dtype)

    @pl.when(mt == n_m - 1)
    def _():
        lane = lax.broadcasted_iota(jnp.int32, (SUBLANES, half), 1)
        alt = jnp.where((lane & 1) == 0, 1.0, -1.0).astype(BF16)
        y_nyq = norm(jnp.dot(alt, xcf[...], preferred_element_type=F32)[0:1, :] + x_nyq)
        upper = jnp.where(row_h == 0, y_nyq, pltpu.roll(_reversed_rows(hi_s, 0, half, tm, flip_mat), 1, 0))
        o_ref[half:, :] = upper.astype(o_ref.dtype)


def _fourier(xc, xs, cos_h, sin_h, ft_norm_w, n_b, length, row_block0):
    half = length // 2
    tm = min(512, half)
    n_m = half // tm
    scale = 1.0 / math.sqrt(length * FT_GROUP_DIM)
    return pl.pallas_call(
        functools.partial(_ft_kernel, tm, scale),
        out_shape=jax.ShapeDtypeStruct((n_b * length, FT_WIDTH), BF16),
        grid=(n_b, n_m),
        in_specs=[pl.BlockSpec((tm, half), lambda b, m: (m, 0)),
                  pl.BlockSpec((tm, half), lambda b, m: (m, 0)),
                  pl.BlockSpec((length, FT_WIDTH), lambda b, m: (row_block0 + b, 0)),
                  pl.BlockSpec((length, FT_WIDTH), lambda b, m: (row_block0 + b, 0)),
                  pl.BlockSpec((1, FT_WIDTH), lambda b, m: (0, 0))],
        out_specs=pl.BlockSpec((length, FT_WIDTH), lambda b, m: (b, 0)),
        scratch_shapes=[pltpu.VMEM((half, FT_WIDTH), BF16), pltpu.VMEM((half, FT_WIDTH), BF16),
                        pltpu.VMEM((half, FT_WIDTH), BF16)],
        compiler_params=_params(("parallel", "arbitrary")),
        name="fourier_seq",
    )(cos_h, sin_h, xc, xs, ft_norm_w)


def _dft_tables(length, n=None):
    m = jnp.arange(length if n is None else n, dtype=jnp.int32)
    ph = (m[:, None] * m[None, :]) % length
    ang = ph.astype(F32) * (2.0 * math.pi / length)
    return jnp.cos(ang).astype(BF16), jnp.sin(ang).astype(BF16)


def _store_token_tiles(ref, val):
    rows, d = val.shape
    n_lt = d // LANES
    for j in range(n_lt):
        ref[pl.ds(j, rows, stride=n_lt), :] = val[:, j * LANES:(j + 1) * LANES]


def _load_token_tiles(ref, rows, n_lt):
    return jnp.concatenate([ref[pl.ds(j, rows, stride=n_lt), :] for j in range(n_lt)], axis=1)


def _out_proj_kernel(n_lat_tiles, n_lat_tiles_per_b, n_b, x_ref, dnl_ref, dnc_ref, ftl_ref, ftc_ref,
                     wdn_ref, wft_ref, mod_ref, nw_ref, rwt_ref, x1_ref, h2_ref, lg_ref):
    t = pl.program_id(0)
    bi = jnp.minimum(t // n_lat_tiles_per_b, n_b)
    d = x_ref.shape[1]
    is_lat = t < n_lat_tiles
    dn = jnp.where(is_lat, dnl_ref[...], dnc_ref[...])
    ft = jnp.where(is_lat, ftl_ref[...], ftc_ref[...])
    mix = (jnp.dot(dn, wdn_ref[0], preferred_element_type=F32)
           + jnp.dot(ft, wft_ref[0], preferred_element_type=F32))
    g1 = mod_ref[0, pl.ds(bi, 1), 2 * d:3 * d]
    sh2 = mod_ref[0, pl.ds(bi, 1), 3 * d:4 * d]
    sc2 = mod_ref[0, pl.ds(bi, 1), 4 * d:5 * d]
    x1 = x_ref[...] + g1 * mix
    x1_ref[...] = x1
    y = x1 * lax.rsqrt(jnp.mean(x1 * x1, axis=-1, keepdims=True) + EPS) * nw_ref[...]
    h2 = y * (1.0 + sc2) + sh2
    _store_token_tiles(h2_ref, h2)
    lg_ref[...] = lax.dot_general(rwt_ref[...], h2, (((1,), (1,)), ((), ())),
                                  preferred_element_type=F32, precision=lax.Precision.HIGHEST)


def _out_proj(xs, dn_l, dn_c, ft_l, ft_c, w_out, mod_i, layer, norm_w, router_wt, n_b, seq, t_rows):
    d = xs.shape[1]
    tm = 512
    n_lt = d // LANES
    n_lat_tiles = n_b * seq // tm
    kern = functools.partial(_out_proj_kernel, n_lat_tiles, seq // tm, n_b)
    const = lambda *shape: pl.BlockSpec(shape, lambda t: tuple(0 for _ in shape))
    rows = lambda w: pl.BlockSpec((tm, w), lambda t: (t, 0))
    lat = lambda w: pl.BlockSpec((tm, w), lambda t: (jnp.minimum(t, n_lat_tiles - 1), 0))
    ctx = lambda w: pl.BlockSpec((tm, w), lambda t: (jnp.maximum(t - n_lat_tiles, 0), 0))
    return pl.pallas_call(
        kern,
        out_shape=(jax.ShapeDtypeStruct((t_rows, d), F32),
                   jax.ShapeDtypeStruct((t_rows * n_lt, LANES), F32),
                   jax.ShapeDtypeStruct((N_EXPERTS, t_rows), F32)),
        grid=(t_rows // tm,),
        in_specs=[rows(d), lat(DN_WIDTH), ctx(DN_WIDTH), lat(FT_WIDTH), ctx(FT_WIDTH),
                  pl.BlockSpec((1, DN_WIDTH, d), lambda t: (layer, 0, 0)),
                  pl.BlockSpec((1, FT_WIDTH, d), lambda t: (layer, DN_WIDTH // FT_WIDTH, 0)),
                  pl.BlockSpec((1,) + mod_i.shape[1:], lambda t: (layer, 0, 0)),
                  const(1, d), const(N_EXPERTS, d)],
        out_specs=(rows(d), pl.BlockSpec((tm * n_lt, LANES), lambda t: (t, 0)),
                   pl.BlockSpec((N_EXPERTS, tm), lambda t: (0, t))),
        compiler_params=_params(("parallel",)),
        name="out_proj",
    )(xs, dn_l, dn_c, ft_l, ft_c, w_out, w_out, mod_i, norm_w, router_wt)


def _route_kernel(tr, lg_ref, bias_ref, idx_ref, wt_ref, cnt_ref, upper, carry):
    step = pl.program_id(0)

    @pl.when(step == 0)
    def _():
        a = lax.broadcasted_iota(jnp.int32, (tr, tr), 0)
        b = lax.broadcasted_iota(jnp.int32, (tr, tr), 1)
        upper[...] = jnp.where(a < b, 1.0, 0.0).astype(BF16)
        carry[...] = jnp.zeros_like(carry)

    scores = _sigmoid(lg_ref[...])
    biased = scores + bias_ref[...]
    rows = [biased[r:r + 1, :] for r in range(N_EXPERTS)]
    srow = [scores[r:r + 1, :] for r in range(N_EXPERTS)]
    epg = EXPERTS_PER_GROUP

    def group_score(g):
        best = None
        for i in range(epg):
            for j in range(i + 1, epg):
                pair = rows[g * epg + i] + rows[g * epg + j]
                best = pair if best is None else jnp.maximum(best, pair)
        return best

    best_g = jnp.zeros((1, tr), jnp.int32)
    best_v = group_score(0)
    for g in range(1, N_EXPERT_GROUPS):
        gs = group_score(g)
        take = gs > best_v
        best_g = jnp.where(take, g, best_g)
        best_v = jnp.where(take, gs, best_v)

    def pick(table, r):
        out = table[r]
        for g in range(1, N_EXPERT_GROUPS):
            out = jnp.where(best_g == g, table[g * epg + r], out)
        return out

    in_b = [pick(rows, r) for r in range(epg)]
    in_s = [pick(srow, r) for r in range(epg)]
    l1 = jnp.zeros((1, tr), jnp.int32)
    m1 = in_b[0]
    for r in range(1, epg):
        take = in_b[r] > m1
        l1 = jnp.where(take, r, l1)
        m1 = jnp.where(take, in_b[r], m1)
    l2 = jnp.full((1, tr), -1, jnp.int32)
    m2 = jnp.full((1, tr), -jnp.inf, F32)
    for r in range(epg):
        take = jnp.logical_and(l1 != r, jnp.logical_or(l2 < 0, in_b[r] > m2))
        l2 = jnp.where(take, r, l2)
        m2 = jnp.where(take, in_b[r], m2)
    s1 = in_s[0]
    s2 = in_s[0]
    for r in range(1, epg):
        s1 = jnp.where(l1 == r, in_s[r], s1)
        s2 = jnp.where(l2 == r, in_s[r], s2)
    e1 = best_g * epg + l1
    e2 = best_g * epg + l2
    tot = s1 + s2
    wt_ref[0:1, :] = s1 / tot
    wt_ref[1:2, :] = s2 / tot

    eid = lax.broadcasted_iota(jnp.int32, (N_EXPERTS, tr), 0)
    is1 = eid == e1
    is2 = eid == e2
    memb = jnp.where(jnp.logical_or(is1, is2), 1.0, 0.0)
    prefix = jnp.dot(memb.astype(BF16), upper[...], preferred_element_type=F32) + carry[...]
    rank1 = jnp.sum(jnp.where(is1, prefix, 0.0), axis=0, keepdims=True)
    rank2 = jnp.sum(jnp.where(is2, prefix, 0.0), axis=0, keepdims=True)
    idx_ref[0:1, :] = e1
    idx_ref[1:2, :] = e2
    idx_ref[2:3, :] = rank1.astype(jnp.int32)
    idx_ref[3:4, :] = rank2.astype(jnp.int32)
    new_carry = carry[...] + jnp.sum(memb, axis=1, keepdims=True)
    carry[...] = new_carry
    cnt_ref[...] = new_carry.astype(jnp.int32)


def _route(logits_t, router_bias):
    n_e, t_rows = logits_t.shape
    tr = 512
    return pl.pallas_call(
        functools.partial(_route_kernel, tr),
        out_shape=(jax.ShapeDtypeStruct((4, t_rows), jnp.int32),
                   jax.ShapeDtypeStruct((2, t_rows), F32),
                   jax.ShapeDtypeStruct((n_e, 1), jnp.int32)),
        grid=(t_rows // tr,),
        in_specs=[pl.BlockSpec((n_e, tr), lambda t: (0, t)),
                  pl.BlockSpec((n_e, 1), lambda t: (0, 0))],
        out_specs=(pl.BlockSpec((4, tr), lambda t: (0, t)),
                   pl.BlockSpec((2, tr), lambda t: (0, t)),
                   pl.BlockSpec((n_e, 1), lambda t: (0, 0))),
        scratch_shapes=[pltpu.VMEM((tr, tr), BF16), pltpu.VMEM((n_e, 1), F32)],
        compiler_params=_params(("arbitrary",)),
        name="route",
    )(logits_t, router_bias)


def _row_copy(src, src_row, dst, dst_row, sem):
    s0 = pl.multiple_of(src_row * SUBLANES, SUBLANES)
    d0 = pl.multiple_of(dst_row * SUBLANES, SUBLANES)
    return pltpu.make_async_copy(src.at[pl.ds(s0, SUBLANES), :], dst.at[pl.ds(d0, SUBLANES), :], sem)


def _dispatch_kernel(tm, t_rows, dest_ref, lo_ref, hi_ref, h_ref, xs_hbm, sem, pad_sem):
    i = pl.program_id(0)
    base = i * tm

    @pl.when(i == 0)
    def _():
        n_bits = tm.bit_length() - 1

        def run(copy_op):
            def per_expert(e, carry):
                lo = lo_ref[e]
                n = hi_ref[e] - lo

                def big(j, c):
                    d0 = pl.multiple_of((lo + j * tm) * SUBLANES, SUBLANES)
                    copy_op(pltpu.make_async_copy(h_ref, xs_hbm.at[pl.ds(d0, tm * SUBLANES), :], pad_sem))
                    return c

                lax.fori_loop(0, n >> n_bits, big, 0)
                for bit in range(n_bits - 1, -1, -1):
                    size = (1 << bit) * SUBLANES
                    done = (n >> (bit + 1)) << (bit + 1)

                    @pl.when(((n >> bit) & 1) == 1)
                    def _():
                        d0 = pl.multiple_of((lo + done) * SUBLANES, SUBLANES)
                        copy_op(pltpu.make_async_copy(h_ref.at[pl.ds(0, size), :],
                                                      xs_hbm.at[pl.ds(d0, size), :], pad_sem))
                return carry

            lax.fori_loop(0, N_EXPERTS, per_expert, 0)

        run(lambda cp: cp.start())
        run(lambda cp: cp.wait())

    def body(r, carry):
        _row_copy(h_ref, r, xs_hbm, dest_ref[base + r], sem).start(priority=0)
        _row_copy(h_ref, r, xs_hbm, dest_ref[t_rows + base + r], sem).start(priority=1)
        return carry

    lax.fori_loop(0, tm, body, 0, unroll=8)
    whole = pltpu.make_async_copy(h_ref, xs_hbm.at[pl.ds(0, tm * SUBLANES), :], sem)
    whole.wait()
    whole.wait()


def _dispatch(dest, pad_lo, pad_hi, h2, t_rows, p_rows):
    tm = 1024
    grid_spec = pltpu.PrefetchScalarGridSpec(
        num_scalar_prefetch=3,
        grid=(t_rows // tm,),
        in_specs=[pl.BlockSpec((tm * SUBLANES, LANES), lambda i, *_: (i, 0))],
        out_specs=pl.BlockSpec(memory_space=pl.ANY),
        scratch_shapes=[pltpu.SemaphoreType.DMA, pltpu.SemaphoreType.DMA],
    )
    return pl.pallas_call(
        functools.partial(_dispatch_kernel, tm, t_rows),
        out_shape=jax.ShapeDtypeStruct((p_rows * SUBLANES, LANES), F32),
        grid_spec=grid_spec,
        compiler_params=_params(("arbitrary",)),
        name="moe_dispatch",
    )(dest, pad_lo, pad_hi, h2)


def _combine_kernel(tm, t_rows, n_lat_tiles_per_b, n_b, final, dest_ref, x1_ref, wt_ref, mod_ref, fw_ref,
                    ys_hbm, o_ref, buf, sem):
    i = pl.program_id(0)
    n_steps = pl.num_programs(0)
    d = x1_ref.shape[1]
    n_lt = d // LANES

    def issue(step, slot):
        base = step * tm

        def body(r, carry):
            _row_copy(ys_hbm, dest_ref[base + r], buf.at[slot, 0], r, sem.at[slot]).start(priority=0)
            _row_copy(ys_hbm, dest_ref[t_rows + base + r], buf.at[slot, 1], r, sem.at[slot]).start(priority=1)
            return carry

        lax.fori_loop(0, tm, body, 0, unroll=8)

    @pl.when(i == 0)
    def _():
        issue(0, 0)

    @pl.when(i + 1 < n_steps)
    def _():
        issue(i + 1, (i + 1) % 2)

    slot = i % 2
    for k in range(2):
        pltpu.make_async_copy(ys_hbm.at[pl.ds(0, tm * n_lt), :], buf.at[slot, k], sem.at[slot]).wait()
    bi = jnp.minimum(i // n_lat_tiles_per_b, n_b)
    g2 = mod_ref[0, pl.ds(bi, 1), 5 * d:6 * d]
    wt = wt_ref[...]
    y = (_load_token_tiles(buf.at[slot, 0], tm, n_lt) * wt[:, 0:1]
         + _load_token_tiles(buf.at[slot, 1], tm, n_lt) * wt[:, 1:2])
    out = x1_ref[...] + g2 * y
    if final:
        out = out * lax.rsqrt(jnp.mean(out * out, axis=-1, keepdims=True) + EPS) * fw_ref[...]
    o_ref[...] = out


def _combine(dest, x1, wts_t, mod, layer, final_w, ys, n_b, seq, final):
    t_rows, d = x1.shape
    tm = 512
    n_lt = d // LANES
    grid_spec = pltpu.PrefetchScalarGridSpec(
        num_scalar_prefetch=1,
        grid=(t_rows // tm,),
        in_specs=[pl.BlockSpec((tm, d), lambda i, *_: (i, 0)),
                  pl.BlockSpec((tm, 2), lambda i, *_: (i, 0)),
                  pl.BlockSpec((1,) + mod.shape[1:], lambda i, *_: (layer, 0, 0)),
                  pl.BlockSpec((1, d), lambda i, *_: (0, 0)),
                  pl.BlockSpec(memory_space=pl.ANY)],
        out_specs=pl.BlockSpec((tm, d), lambda i, *_: (i, 0)),
        scratch_shapes=[pltpu.VMEM((2, 2, tm * n_lt, LANES), F32), pltpu.SemaphoreType.DMA((2,))],
    )
    return pl.pallas_call(
        functools.partial(_combine_kernel, tm, t_rows, seq // tm, n_b, final),
        out_shape=jax.ShapeDtypeStruct((t_rows, d), F32),
        grid_spec=grid_spec,
        compiler_params=_params(("arbitrary",)),
        name="moe_combine",
    )(dest, x1, wts_t, mod, final_w, ys)


def _ffn_kernel(n_lt, be_ref, x_ref, wg_ref, wu_ref, wd_ref, o_ref, wg_s, wu_s, wd_s):
    i = pl.program_id(0)
    prev_expert = be_ref[jnp.maximum(i - 1, 0)]

    @pl.when(jnp.logical_or(i == 0, be_ref[i] != prev_expert))
    def _():
        wg_s[...] = wg_ref[0, 0].astype(BF16)
        wu_s[...] = wu_ref[0, 0].astype(BF16)
        wd_s[...] = wd_ref[0, 0].astype(BF16)

    x = _load_token_tiles(x_ref, MOE_BLOCK, n_lt).astype(BF16)
    gate = jnp.dot(x, wg_s[...], preferred_element_type=F32)
    up = jnp.dot(x, wu_s[...], preferred_element_type=F32)
    hid = (_silu(gate) * up).astype(BF16)
    _store_token_tiles(o_ref, jnp.dot(hid, wd_s[...], preferred_element_type=F32))


def _expert_ffn(blk_expert, xsorted, w_gate, w_up, w_down, layer):
    d, de = w_gate.shape[-2:]
    n_lt = d // LANES
    n_blk = xsorted.shape[0] // (MOE_BLOCK * n_lt)
    grid_spec = pltpu.PrefetchScalarGridSpec(
        num_scalar_prefetch=1,
        grid=(n_blk,),
        in_specs=[pl.BlockSpec((MOE_BLOCK * n_lt, LANES), lambda i, be: (i, 0)),
                  pl.BlockSpec((1, 1, d, de), lambda i, be: (layer, be[i], 0, 0)),
                  pl.BlockSpec((1, 1, d, de), lambda i, be: (layer, be[i], 0, 0)),
                  pl.BlockSpec((1, 1, de, d), lambda i, be: (layer, be[i], 0, 0))],
        out_specs=pl.BlockSpec((MOE_BLOCK * n_lt, LANES), lambda i, be: (i, 0)),
        scratch_shapes=[pltpu.VMEM((d, de), BF16), pltpu.VMEM((d, de), BF16), pltpu.VMEM((de, d), BF16)],
    )
    return pl.pallas_call(
        functools.partial(_ffn_kernel, n_lt),
        out_shape=jax.ShapeDtypeStruct(xsorted.shape, F32),
        grid_spec=grid_spec,
        compiler_params=_params(("arbitrary",)),
        name="expert_ffn",
    )(blk_expert, xsorted, w_gate, w_up, w_down)


def _sincos_2d(length, dim):
    rows = length // GRID_W
    quarter = dim // 4
    omega = 1.0 / (POS_BASE ** (jnp.arange(quarter, dtype=F32) / quarter))
    ang_r = jnp.arange(rows, dtype=F32)[:, None] * omega
    ang_c = jnp.arange(GRID_W, dtype=F32)[:, None] * omega
    emb_r = jnp.concatenate([jnp.sin(ang_r), jnp.cos(ang_r)], axis=-1)
    emb_c = jnp.concatenate([jnp.sin(ang_c), jnp.cos(ang_c)], axis=-1)
    half = dim // 2
    emb = jnp.concatenate([jnp.broadcast_to(emb_r[:, None, :], (rows, GRID_W, half)),
                           jnp.broadcast_to(emb_c[None, :, :], (rows, GRID_W, half))], axis=-1)
    return emb.reshape(rows * GRID_W, dim)


def _moe(h2, x1, idx, wts, counts, mod, layer, final_w, w_gate, w_up, w_down, n_b, seq, final):
    t_rows, d = x1.shape
    a = 2 * t_rows
    counts = counts[:, 0]
    padded = (counts + MOE_BLOCK - 1) // MOE_BLOCK * MOE_BLOCK
    pad_end = jnp.cumsum(padded)
    pad_start = pad_end - padded
    n_blk = (a + N_EXPERTS * (MOE_BLOCK - 1) + MOE_BLOCK - 1) // MOE_BLOCK
    p_rows = n_blk * MOE_BLOCK
    blk_start = jnp.arange(n_blk, dtype=jnp.int32) * MOE_BLOCK
    blk_expert = jnp.minimum(jnp.sum(blk_start[:, None] >= pad_end[None, :], axis=1),
                             N_EXPERTS - 1).astype(jnp.int32)
    e_ids = jnp.arange(N_EXPERTS, dtype=jnp.int32)
    seg_start = jnp.sum(jnp.where(idx[0:2, :, None] == e_ids, pad_start, 0), axis=-1)
    dest = (seg_start + idx[2:4]).reshape(-1).astype(jnp.int32)
    pad_lo = (pad_start + counts).astype(jnp.int32)
    pad_hi = pad_end.at[N_EXPERTS - 1].set(p_rows).astype(jnp.int32)
    xsorted = _dispatch(dest, pad_lo, pad_hi, h2, t_rows, p_rows)
    ys = _expert_ffn(blk_expert, xsorted, w_gate, w_up, w_down, layer)
    return _combine(dest, x1, wts.T, mod, layer, final_w, ys, n_b, seq, final)


def kernel(x, c, ctx, c_ctx, ada_w, ada_b, norm_mix_w, norm_ffn_w, w_in, conv_w, a_log, dt_bias,
           dn_norm_w, ft_norm_w, w_out, router_w, router_bias, w_gate, w_up, w_down, final_norm_w):
    n_b, seq, d = x.shape
    ctx_len = ctx.shape[1]
    depth = ada_w.shape[0]
    n_lat = n_b * seq
    n_tok = n_lat + n_b * ctx_len
    ft_off = 4 * DN_WIDTH + N_GATE_COLS

    xs = _embed(x.reshape(n_lat, d), _sincos_2d(seq, d), ctx.reshape(n_b * ctx_len, d))

    mod_rows = -(-(n_b + 1) // 8) * 8
    cvec = jnp.zeros((mod_rows, d), F32).at[:n_b].set(c).at[n_b].set(c_ctx)
    mod = _ada(cvec, ada_w, ada_b)

    cos_l, sin_l = _dft_tables(seq, seq // 2)
    cos_c, sin_c = _dft_tables(ctx_len, ctx_len // 2)
    cc, sc = _dft_tables(FT_GROUP_DIM)
    dft_cs = jnp.concatenate([cc, sc], axis=1)
    assert d == SUBLANES * LANES, "token tiles assume one (8, 128) tile per token"
    final_w = final_norm_w.reshape(1, d)
    router_wt = router_w.T
    rbias = router_bias.reshape(N_EXPERTS, 1)

    col = jnp.arange(N_GATE_COLS)
    col_head, col_dir, col_ab = col // 4, (col // 2) % 2, col % 2
    gate_src = col_dir * (2 * N_DN_HEADS) + col_ab * N_DN_HEADS + col_head
    is_a = (col_ab == 0).astype(F32)
    is_bwd = (col_dir == 1).astype(F32)

    gate_w = w_in[:, :, 4 * DN_WIDTH:ft_off][:, :, gate_src]
    w_pack = _pack_in_weights(w_in)
    wabt = jnp.swapaxes(gate_w, 1, 2).astype(BF16)
    neg_a = -jnp.exp(a_log)[:, col_dir, col_head] * is_a
    dtb = dt_bias[:, col_dir, col_head] * is_a
    gpart = jnp.stack([neg_a, dtb, jnp.broadcast_to(is_a, neg_a.shape),
                       jnp.broadcast_to(is_bwd, neg_a.shape)], axis=2)
    w_out_b = w_out.astype(BF16)
    norm_mix = norm_mix_w.reshape(depth, 1, d)

    for i in range(depth):
        last = i == depth - 1
        qkv, z, xc, xsn, gcol, grow = _in_proj(xs, mod, i, norm_mix, w_pack, wabt, gpart, dft_cs, n_b, seq)
        dn_l, dn_c = _deltanet(qkv, z, gcol, grow, conv_w[i], dn_norm_w[i].reshape(1, DN_HEAD_DIM),
                               n_b, seq, ctx_len)
        fnw = ft_norm_w[i].reshape(1, FT_WIDTH)
        ft_l = _fourier(xc, xsn, cos_l, sin_l, fnw, n_b, seq, 0)
        if last:
            ft_c, rows = dn_c, n_lat
        else:
            ft_c = _fourier(xc, xsn, cos_c, sin_c, fnw, n_b, ctx_len, n_lat // ctx_len)
            rows = n_tok
        x1, h2, logits_t = _out_proj(xs, dn_l, dn_c, ft_l, ft_c, w_out_b, mod, i,
                                     norm_ffn_w[i].reshape(1, d), router_wt, n_b, seq, rows)
        idx, wts, counts = _route(logits_t, rbias)
        xs = _moe(h2, x1, idx, wts, counts, mod, i, final_w, w_gate, w_up, w_down, n_b, seq, last)
    return xs.reshape(n_b, seq, d)
```

```python
import functools
import math

import jax
import jax.numpy as jnp
from jax import lax
from jax.experimental import pallas as pl
from jax.experimental.pallas import tpu as pltpu

F32 = jnp.float32
BF16 = jnp.bfloat16

GRID_W = 64
N_DN_HEADS = 4
DN_HEAD_DIM = 128
DN_WIDTH = N_DN_HEADS * DN_HEAD_DIM
N_FT_GROUPS = 4
FT_GROUP_DIM = 128
FT_WIDTH = N_FT_GROUPS * FT_GROUP_DIM
CONV_K = 5
CHUNK = 64
N_EXPERTS = 16
N_EXPERT_GROUPS = 4
EXPERTS_PER_GROUP = N_EXPERTS // N_EXPERT_GROUPS
MOE_BLOCK = 512
POS_BASE = 10000.0
EPS = 1e-6
N_GATE_COLS = 4 * N_DN_HEADS
CONV_BLOCK = 256
DN_UNROLL = 16

LANES = 128
SUBLANES = 8
VMEM_LIMIT = 56 * 1024 * 1024


def _params(sem, vmem=VMEM_LIMIT):
    return pltpu.CompilerParams(dimension_semantics=sem, vmem_limit_bytes=vmem)


def _sigmoid(x):
    return 1.0 / (1.0 + jnp.exp(-x))


def _silu(x):
    return x * _sigmoid(x)


def _softplus(x):
    return jnp.maximum(x, 0.0) + jnp.log(1.0 + jnp.exp(-jnp.abs(x)))


def _ada_kernel(c_ref, w_ref, b_ref, o_ref):
    act = _silu(c_ref[...])
    o_ref[0] = jnp.dot(act, w_ref[0], preferred_element_type=F32,
                       precision=lax.Precision.HIGHEST) + b_ref[0]


def _ada(cvec, ada_w, ada_b):
    depth, d, n = ada_w.shape
    rows = cvec.shape[0]
    tn = 1024
    return pl.pallas_call(
        _ada_kernel,
        out_shape=jax.ShapeDtypeStruct((depth, rows, n), F32),
        grid=(depth, n // tn),
        in_specs=[pl.BlockSpec((rows, d), lambda i, j: (0, 0)),
                  pl.BlockSpec((1, d, tn), lambda i, j: (i, 0, j)),
                  pl.BlockSpec((1, 1, tn), lambda i, j: (i, 0, j))],
        out_specs=pl.BlockSpec((1, rows, tn), lambda i, j: (i, 0, j)),
        compiler_params=_params(("parallel", "parallel")),
        name="ada_mod",
    )(cvec, ada_w, ada_b.reshape(depth, 1, n))


def _embed_kernel(n_lat_tiles, x_ref, pos_ref, ctx_ref, o_ref):
    t = pl.program_id(0)
    o_ref[...] = jnp.where(t < n_lat_tiles, x_ref[...] + pos_ref[...], ctx_ref[...])


def _embed(x2d, pos, ctx2d):
    n_lat, d = x2d.shape
    seq = pos.shape[0]
    tm = 512
    n_lat_tiles = n_lat // tm
    n_tok = n_lat + ctx2d.shape[0]
    return pl.pallas_call(
        functools.partial(_embed_kernel, n_lat_tiles),
        out_shape=jax.ShapeDtypeStruct((n_tok, d), F32),
        grid=(n_tok // tm,),
        in_specs=[pl.BlockSpec((tm, d), lambda t: (jnp.minimum(t, n_lat_tiles - 1), 0)),
                  pl.BlockSpec((tm, d), lambda t: (t % (seq // tm), 0)),
                  pl.BlockSpec((tm, d), lambda t: (jnp.maximum(t - n_lat_tiles, 0), 0))],
        out_specs=pl.BlockSpec((tm, d), lambda t: (t, 0)),
        compiler_params=_params(("parallel",)),
        name="embed",
    )(x2d, pos, ctx2d)


def _pack_w_kernel(w_ref, o_ref):
    n_main = 4 * DN_WIDTH
    o_ref[0, :, :n_main] = w_ref[0, :, :n_main].astype(BF16)
    o_ref[0, :, n_main:] = w_ref[0, :, n_main + N_GATE_COLS:].astype(BF16)


def _pack_in_weights(w_in):
    depth, d, n_in = w_in.shape
    n_out = n_in - N_GATE_COLS
    tr = 256
    return pl.pallas_call(
        _pack_w_kernel,
        out_shape=jax.ShapeDtypeStruct((depth, d, n_out), BF16),
        grid=(depth, d // tr),
        in_specs=[pl.BlockSpec((1, tr, n_in), lambda i, r: (i, r, 0))],
        out_specs=pl.BlockSpec((1, tr, n_out), lambda i, r: (i, r, 0)),
        compiler_params=_params(("parallel", "parallel")),
        name="pack_w_in",
    )(w_in)


def _seg_scan(x, pos, axis, reverse):
    n = x.shape[axis]
    s = 1
    while s < CHUNK:
        if reverse:
            shifted = pltpu.roll(x, n - s, axis)
            x = x + jnp.where(pos < CHUNK - s, shifted, 0.0)
        else:
            shifted = pltpu.roll(x, s, axis)
            x = x + jnp.where(pos >= s, shifted, 0.0)
        s *= 2
    return x


def _in_proj_kernel(tm, n_lat_tiles_per_b, n_b, x_ref, mod_ref, nw_ref, w_ref, wabt_ref, gpart_ref,
                    dft_ref, qkv_ref, z_ref, xc_ref, xs_ref, gcol_ref, grow_ref):
    t = pl.program_id(0)
    bi = jnp.minimum(t // n_lat_tiles_per_b, n_b)
    d = x_ref.shape[1]
    x = x_ref[...]
    y = x * lax.rsqrt(jnp.mean(x * x, axis=-1, keepdims=True) + EPS) * nw_ref[0]
    shift = mod_ref[0, pl.ds(bi, 1), 0:d]
    scale = mod_ref[0, pl.ds(bi, 1), d:2 * d]
    h = (y * (1.0 + scale) + shift).astype(BF16)
    z_off, ft_off = 3 * DN_WIDTH, 4 * DN_WIDTH

    for j in range(3):
        cs = slice(j * DN_WIDTH, (j + 1) * DN_WIDTH)
        qkv_ref[:, cs] = jnp.dot(h, w_ref[0, :, cs], preferred_element_type=F32)
    z_ref[...] = jnp.dot(h, w_ref[0, :, z_off:ft_off], preferred_element_type=F32)

    ft = jnp.dot(h, w_ref[0, :, ft_off:], preferred_element_type=F32).astype(BF16)
    for g in range(N_FT_GROUPS):
        cs = slice(g * FT_GROUP_DIM, (g + 1) * FT_GROUP_DIM)
        cssn = jnp.dot(ft[:, cs], dft_ref[...], preferred_element_type=F32)
        xc_ref[:, cs] = cssn[:, :FT_GROUP_DIM].astype(BF16)
        xs_ref[:, cs] = cssn[:, FT_GROUP_DIM:].astype(BF16)

    abt = lax.dot_general(wabt_ref[0], h, (((1,), (1,)), ((), ())), preferred_element_type=F32)

    def gates(v, par, axis):
        neg_a, dtb, is_a, is_bwd = par
        g = neg_a * _softplus(v + dtb)
        pos = lax.broadcasted_iota(jnp.int32, v.shape, axis) % CHUNK
        fwd = _seg_scan(g, pos, axis, reverse=False)
        bwd = _seg_scan(g, pos, axis, reverse=True)
        cum = jnp.where(is_bwd > 0.5, bwd, fwd)
        return jnp.where(is_a > 0.5, cum, _sigmoid(v))

    gpt = gpart_ref[0]
    gt = gates(abt, (gpt[:, 0:1], gpt[:, 1:2], gpt[:, 2:3], gpt[:, 3:4]), 1)
    gc = gt.T
    for hh in range(N_DN_HEADS):
        gcol_ref[hh] = gc[:, 4 * hh:4 * hh + 4]
        for j in range(tm // CHUNK):
            cs = slice(j * CHUNK, (j + 1) * CHUNK)
            grow_ref[hh, j] = jnp.concatenate([gt[4 * hh:4 * hh + 1, cs], gt[4 * hh + 2:4 * hh + 3, cs]], axis=1)


def _in_proj(xs, mod_i, layer, norm_w, w_pack, wabt, gpart, dft_cs, n_b, seq):
    t_rows, d = xs.shape
    tm = 512
    n_tiles = t_rows // tm
    kern = functools.partial(_in_proj_kernel, tm, seq // tm, n_b)
    const = lambda *shape: pl.BlockSpec(shape, lambda t: tuple(0 for _ in shape))
    per_layer = lambda a: pl.BlockSpec((1,) + a.shape[1:], lambda t: (layer,) + (0,) * (a.ndim - 1))
    rows = lambda w: pl.BlockSpec((tm, w), lambda t: (t, 0))
    return pl.pallas_call(
        kern,
        out_shape=(jax.ShapeDtypeStruct((t_rows, 3 * DN_WIDTH), F32),
                   jax.ShapeDtypeStruct((t_rows, DN_WIDTH), F32),
                   jax.ShapeDtypeStruct((t_rows, FT_WIDTH), BF16),
                   jax.ShapeDtypeStruct((t_rows, FT_WIDTH), BF16),
                   jax.ShapeDtypeStruct((N_DN_HEADS, t_rows, 4), F32),
                   jax.ShapeDtypeStruct((N_DN_HEADS, t_rows // CHUNK, 1, 2 * CHUNK), F32)),
        grid=(n_tiles,),
        in_specs=[rows(d), per_layer(mod_i), per_layer(norm_w), per_layer(w_pack), per_layer(wabt),
                  per_layer(gpart), const(FT_GROUP_DIM, 2 * FT_GROUP_DIM)],
        out_specs=(rows(3 * DN_WIDTH), rows(DN_WIDTH), rows(FT_WIDTH), rows(FT_WIDTH),
                   pl.BlockSpec((N_DN_HEADS, tm, 4), lambda t: (0, t, 0)),
                   pl.BlockSpec((N_DN_HEADS, tm // CHUNK, 1, 2 * CHUNK), lambda t: (0, t, 0, 0))),
        compiler_params=_params(("parallel",)),
        name="in_proj",
    )(xs, mod_i, norm_w, w_pack, wabt, gpart, dft_cs)


def _conv_block(src_ref, cw, r0, n_rows, mode):
    blk = CONV_BLOCK
    half_k = CONV_K // 2
    if r0 >= half_k and r0 + blk + half_k <= n_rows:
        acc = src_ref[r0 - half_k:r0 - half_k + blk, :] * cw[0:1, :]
        for j in range(1, CONV_K):
            acc = acc + src_ref[r0 + j - half_k:r0 + j - half_k + blk, :] * cw[j:j + 1, :]
    else:
        halo = 8
        n_win = blk + 2 * halo
        zeros = jnp.zeros((halo, DN_HEAD_DIM), F32)
        prev = src_ref[r0 - halo:r0, :] if r0 > 0 else zeros
        nxt = src_ref[r0 + blk:r0 + blk + halo, :] if r0 + blk < n_rows else zeros
        win = jnp.concatenate([prev, src_ref[r0:r0 + blk, :], nxt], axis=0)
        acc = jnp.zeros((blk, DN_HEAD_DIM), F32)
        for j in range(CONV_K):
            shift = (half_k - j) % n_win
            rolled = win if shift == 0 else pltpu.roll(win, shift, 0)
            acc = acc + rolled[halo:halo + blk] * cw[j:j + 1, :]
    y = _silu(acc)
    if mode == "v":
        return y
    inv = lax.rsqrt(jnp.sum(y * y, axis=-1, keepdims=True) + EPS)
    if mode == "q":
        inv = inv * (DN_HEAD_DIM ** -0.5)
    return y * inv


def _blockdiag(x, isb):
    xb = x.astype(BF16)
    keep_b = jnp.where(isb, 1.0, 0.0).astype(BF16)
    keep_f = jnp.where(isb, 0.0, 1.0).astype(BF16)
    return jnp.concatenate([xb * keep_f, xb * keep_b], axis=0)


def _blockdiag_wide(x):
    w = x.shape[1] // 2
    zero = jnp.zeros((x.shape[0], w), BF16)
    xb = x.astype(BF16)
    return jnp.concatenate([jnp.concatenate([xb[:, :w], zero], axis=1),
                            jnp.concatenate([zero, xb[:, w:]], axis=1)], axis=0)


def _tri_inverse_dual(a_list, eye, xor, isb):
    mm = lambda x, y: jnp.dot(x.astype(BF16), _blockdiag(y, isb), preferred_element_type=F32)
    ad = [jnp.where((xor >> 3) == 0, a, 0.0) for a in a_list]
    a2 = [mm(x, x) for x in ad]
    a4 = [mm(x, x) for x in a2]
    t = [eye - x for x in ad]
    t = [x + mm(x, y) for x, y in zip(t, a2)]
    t = [x + mm(x, y) for x, y in zip(t, a4)]
    for s in (3, 4, 5):
        off = [jnp.where((xor >> s) == 1, a, 0.0) for a in a_list]
        to = [mm(x, y) for x, y in zip(t, off)]
        t = [x - mm(y, x) for x, y in zip(t, to)]
    return t


def _chunks_local(loaded):
    hd = DN_HEAD_DIM
    n = len(loaded)
    q = [x[0] for x in loaded]
    k = [x[1] for x in loaded]
    v = [x[2] for x in loaded]
    wide = lambda col: jnp.broadcast_to(col, (CHUNK, hd))
    gf = [wide(x[3][:, 0:1]) for x in loaded]
    bf_ = [wide(x[3][:, 1:2]) for x in loaded]
    gb = [wide(x[3][:, 2:3]) for x in loaded]
    bb = [wide(x[3][:, 3:4]) for x in loaded]
    grow = [x[4] for x in loaded]
    row = lax.broadcasted_iota(jnp.int32, (CHUNK, 2 * CHUNK), 0)
    lane = lax.broadcasted_iota(jnp.int32, (CHUNK, 2 * CHUNK), 1)
    jl = lane & (CHUNK - 1)
    isb = lane >= CHUNK
    delta = jnp.where(isb, jl - row, row - jl)
    xor = row ^ jl
    eye = jnp.where(delta == 0, 1.0, 0.0)
    nt = (((1,), (1,)), ((), ()))
    k2 = [jnp.concatenate([x, x], axis=0).astype(BF16) for x in k]
    kkd = [lax.dot_general(k[i].astype(BF16), k2[i], nt, preferred_element_type=F32) for i in range(n)]
    qkd = [lax.dot_general(q[i].astype(BF16), k2[i], nt, preferred_element_type=F32) for i in range(n)]
    dec = [jnp.where(delta >= 0, jnp.exp(jnp.where(isb, gb[i], gf[i]) - grow[i]), 0.0) for i in range(n)]
    a_mat = [jnp.where(delta > 0, kkd[i] * jnp.where(isb, bb[i], bf_[i]) * dec[i], 0.0) for i in range(n)]
    t_inv = _tri_inverse_dual(a_mat, eye, xor, isb)
    egf = [jnp.exp(x) for x in gf]
    egb = [jnp.exp(x) for x in gb]
    rhs = [jnp.concatenate([v[i] * bf_[i], k[i] * (bf_[i] * egf[i]), v[i] * bb[i], k[i] * (bb[i] * egb[i])],
                           axis=1) for i in range(n)]
    sol = [jnp.dot(t_inv[i].astype(BF16), _blockdiag_wide(rhs[i]), preferred_element_type=F32)
           for i in range(n)]
    bd_sol = [_blockdiag_wide(x) for x in sol]
    r1 = [jnp.dot((qkd[i] * dec[i]).astype(BF16), bd_sol[i], preferred_element_type=F32)
          for i in range(n)]
    glf = [x[CHUNK - 1:CHUNK, :] for x in gf]
    glb = [x[0:1, :] for x in gb]
    kdec = [jnp.concatenate([k[i] * jnp.exp(glf[i] - gf[i]), k[i] * jnp.exp(glb[i] - gb[i])], axis=0)
            for i in range(n)]
    r2 = [jnp.dot(kdec[i].T.astype(BF16), bd_sol[i], preferred_element_type=F32)
          for i in range(n)]
    out = []
    for i in range(n):
        o_loc = r1[i][:, 0:hd] + r1[i][:, 2 * hd:3 * hd]
        qt = jnp.concatenate([q[i] * egf[i] - r1[i][:, hd:2 * hd], q[i] * egb[i] - r1[i][:, 3 * hd:]],
                             axis=1).astype(BF16)
        nn = jnp.concatenate([r2[i][:, 0:hd], r2[i][:, 2 * hd:3 * hd]], axis=1)
        kw = jnp.concatenate([r2[i][:, hd:2 * hd], r2[i][:, 3 * hd:]], axis=1).astype(BF16)
        ge = jnp.concatenate([jnp.exp(glf[i]), jnp.exp(glb[i])], axis=0)
        out.append((o_loc, qt, nn, kw, ge))
    return out


def _state_step(c, d, oacc, qt_ref, kw_ref, nn_ref, ge_ref, s_ref):
    hd = DN_HEAD_DIM
    r0 = pl.multiple_of(c * CHUNK, CHUNK)
    cs = slice(d * hd, (d + 1) * hd)
    s = s_ref[d]
    lhs = jnp.concatenate([qt_ref[c, :, cs], kw_ref[c, :, cs]], axis=0)
    r = jnp.dot(lhs, s.astype(BF16), preferred_element_type=F32)
    oacc[pl.ds(r0, CHUNK), :] += r[:CHUNK]
    s_ref[d] = s * ge_ref[c, d:d + 1, :] + nn_ref[c, :, cs] - r[CHUNK:]


def _dn_kernel(seq, ctx_len,
               ql_ref, kl_ref, vl_ref, qc_ref, kc_ref, vc_ref, cwq_ref, cwk_ref, cwv_ref,
               zl_ref, zc_ref, gcl_ref, gcc_ref, grl_ref, grc_ref, nw_ref,
               ol_ref, oc_ref,
               qn, kn, vn, oacc, qt_s, kw_s, nn_s, ge_s, s_s):
    nw = nw_ref[...]
    s_s[...] = jnp.zeros_like(s_s)

    def segment(n_rows, q_ref, k_ref, v_ref, z_ref, gcol_ref, grow_ref, out_ref):
        n_chunks = n_rows // CHUNK
        unroll = math.gcd(DN_UNROLL, n_chunks)
        g_rows = unroll * CHUNK
        n_groups = n_chunks // unroll

        def prep(g, slot):
            for b0 in range(0, g_rows, CONV_BLOCK):
                r0 = g * g_rows + b0
                for src, cwr, dst, mode in ((q_ref, cwq_ref, qn, "q"), (k_ref, cwk_ref, kn, "k"),
                                            (v_ref, cwv_ref, vn, "v")):
                    dst[slot, b0:b0 + CONV_BLOCK, :] = _conv_block(src, cwr[...], r0, n_rows, mode)

        prep(0, 0)
        for g in range(n_groups):
            slot = g % 2
            chunks = [g * unroll + j for j in range(unroll)]
            tile = lambda ref, j: ref[slot, j * CHUNK:(j + 1) * CHUNK, :]
            loaded = [(tile(qn, j), tile(kn, j), tile(vn, j),
                       gcol_ref[0, c * CHUNK:(c + 1) * CHUNK, :], grow_ref[0, c])
                      for j, c in enumerate(chunks)]
            if g + 1 < n_groups:
                prep(g + 1, 1 - slot)
            results = _chunks_local(loaded)
            for c, (o_loc, qt, nn, kw, ge) in zip(chunks, results):
                oacc[c * CHUNK:(c + 1) * CHUNK, :] = o_loc
                qt_s[c] = qt
                nn_s[c] = nn
                kw_s[c] = kw
                ge_s[c] = ge

        def step(s, carry):
            _state_step(s, 0, oacc, qt_s, kw_s, nn_s, ge_s, s_s)
            _state_step(n_chunks - 1 - s, 1, oacc, qt_s, kw_s, nn_s, ge_s, s_s)
            return carry

        lax.fori_loop(0, n_chunks, step, 0)

        blk = 256

        def fin(i, carry):
            r0 = pl.multiple_of(i * blk, blk)
            o = oacc[pl.ds(r0, blk), :]
            o = o * lax.rsqrt(jnp.mean(o * o, axis=-1, keepdims=True) + EPS) * nw
            out_ref[pl.ds(r0, blk), :] = (o * _silu(z_ref[pl.ds(r0, blk), :])).astype(out_ref.dtype)
            return carry

        lax.fori_loop(0, n_rows // blk, fin, 0)

    segment(ctx_len, qc_ref, kc_ref, vc_ref, zc_ref, gcc_ref, grc_ref, oc_ref)
    segment(seq, ql_ref, kl_ref, vl_ref, zl_ref, gcl_ref, grl_ref, ol_ref)


def _deltanet(qkv, z, gcol, grow, conv_w, dn_norm_w, n_b, seq, ctx_len):
    hd = DN_HEAD_DIM
    nc = seq // CHUNK
    cb = n_b * seq // ctx_len
    lat = lambda off: pl.BlockSpec((seq, hd), lambda b, h: (b, h + off))
    ctx = lambda off: pl.BlockSpec((ctx_len, hd), lambda b, h: (cb + b, h + off))
    cws = lambda off: pl.BlockSpec((CONV_K, hd), lambda b, h: (0, h + off))
    kern = functools.partial(_dn_kernel, seq, ctx_len)
    return pl.pallas_call(
        kern,
        out_shape=(jax.ShapeDtypeStruct((n_b * seq, DN_WIDTH), BF16),
                   jax.ShapeDtypeStruct((n_b * ctx_len, DN_WIDTH), BF16)),
        grid=(n_b, N_DN_HEADS),
        in_specs=[lat(0), lat(N_DN_HEADS), lat(2 * N_DN_HEADS),
                  ctx(0), ctx(N_DN_HEADS), ctx(2 * N_DN_HEADS),
                  cws(0), cws(N_DN_HEADS), cws(2 * N_DN_HEADS),
                  lat(0), ctx(0),
                  pl.BlockSpec((1, seq, 4), lambda b, h: (h, b, 0)),
                  pl.BlockSpec((1, ctx_len, 4), lambda b, h: (h, cb + b, 0)),
                  pl.BlockSpec((1, nc, 1, 2 * CHUNK), lambda b, h: (h, b, 0, 0)),
                  pl.BlockSpec((1, ctx_len // CHUNK, 1, 2 * CHUNK), lambda b, h: (h, cb + b, 0, 0)),
                  pl.BlockSpec((1, hd), lambda b, h: (0, 0))],
        out_specs=(pl.BlockSpec((seq, hd), lambda b, h: (b, h)),
                   pl.BlockSpec((ctx_len, hd), lambda b, h: (b, h))),
        scratch_shapes=[pltpu.VMEM((2, min(DN_UNROLL, nc) * CHUNK, hd), F32) for _ in range(3)]
        + [pltpu.VMEM((seq, hd), F32),
           pltpu.VMEM((nc, CHUNK, 2 * hd), BF16), pltpu.VMEM((nc, hd, 2 * hd), BF16),
           pltpu.VMEM((nc, hd, 2 * hd), F32), pltpu.VMEM((nc, 2, hd), F32),
           pltpu.VMEM((2, hd, hd), F32)],
        compiler_params=_params(("parallel", "parallel")),
        name="deltanet",
    )(qkv, qkv, qkv, qkv, qkv, qkv, conv_w, conv_w, conv_w, z, z, gcol, gcol, grow, grow, dn_norm_w)


def _reversed_rows(ref, first_row, n_rows, blk, flip_mat):
    n_blk = n_rows // blk
    parts = [jnp.dot(flip_mat, ref[pl.ds(first_row + (n_blk - 1 - i) * blk, blk), :],
                     preferred_element_type=F32) for i in range(n_blk)]
    return jnp.concatenate(parts, axis=0)


def _ft_kernel(tm, scale, cl_ref, sl_ref, xc_ref, xs_ref, nw_ref, o_ref, xcf, xsf, hi_s):
    mt = pl.program_id(1)
    n_m = pl.num_programs(1)
    half = xcf.shape[0]
    ii = lax.broadcasted_iota(jnp.int32, (tm, tm), 0)
    jj = lax.broadcasted_iota(jnp.int32, (tm, tm), 1)
    flip_mat = jnp.where(ii + jj == tm - 1, 1.0, 0.0).astype(BF16)
    row_h = lax.broadcasted_iota(jnp.int32, (half, 1), 0)

    def shifted_reverse(ref, first_row):
        rev = _reversed_rows(ref, first_row, half, tm, flip_mat)
        return jnp.where(row_h == 0, 0.0, pltpu.roll(rev, 1, 0))

    @pl.when(mt == 0)
    def _():
        xcf[...] = (xc_ref[0:half, :].astype(F32) + shifted_reverse(xc_ref, half)).astype(BF16)
        xsf[...] = (xs_ref[0:half, :].astype(F32) - shifted_reverse(xs_ref, half)).astype(BF16)

    a = jnp.dot(cl_ref[...], xcf[...], preferred_element_type=F32)
    b = jnp.dot(sl_ref[...], xsf[...], preferred_element_type=F32)
    x_nyq = xc_ref[half:half + 2 * SUBLANES, :].astype(F32)[0:1, :]
    m_idx = mt * tm + lax.broadcasted_iota(jnp.int32, (tm, 1), 0)
    corr = jnp.where((m_idx & 1) == 0, 1.0, -1.0) * x_nyq
    nw = nw_ref[...]

    def norm(y):
        y = y * scale
        return y * lax.rsqrt(jnp.mean(y * y, axis=-1, keepdims=True) + EPS) * nw

    r0 = pl.multiple_of(mt * tm, tm)
    o_ref[pl.ds(r0, tm), :] = norm(a - b + corr).astype(o_ref.dtype)
    hi_s[pl.ds(r0, tm), :] = norm(a + b + corr).astype(hi_s.dtype)

    @pl.when(mt == n_m - 1)
    def _():
        lane = lax.broadcasted_iota(jnp.int32, (SUBLANES, half), 1)
        alt = jnp.where((lane & 1) == 0, 1.0, -1.0).astype(BF16)
        y_nyq = norm(jnp.dot(alt, xcf[...], preferred_element_type=F32)[0:1, :] + x_nyq)
        upper = jnp.where(row_h == 0, y_nyq, pltpu.roll(_reversed_rows(hi_s, 0, half, tm, flip_mat), 1, 0))
        o_ref[half:, :] = upper.astype(o_ref.dtype)


def _fourier(xc, xs, cos_h, sin_h, ft_norm_w, n_b, length, row_block0):
    half = length // 2
    tm = min(512, half)
    n_m = half // tm
    scale = 1.0 / math.sqrt(length * FT_GROUP_DIM)
    return pl.pallas_call(
        functools.partial(_ft_kernel, tm, scale),
        out_shape=jax.ShapeDtypeStruct((n_b * length, FT_WIDTH), BF16),
        grid=(n_b, n_m),
        in_specs=[pl.BlockSpec((tm, half), lambda b, m: (m, 0)),
                  pl.BlockSpec((tm, half), lambda b, m: (m, 0)),
                  pl.BlockSpec((length, FT_WIDTH), lambda b, m: (row_block0 + b, 0)),
                  pl.BlockSpec((length, FT_WIDTH), lambda b, m: (row_block0 + b, 0)),
                  pl.BlockSpec((1, FT_WIDTH), lambda b, m: (0, 0))],
        out_specs=pl.BlockSpec((length, FT_WIDTH), lambda b, m: (b, 0)),
        scratch_shapes=[pltpu.VMEM((half, FT_WIDTH), BF16), pltpu.VMEM((half, FT_WIDTH), BF16),
                        pltpu.VMEM((half, FT_WIDTH), BF16)],
        compiler_params=_params(("parallel", "arbitrary")),
        name="fourier_seq",
    )(cos_h, sin_h, xc, xs, ft_norm_w)


def _dft_tables(length, n=None):
    m = jnp.arange(length if n is None else n, dtype=jnp.int32)
    ph = (m[:, None] * m[None, :]) % length
    ang = ph.astype(F32) * (2.0 * math.pi / length)
    return jnp.cos(ang).astype(BF16), jnp.sin(ang).astype(BF16)


def _store_token_tiles(ref, val):
    rows, d = val.shape
    n_lt = d // LANES
    for j in range(n_lt):
        ref[pl.ds(j, rows, stride=n_lt), :] = val[:, j * LANES:(j + 1) * LANES]


def _load_token_tiles(ref, rows, n_lt):
    return jnp.concatenate([ref[pl.ds(j, rows, stride=n_lt), :] for j in range(n_lt)], axis=1)


def _out_proj_kernel(n_lat_tiles, n_lat_tiles_per_b, n_b, x_ref, dnl_ref, dnc_ref, ftl_ref, ftc_ref,
                     wdn_ref, wft_ref, mod_ref, nw_ref, rwt_ref, x1_ref, h2_ref, lg_ref):
    t = pl.program_id(0)
    bi = jnp.minimum(t // n_lat_tiles_per_b, n_b)
    d = x_ref.shape[1]
    is_lat = t < n_lat_tiles
    dn = jnp.where(is_lat, dnl_ref[...], dnc_ref[...])
    ft = jnp.where(is_lat, ftl_ref[...], ftc_ref[...])
    mix = (jnp.dot(dn, wdn_ref[0], preferred_element_type=F32)
           + jnp.dot(ft, wft_ref[0], preferred_element_type=F32))
    g1 = mod_ref[0, pl.ds(bi, 1), 2 * d:3 * d]
    sh2 = mod_ref[0, pl.ds(bi, 1), 3 * d:4 * d]
    sc2 = mod_ref[0, pl.ds(bi, 1), 4 * d:5 * d]
    x1 = x_ref[...] + g1 * mix
    x1_ref[...] = x1
    y = x1 * lax.rsqrt(jnp.mean(x1 * x1, axis=-1, keepdims=True) + EPS) * nw_ref[...]
    h2 = y * (1.0 + sc2) + sh2
    _store_token_tiles(h2_ref, h2)
    lg_ref[...] = lax.dot_general(rwt_ref[...], h2, (((1,), (1,)), ((), ())),
                                  preferred_element_type=F32, precision=lax.Precision.HIGHEST)


def _out_proj(xs, dn_l, dn_c, ft_l, ft_c, w_out, mod_i, layer, norm_w, router_wt, n_b, seq, t_rows):
    d = xs.shape[1]
    tm = 512
    n_lt = d // LANES
    n_lat_tiles = n_b * seq // tm
    kern = functools.partial(_out_proj_kernel, n_lat_tiles, seq // tm, n_b)
    const = lambda *shape: pl.BlockSpec(shape, lambda t: tuple(0 for _ in shape))
    rows = lambda w: pl.BlockSpec((tm, w), lambda t: (t, 0))
    lat = lambda w: pl.BlockSpec((tm, w), lambda t: (jnp.minimum(t, n_lat_tiles - 1), 0))
    ctx = lambda w: pl.BlockSpec((tm, w), lambda t: (jnp.maximum(t - n_lat_tiles, 0), 0))
    return pl.pallas_call(
        kern,
        out_shape=(jax.ShapeDtypeStruct((t_rows, d), F32),
                   jax.ShapeDtypeStruct((t_rows * n_lt, LANES), F32),
                   jax.ShapeDtypeStruct((N_EXPERTS, t_rows), F32)),
        grid=(t_rows // tm,),
        in_specs=[rows(d), lat(DN_WIDTH), ctx(DN_WIDTH), lat(FT_WIDTH), ctx(FT_WIDTH),
                  pl.BlockSpec((1, DN_WIDTH, d), lambda t: (layer, 0, 0)),
                  pl.BlockSpec((1, FT_WIDTH, d), lambda t: (layer, DN_WIDTH // FT_WIDTH, 0)),
                  pl.BlockSpec((1,) + mod_i.shape[1:], lambda t: (layer, 0, 0)),
                  const(1, d), const(N_EXPERTS, d)],
        out_specs=(rows(d), pl.BlockSpec((tm * n_lt, LANES), lambda t: (t, 0)),
                   pl.BlockSpec((N_EXPERTS, tm), lambda t: (0, t))),
        compiler_params=_params(("parallel",)),
        name="out_proj",
    )(xs, dn_l, dn_c, ft_l, ft_c, w_out, w_out, mod_i, norm_w, router_wt)


def _route_kernel(tr, lg_ref, bias_ref, idx_ref, wt_ref, cnt_ref, upper, carry):
    step = pl.program_id(0)

    @pl.when(step == 0)
    def _():
        a = lax.broadcasted_iota(jnp.int32, (tr, tr), 0)
        b = lax.broadcasted_iota(jnp.int32, (tr, tr), 1)
        upper[...] = jnp.where(a < b, 1.0, 0.0).astype(BF16)
        carry[...] = jnp.zeros_like(carry)

    scores = _sigmoid(lg_ref[...])
    biased = scores + bias_ref[...]
    rows = [biased[r:r + 1, :] for r in range(N_EXPERTS)]
    srow = [scores[r:r + 1, :] for r in range(N_EXPERTS)]
    epg = EXPERTS_PER_GROUP

    def group_score(g):
        best = None
        for i in range(epg):
            for j in range(i + 1, epg):
                pair = rows[g * epg + i] + rows[g * epg + j]
                best = pair if best is None else jnp.maximum(best, pair)
        return best

    best_g = jnp.zeros((1, tr), jnp.int32)
    best_v = group_score(0)
    for g in range(1, N_EXPERT_GROUPS):
        gs = group_score(g)
        take = gs > best_v
        best_g = jnp.where(take, g, best_g)
        best_v = jnp.where(take, gs, best_v)

    def pick(table, r):
        out = table[r]
        for g in range(1, N_EXPERT_GROUPS):
            out = jnp.where(best_g == g, table[g * epg + r], out)
        return out

    in_b = [pick(rows, r) for r in range(epg)]
    in_s = [pick(srow, r) for r in range(epg)]
    l1 = jnp.zeros((1, tr), jnp.int32)
    m1 = in_b[0]
    for r in range(1, epg):
        take = in_b[r] > m1
        l1 = jnp.where(take, r, l1)
        m1 = jnp.where(take, in_b[r], m1)
    l2 = jnp.full((1, tr), -1, jnp.int32)
    m2 = jnp.full((1, tr), -jnp.inf, F32)
    for r in range(epg):
        take = jnp.logical_and(l1 != r, jnp.logical_or(l2 < 0, in_b[r] > m2))
        l2 = jnp.where(take, r, l2)
        m2 = jnp.where(take, in_b[r], m2)
    s1 = in_s[0]
    s2 = in_s[0]
    for r in range(1, epg):
        s1 = jnp.where(l1 == r, in_s[r], s1)
        s2 = jnp.where(l2 == r, in_s[r], s2)
    e1 = best_g * epg + l1
    e2 = best_g * epg + l2
    tot = s1 + s2
    wt_ref[0:1, :] = s1 / tot
    wt_ref[1:2, :] = s2 / tot

    eid = lax.broadcasted_iota(jnp.int32, (N_EXPERTS, tr), 0)
    is1 = eid == e1
    is2 = eid == e2
    memb = jnp.where(jnp.logical_or(is1, is2), 1.0, 0.0)
    prefix = jnp.dot(memb.astype(BF16), upper[...], preferred_element_type=F32) + carry[...]
    rank1 = jnp.sum(jnp.where(is1, prefix, 0.0), axis=0, keepdims=True)
    rank2 = jnp.sum(jnp.where(is2, prefix, 0.0), axis=0, keepdims=True)
    idx_ref[0:1, :] = e1
    idx_ref[1:2, :] = e2
    idx_ref[2:3, :] = rank1.astype(jnp.int32)
    idx_ref[3:4, :] = rank2.astype(jnp.int32)
    new_carry = carry[...] + jnp.sum(memb, axis=1, keepdims=True)
    carry[...] = new_carry
    cnt_ref[...] = new_carry.astype(jnp.int32)


def _route(logits_t, router_bias):
    n_e, t_rows = logits_t.shape
    tr = 512
    return pl.pallas_call(
        functools.partial(_route_kernel, tr),
        out_shape=(jax.ShapeDtypeStruct((4, t_rows), jnp.int32),
                   jax.ShapeDtypeStruct((2, t_rows), F32),
                   jax.ShapeDtypeStruct((n_e, 1), jnp.int32)),
        grid=(t_rows // tr,),
        in_specs=[pl.BlockSpec((n_e, tr), lambda t: (0, t)),
                  pl.BlockSpec((n_e, 1), lambda t: (0, 0))],
        out_specs=(pl.BlockSpec((4, tr), lambda t: (0, t)),
                   pl.BlockSpec((2, tr), lambda t: (0, t)),
                   pl.BlockSpec((n_e, 1), lambda t: (0, 0))),
        scratch_shapes=[pltpu.VMEM((tr, tr), BF16), pltpu.VMEM((n_e, 1), F32)],
        compiler_params=_params(("arbitrary",)),
        name="route",
    )(logits_t, router_bias)


def _row_copy(src, src_row, dst, dst_row, sem):
    s0 = pl.multiple_of(src_row * SUBLANES, SUBLANES)
    d0 = pl.multiple_of(dst_row * SUBLANES, SUBLANES)
    return pltpu.make_async_copy(src.at[pl.ds(s0, SUBLANES), :], dst.at[pl.ds(d0, SUBLANES), :], sem)


def _dispatch_kernel(tm, t_rows, dest_ref, lo_ref, hi_ref, h_ref, xs_hbm, sem, pad_sem):
    i = pl.program_id(0)
    base = i * tm

    @pl.when(i == 0)
    def _():
        n_bits = tm.bit_length() - 1

        def run(copy_op):
            def per_expert(e, carry):
                lo = lo_ref[e]
                n = hi_ref[e] - lo

                def big(j, c):
                    d0 = pl.multiple_of((lo + j * tm) * SUBLANES, SUBLANES)
                    copy_op(pltpu.make_async_copy(h_ref, xs_hbm.at[pl.ds(d0, tm * SUBLANES), :], pad_sem))
                    return c

                lax.fori_loop(0, n >> n_bits, big, 0)
                for bit in range(n_bits - 1, -1, -1):
                    size = (1 << bit) * SUBLANES
                    done = (n >> (bit + 1)) << (bit + 1)

                    @pl.when(((n >> bit) & 1) == 1)
                    def _():
                        d0 = pl.multiple_of((lo + done) * SUBLANES, SUBLANES)
                        copy_op(pltpu.make_async_copy(h_ref.at[pl.ds(0, size), :],
                                                      xs_hbm.at[pl.ds(d0, size), :], pad_sem))
                return carry

            lax.fori_loop(0, N_EXPERTS, per_expert, 0)

        run(lambda cp: cp.start())
        run(lambda cp: cp.wait())

    def body(r, carry):
        _row_copy(h_ref, r, xs_hbm, dest_ref[base + r], sem).start(priority=0)
        _row_copy(h_ref, r, xs_hbm, dest_ref[t_rows + base + r], sem).start(priority=1)
        return carry

    lax.fori_loop(0, tm, body, 0, unroll=8)
    whole = pltpu.make_async_copy(h_ref, xs_hbm.at[pl.ds(0, tm * SUBLANES), :], sem)
    whole.wait()
    whole.wait()


def _dispatch(dest, pad_lo, pad_hi, h2, t_rows, p_rows):
    tm = 1024
    grid_spec = pltpu.PrefetchScalarGridSpec(
        num_scalar_prefetch=3,
        grid=(t_rows // tm,),
        in_specs=[pl.BlockSpec((tm * SUBLANES, LANES), lambda i, *_: (i, 0))],
        out_specs=pl.BlockSpec(memory_space=pl.ANY),
        scratch_shapes=[pltpu.SemaphoreType.DMA, pltpu.SemaphoreType.DMA],
    )
    return pl.pallas_call(
        functools.partial(_dispatch_kernel, tm, t_rows),
        out_shape=jax.ShapeDtypeStruct((p_rows * SUBLANES, LANES), F32),
        grid_spec=grid_spec,
        compiler_params=_params(("arbitrary",)),
        name="moe_dispatch",
    )(dest, pad_lo, pad_hi, h2)


def _combine_kernel(tm, t_rows, n_lat_tiles_per_b, n_b, final, dest_ref, x1_ref, wt_ref, mod_ref, fw_ref,
                    ys_hbm, o_ref, buf, sem):
    i = pl.program_id(0)
    n_steps = pl.num_programs(0)
    d = x1_ref.shape[1]
    n_lt = d // LANES

    def issue(step, slot):
        base = step * tm

        def body(r, carry):
            _row_copy(ys_hbm, dest_ref[base + r], buf.at[slot, 0], r, sem.at[slot]).start(priority=0)
            _row_copy(ys_hbm, dest_ref[t_rows + base + r], buf.at[slot, 1], r, sem.at[slot]).start(priority=1)
            return carry

        lax.fori_loop(0, tm, body, 0, unroll=8)

    @pl.when(i == 0)
    def _():
        issue(0, 0)

    @pl.when(i + 1 < n_steps)
    def _():
        issue(i + 1, (i + 1) % 2)

    slot = i % 2
    for k in range(2):
        pltpu.make_async_copy(ys_hbm.at[pl.ds(0, tm * n_lt), :], buf.at[slot, k], sem.at[slot]).wait()
    bi = jnp.minimum(i // n_lat_tiles_per_b, n_b)
    g2 = mod_ref[0, pl.ds(bi, 1), 5 * d:6 * d]
    wt = wt_ref[...]
    y = (_load_token_tiles(buf.at[slot, 0], tm, n_lt) * wt[:, 0:1]
         + _load_token_tiles(buf.at[slot, 1], tm, n_lt) * wt[:, 1:2])
    out = x1_ref[...] + g2 * y
    if final:
        out = out * lax.rsqrt(jnp.mean(out * out, axis=-1, keepdims=True) + EPS) * fw_ref[...]
    o_ref[...] = out


def _combine(dest, x1, wts_t, mod, layer, final_w, ys, n_b, seq, final):
    t_rows, d = x1.shape
    tm = 512
    n_lt = d // LANES
    grid_spec = pltpu.PrefetchScalarGridSpec(
        num_scalar_prefetch=1,
        grid=(t_rows // tm,),
        in_specs=[pl.BlockSpec((tm, d), lambda i, *_: (i, 0)),
                  pl.BlockSpec((tm, 2), lambda i, *_: (i, 0)),
                  pl.BlockSpec((1,) + mod.shape[1:], lambda i, *_: (layer, 0, 0)),
                  pl.BlockSpec((1, d), lambda i, *_: (0, 0)),
                  pl.BlockSpec(memory_space=pl.ANY)],
        out_specs=pl.BlockSpec((tm, d), lambda i, *_: (i, 0)),
        scratch_shapes=[pltpu.VMEM((2, 2, tm * n_lt, LANES), F32), pltpu.SemaphoreType.DMA((2,))],
    )
    return pl.pallas_call(
        functools.partial(_combine_kernel, tm, t_rows, seq // tm, n_b, final),
        out_shape=jax.ShapeDtypeStruct((t_rows, d), F32),
        grid_spec=grid_spec,
        compiler_params=_params(("arbitrary",)),
        name="moe_combine",
    )(dest, x1, wts_t, mod, final_w, ys)


def _ffn_kernel(n_lt, be_ref, x_ref, wg_ref, wu_ref, wd_ref, o_ref, wg_s, wu_s, wd_s):
    i = pl.program_id(0)
    prev_expert = be_ref[jnp.maximum(i - 1, 0)]

    @pl.when(jnp.logical_or(i == 0, be_ref[i] != prev_expert))
    def _():
        wg_s[...] = wg_ref[0, 0].astype(BF16)
        wu_s[...] = wu_ref[0, 0].astype(BF16)
        wd_s[...] = wd_ref[0, 0].astype(BF16)

    x = _load_token_tiles(x_ref, MOE_BLOCK, n_lt).astype(BF16)
    gate = jnp.dot(x, wg_s[...], preferred_element_type=F32)
    up = jnp.dot(x, wu_s[...], preferred_element_type=F32)
    hid = (_silu(gate) * up).astype(BF16)
    _store_token_tiles(o_ref, jnp.dot(hid, wd_s[...], preferred_element_type=F32))


def _expert_ffn(blk_expert, xsorted, w_gate, w_up, w_down, layer):
    d, de = w_gate.shape[-2:]
    n_lt = d // LANES
    n_blk = xsorted.shape[0] // (MOE_BLOCK * n_lt)
    grid_spec = pltpu.PrefetchScalarGridSpec(
        num_scalar_prefetch=1,
        grid=(n_blk,),
        in_specs=[pl.BlockSpec((MOE_BLOCK * n_lt, LANES), lambda i, be: (i, 0)),
                  pl.BlockSpec((1, 1, d, de), lambda i, be: (layer, be[i], 0, 0)),
                  pl.BlockSpec((1, 1, d, de), lambda i, be: (layer, be[i], 0, 0)),
                  pl.BlockSpec((1, 1, de, d), lambda i, be: (layer, be[i], 0, 0))],
        out_specs=pl.BlockSpec((MOE_BLOCK * n_lt, LANES), lambda i, be: (i, 0)),
        scratch_shapes=[pltpu.VMEM((d, de), BF16), pltpu.VMEM((d, de), BF16), pltpu.VMEM((de, d), BF16)],
    )
    return pl.pallas_call(
        functools.partial(_ffn_kernel, n_lt),
        out_shape=jax.ShapeDtypeStruct(xsorted.shape, F32),
        grid_spec=grid_spec,
        compiler_params=_params(("arbitrary",)),
        name="expert_ffn",
    )(blk_expert, xsorted, w_gate, w_up, w_down)


def _sincos_2d(length, dim):
    rows = length // GRID_W
    quarter = dim // 4
    omega = 1.0 / (POS_BASE ** (jnp.arange(quarter, dtype=F32) / quarter))
    ang_r = jnp.arange(rows, dtype=F32)[:, None] * omega
    ang_c = jnp.arange(GRID_W, dtype=F32)[:, None] * omega
    emb_r = jnp.concatenate([jnp.sin(ang_r), jnp.cos(ang_r)], axis=-1)
    emb_c = jnp.concatenate([jnp.sin(ang_c), jnp.cos(ang_c)], axis=-1)
    half = dim // 2
    emb = jnp.concatenate([jnp.broadcast_to(emb_r[:, None, :], (rows, GRID_W, half)),
                           jnp.broadcast_to(emb_c[None, :, :], (rows, GRID_W, half))], axis=-1)
    return emb.reshape(rows * GRID_W, dim)


def _moe(h2, x1, idx, wts, counts, mod, layer, final_w, w_gate, w_up, w_down, n_b, seq, final):
    t_rows, d = x1.shape
    a = 2 * t_rows
    counts = counts[:, 0]
    padded = (counts + MOE_BLOCK - 1) // MOE_BLOCK * MOE_BLOCK
    pad_end = jnp.cumsum(padded)
    pad_start = pad_end - padded
    n_blk = (a + N_EXPERTS * (MOE_BLOCK - 1) + MOE_BLOCK - 1) // MOE_BLOCK
    p_rows = n_blk * MOE_BLOCK
    blk_start = jnp.arange(n_blk, dtype=jnp.int32) * MOE_BLOCK
    blk_expert = jnp.minimum(jnp.sum(blk_start[:, None] >= pad_end[None, :], axis=1),
                             N_EXPERTS - 1).astype(jnp.int32)
    e_ids = jnp.arange(N_EXPERTS, dtype=jnp.int32)
    seg_start = jnp.sum(jnp.where(idx[0:2, :, None] == e_ids, pad_start, 0), axis=-1)
    dest = (seg_start + idx[2:4]).reshape(-1).astype(jnp.int32)
    pad_lo = (pad_start + counts).astype(jnp.int32)
    pad_hi = pad_end.at[N_EXPERTS - 1].set(p_rows).astype(jnp.int32)
    xsorted = _dispatch(dest, pad_lo, pad_hi, h2, t_rows, p_rows)
    ys = _expert_ffn(blk_expert, xsorted, w_gate, w_up, w_down, layer)
    return _combine(dest, x1, wts.T, mod, layer, final_w, ys, n_b, seq, final)


def kernel(x, c, ctx, c_ctx, ada_w, ada_b, norm_mix_w, norm_ffn_w, w_in, conv_w, a_log, dt_bias,
           dn_norm_w, ft_norm_w, w_out, router_w, router_bias, w_gate, w_up, w_down, final_norm_w):
    n_b, seq, d = x.shape
    ctx_len = ctx.shape[1]
    depth = ada_w.shape[0]
    n_lat = n_b * seq
    n_tok = n_lat + n_b * ctx_len
    ft_off = 4 * DN_WIDTH + N_GATE_COLS

    xs = _embed(x.reshape(n_lat, d), _sincos_2d(seq, d), ctx.reshape(n_b * ctx_len, d))

    mod_rows = -(-(n_b + 1) // 8) * 8
    cvec = jnp.zeros((mod_rows, d), F32).at[:n_b].set(c).at[n_b].set(c_ctx)
    mod = _ada(cvec, ada_w, ada_b)

    cos_l, sin_l = _dft_tables(seq, seq // 2)
    cos_c, sin_c = _dft_tables(ctx_len, ctx_len // 2)
    cc, sc = _dft_tables(FT_GROUP_DIM)
    dft_cs = jnp.concatenate([cc, sc], axis=1)
    assert d == SUBLANES * LANES, "token tiles assume one (8, 128) tile per token"
    final_w = final_norm_w.reshape(1, d)
    router_wt = router_w.T
    rbias = router_bias.reshape(N_EXPERTS, 1)

    col = jnp.arange(N_GATE_COLS)
    col_head, col_dir, col_ab = col // 4, (col // 2) % 2, col % 2
    gate_src = col_dir * (2 * N_DN_HEADS) + col_ab * N_DN_HEADS + col_head
    is_a = (col_ab == 0).astype(F32)
    is_bwd = (col_dir == 1).astype(F32)

    gate_w = w_in[:, :, 4 * DN_WIDTH:ft_off][:, :, gate_src]
    w_pack = _pack_in_weights(w_in)
    wabt = jnp.swapaxes(gate_w, 1, 2).astype(BF16)
    neg_a = -jnp.exp(a_log)[:, col_dir, col_head] * is_a
    dtb = dt_bias[:, col_dir, col_head] * is_a
    gpart = jnp.stack([neg_a, dtb, jnp.broadcast_to(is_a, neg_a.shape),
                       jnp.broadcast_to(is_bwd, neg_a.shape)], axis=2)
    w_out_b = w_out.astype(BF16)
    norm_mix = norm_mix_w.reshape(depth, 1, d)

    for i in range(depth):
        last = i == depth - 1
        qkv, z, xc, xsn, gcol, grow = _in_proj(xs, mod, i, norm_mix, w_pack, wabt, gpart, dft_cs, n_b, seq)
        dn_l, dn_c = _deltanet(qkv, z, gcol, grow, conv_w[i], dn_norm_w[i].reshape(1, DN_HEAD_DIM),
                               n_b, seq, ctx_len)
        fnw = ft_norm_w[i].reshape(1, FT_WIDTH)
        ft_l = _fourier(xc, xsn, cos_l, sin_l, fnw, n_b, seq, 0)
        if last:
            ft_c, rows = dn_c, n_lat
        else:
            ft_c = _fourier(xc, xsn, cos_c, sin_c, fnw, n_b, ctx_len, n_lat // ctx_len)
            rows = n_tok
        x1, h2, logits_t = _out_proj(xs, dn_l, dn_c, ft_l, ft_c, w_out_b, mod, i,
                                     norm_ffn_w[i].reshape(1, d), router_wt, n_b, seq, rows)
        idx, wts, counts = _route(logits_t, rbias)
        xs = _moe(h2, x1, idx, wts, counts, mod, i, final_w, w_gate, w_up, w_down, n_b, seq, last)
    return xs.reshape(n_b, seq, d)
```
